```python
import jax, jax.numpy as jnp
from jax import lax
import numpy as np

D_MODEL = 1024
BATCH = 8
SEQ = 4096
DEPTH = 4

N_MEM = 256
HEAD_DIM = 64
N_MIX_HEADS = 12
MIX_WIDTH = N_MIX_HEADS * HEAD_DIM
N_MEM_HEADS = 4
MEM_WIDTH = N_MEM_HEADS * HEAD_DIM
D_FF = 2816
CHUNK = 128
BLOCK_Q = 128
ROPE_THETA = 10000.0
DILATED_GROUPS = ((128, 1), (512, 4), (2048, 16))
HEADS_PER_DIL_GROUP = N_MIX_HEADS // len(DILATED_GROUPS)
DIL_OUT_WIDTH = HEADS_PER_DIL_GROUP * HEAD_DIM
RMS_EPS = 1e-6
LN_EPS = 1e-5
NEG_INF = -1e30
ATTN_SCALE = HEAD_DIM ** -0.5
MAX_POS_OFFSET = 1024
N_NORMS = 6

N_A = (DEPTH + 2) // 3
N_B = (DEPTH + 1) // 3
N_C = DEPTH // 3
A_IN = 2 * MIX_WIDTH + MEM_WIDTH
A_OUT = MIX_WIDTH + MEM_WIDTH
B_IN = 3 * MIX_WIDTH + MEM_WIDTH
B_OUT = DIL_OUT_WIDTH + MEM_WIDTH
C_IN = 3 * MIX_WIDTH + N_MIX_HEADS + MEM_WIDTH
C_OUT = MIX_WIDTH + MEM_WIDTH

kernel_name = 'hybrid_interleaved_gmlp_dilated_fox_macaron'


def rms_norm(x, g):
    xf = x.astype(jnp.float32)
    y = xf * lax.rsqrt(jnp.mean(xf * xf, axis=-1, keepdims=True) + RMS_EPS)
    return (y * g.astype(jnp.float32)).astype(x.dtype)


def layer_norm(x, g):
    xf = x.astype(jnp.float32)
    mu = jnp.mean(xf, axis=-1, keepdims=True)
    var = jnp.mean(jnp.square(xf - mu), axis=-1, keepdims=True)
    return ((xf - mu) * lax.rsqrt(var + LN_EPS) * g.astype(jnp.float32)).astype(x.dtype)


def swiglu_ffn(x, w_gate_up, w_down):
    gate, up = jnp.split(x @ w_gate_up, 2, axis=-1)
    return (jax.nn.silu(gate) * up) @ w_down


def rope_tables(positions):
    inv_freq = ROPE_THETA ** (-jnp.arange(0, HEAD_DIM, 2, dtype=jnp.float32) / HEAD_DIM)
    ang = positions.astype(jnp.float32)[..., None] * inv_freq
    return jnp.cos(ang), jnp.sin(ang)


def apply_rope(t, cos, sin):
    c = cos[:, :, None, :].astype(t.dtype)
    s = sin[:, :, None, :].astype(t.dtype)
    t1, t2 = jnp.split(t, 2, axis=-1)
    return jnp.concatenate([t1 * c - t2 * s, t2 * c + t1 * s], axis=-1)


def memory_attention(q_mem, mem_n, w_mem_kv):
    B, S, _ = q_mem.shape
    k, v = jnp.split(mem_n @ w_mem_kv, 2, axis=-1)
    q = q_mem.reshape(B, S, N_MEM_HEADS, HEAD_DIM)
    k = k.reshape(B, -1, N_MEM_HEADS, HEAD_DIM)
    v = v.reshape(B, -1, N_MEM_HEADS, HEAD_DIM)
    s = jnp.einsum('bqhd,bkhd->bhqk', q, k).astype(jnp.float32) * ATTN_SCALE
    p = jax.nn.softmax(s, axis=-1).astype(v.dtype)
    return jnp.einsum('bhqk,bkhd->bqhd', p, v).reshape(B, S, MEM_WIDTH)


def mixer_a(h, mem_n, w_mem_kv, w_in, spatial_w, spatial_b, v_norm_g, w_out):
    B, S, _ = h.shape
    proj = h @ w_in
    uv, q_mem = proj[..., :2 * MIX_WIDTH], proj[..., 2 * MIX_WIDTH:]
    u, v = jnp.split(jax.nn.gelu(uv), 2, axis=-1)
    v = layer_norm(v, v_norm_g).reshape(B, S // CHUNK, CHUNK, N_MIX_HEADS, HEAD_DIM)
    w_causal = spatial_w * jnp.tril(jnp.ones((CHUNK, CHUNK), spatial_w.dtype))
    mixed = jnp.einsum('gts,bcsgd->bctgd', w_causal, v) + spatial_b.T[None, None, :, :, None]
    gated = u * mixed.reshape(B, S, MIX_WIDTH)
    y = jnp.concatenate([gated, memory_attention(q_mem, mem_n, w_mem_kv)], axis=-1)
    return y @ w_out


def dilated_window_attention(q, k, v, window, dilation):
    B, S, H, Dh = q.shape
    span = window // dilation
    L = S // dilation
    nb = -(-L // span)
    Lp = nb * span

    def to_blocks(t):
        t = t.reshape(B, L, dilation, H, Dh).transpose(0, 2, 1, 3, 4)
        t = jnp.pad(t, ((0, 0), (0, 0), (0, Lp - L), (0, 0), (0, 0)))
        return t.reshape(B, dilation, nb, span, H, Dh)

    def with_prev(t):
        prev = jnp.pad(t, ((0, 0), (0, 0), (1, 0), (0, 0), (0, 0), (0, 0)))[:, :, :-1]
        return jnp.concatenate([prev, t], axis=3)

    qb = to_blocks(q)
    kk = with_prev(to_blocks(k))
    vv = with_prev(to_blocks(v))
    s = jnp.einsum('brnqhd,brnkhd->brnhqk', qb, kk).astype(jnp.float32) * ATTN_SCALE
    qi = jnp.arange(span)[:, None] + span
    ki = jnp.arange(2 * span)[None, :]
    dist = qi - ki
    band = (dist >= 0) & (dist <= span)
    first = jnp.arange(nb)[:, None, None] == 0
    valid = band[None] & ~(first & (ki < span)[None])
    s = jnp.where(valid[None, None, :, None], s, NEG_INF)
    m = jnp.max(s, axis=-1, keepdims=True)
    p = jnp.exp(s - m)
    denom = jnp.sum(p, axis=-1, keepdims=True)
    out = jnp.einsum('brnhqk,brnkhd->brnqhd', (p / denom).astype(v.dtype), vv)
    lse = (m + jnp.log(denom))[..., 0]
    out = out.reshape(B, dilation, Lp, H, Dh)[:, :, :L].transpose(0, 2, 1, 3, 4).reshape(B, S, H, Dh)
    lse = lse.transpose(0, 1, 2, 4, 3).reshape(B, dilation, Lp, H)[:, :, :L]
    lse = lse.transpose(0, 2, 1, 3).reshape(B, S, H)
    return out, lse


def mixer_b(h, mem_n, w_mem_kv, cos, sin, w_in, w_out):
    B, S, _ = h.shape
    q, k, v, q_mem = jnp.split(h @ w_in, [MIX_WIDTH, 2 * MIX_WIDTH, 3 * MIX_WIDTH], axis=-1)
    q = apply_rope(q.reshape(B, S, N_MIX_HEADS, HEAD_DIM), cos, sin)
    k = apply_rope(k.reshape(B, S, N_MIX_HEADS, HEAD_DIM), cos, sin)
    v = v.reshape(B, S, N_MIX_HEADS, HEAD_DIM)
    outs, lses = [], []
    for g, (window, dilation) in enumerate(DILATED_GROUPS):
        hs = slice(g * HEADS_PER_DIL_GROUP, (g + 1) * HEADS_PER_DIL_GROUP)
        o, l = dilated_window_attention(q[:, :, hs], k[:, :, hs], v[:, :, hs], window, dilation)
        outs.append(o)
        lses.append(l)
    wts = jax.nn.softmax(jnp.stack(lses, axis=0), axis=0)
    merged = jnp.sum(wts[..., None].astype(v.dtype) * jnp.stack(outs, axis=0), axis=0)
    merged = merged.reshape(B, S, DIL_OUT_WIDTH)
    y = jnp.concatenate([merged, memory_attention(q_mem, mem_n, w_mem_kv)], axis=-1)
    return y @ w_out


def forgetting_attention(q, k, v, log_f):
    B, S, H, Dh = q.shape
    c = jnp.cumsum(log_f.astype(jnp.float32), axis=1).transpose(0, 2, 1)
    q_idx = jnp.arange(BLOCK_Q)
    outs = []
    for i in range(S // BLOCK_Q):
        lo, hi = i * BLOCK_Q, (i + 1) * BLOCK_Q
        s = jnp.einsum('bqhd,bkhd->bhqk', q[:, lo:hi], k[:, :hi]).astype(jnp.float32) * ATTN_SCALE
        s = s + c[:, :, lo:hi, None] - c[:, :, None, :hi]
        causal = (lo + q_idx)[:, None] >= jnp.arange(hi)[None, :]
        p = jax.nn.softmax(jnp.where(causal, s, NEG_INF), axis=-1).astype(v.dtype)
        outs.append(jnp.einsum('bhqk,bkhd->bqhd', p, v[:, :hi]))
    return jnp.concatenate(outs, axis=1)


def mixer_c(h, mem_n, w_mem_kv, w_in, forget_bias, w_out):
    B, S, _ = h.shape
    q, k, v, f_logit, q_mem = jnp.split(
        h @ w_in, [MIX_WIDTH, 2 * MIX_WIDTH, 3 * MIX_WIDTH, 3 * MIX_WIDTH + N_MIX_HEADS], axis=-1)
    log_f = jax.nn.log_sigmoid(f_logit.astype(jnp.float32) + forget_bias.astype(jnp.float32))
    heads = lambda t: t.reshape(B, S, N_MIX_HEADS, HEAD_DIM)
    att = forgetting_attention(heads(q), heads(k), heads(v), log_f).reshape(B, S, MIX_WIDTH)
    y = jnp.concatenate([att, memory_attention(q_mem, mem_n, w_mem_kv)], axis=-1)
    return y @ w_out


def setup_inputs(seed: int = 0) -> dict:
    key = jax.random.key(seed)
    ks = jax.random.split(key, 20)
    f32 = jnp.float32

    def dense(k, shape, fan_in):
        return jax.random.normal(k, shape, f32) * fan_in ** -0.5

    def gain(k, shape):
        return 1.0 + 0.02 * jax.random.normal(k, shape, f32)

    x = jax.random.normal(ks[0], (BATCH, SEQ, D_MODEL), f32)
    mem = jax.random.normal(ks[1], (BATCH, N_MEM, D_MODEL), f32)
    offset = jax.random.randint(ks[2], (BATCH, 1), 0, MAX_POS_OFFSET, dtype=jnp.int32)
    positions = (offset + jnp.arange(SEQ, dtype=jnp.int32)[None, :]).astype(jnp.int32)
    return {
        'x': x,
        'mem': mem,
        'positions': positions,
        'norm_g': gain(ks[3], (DEPTH, N_NORMS, D_MODEL)),
        'mem_norm_g': gain(ks[4], (DEPTH, D_MODEL)),
        'w_mem_kv': dense(ks[5], (DEPTH, D_MODEL, 2 * MEM_WIDTH), D_MODEL),
        'ffn_w_gate_up': dense(ks[6], (DEPTH, 2, D_MODEL, 2 * D_FF), D_MODEL),
        'ffn_w_down': dense(ks[7], (DEPTH, 2, D_FF, D_MODEL), D_FF),
        'a_w_in': dense(ks[8], (N_A, D_MODEL, A_IN), D_MODEL),
        'a_spatial_w': dense(ks[9], (N_A, N_MIX_HEADS, CHUNK, CHUNK), CHUNK),
        'a_spatial_b': gain(ks[10], (N_A, N_MIX_HEADS, CHUNK)),
        'a_v_norm_g': gain(ks[11], (N_A, MIX_WIDTH)),
        'a_w_out': dense(ks[12], (N_A, A_OUT, D_MODEL), A_OUT),
        'b_w_in': dense(ks[13], (N_B, D_MODEL, B_IN), D_MODEL),
        'b_w_out': dense(ks[14], (N_B, B_OUT, D_MODEL), B_OUT),
        'c_w_in': dense(ks[15], (N_C, D_MODEL, C_IN), D_MODEL),
        'c_forget_bias': jax.random.uniform(ks[16], (N_C, N_MIX_HEADS), f32, 1.0, 5.0),
        'c_w_out': dense(ks[17], (N_C, C_OUT, D_MODEL), C_OUT),
    }


def reference(x, mem, positions, norm_g, mem_norm_g, w_mem_kv, ffn_w_gate_up, ffn_w_down,
              a_w_in, a_spatial_w, a_spatial_b, a_v_norm_g, a_w_out,
              b_w_in, b_w_out, c_w_in, c_forget_bias, c_w_out):
    cos, sin = rope_tables(positions)
    for i in range(DEPTH):
        kind, j = i % 3, i // 3
        g = norm_g[i]
        x = x + 0.5 * rms_norm(swiglu_ffn(rms_norm(x, g[0]), ffn_w_gate_up[i, 0], ffn_w_down[i, 0]), g[1])
        h = rms_norm(x, g[2])
        mem_n = rms_norm(mem, mem_norm_g[i])
        if kind == 0:
            y = mixer_a(h, mem_n, w_mem_kv[i], a_w_in[j], a_spatial_w[j], a_spatial_b[j],
                        a_v_norm_g[j], a_w_out[j])
        elif kind == 1:
            y = mixer_b(h, mem_n, w_mem_kv[i], cos, sin, b_w_in[j], b_w_out[j])
        else:
            y = mixer_c(h, mem_n, w_mem_kv[i], c_w_in[j], c_forget_bias[j], c_w_out[j])
        x = x + rms_norm(y, g[3])
        x = x + 0.5 * rms_norm(swiglu_ffn(rms_norm(x, g[4]), ffn_w_gate_up[i, 1], ffn_w_down[i, 1]), g[5])
    return x
```

```python
import functools

import jax
import jax.numpy as jnp
from jax import lax
from jax.experimental import pallas as pl
from jax.experimental.pallas import tpu as pltpu

F32 = jnp.float32
BF16 = jnp.bfloat16

HEAD_DIM = 64
N_MIX_HEADS = 12
MIX_WIDTH = N_MIX_HEADS * HEAD_DIM
N_MEM_HEADS = 4
MEM_WIDTH = N_MEM_HEADS * HEAD_DIM
CHUNK = 128
ROPE_THETA = 10000.0
DILATED_GROUPS = ((128, 1), (512, 4), (2048, 16))
DIL_OUT_WIDTH = (N_MIX_HEADS // len(DILATED_GROUPS)) * HEAD_DIM
RMS_EPS = 1e-6
LN_EPS = 1e-5
NEG_INF = -1e30
ATTN_SCALE = HEAD_DIM ** -0.5

LANES = 128
PAIR = 2 * HEAD_DIM
MXU_N = 256
VMEM_LIMIT_BYTES = 56 * 1024 * 1024

TM_FFN = 512
TM_PROJ = 512
TM_MIX_A = 256
TQ_FOX = 256
FF_CHUNK = 256


def _params(*semantics):
    return pltpu.CompilerParams(dimension_semantics=semantics,
                                vmem_limit_bytes=VMEM_LIMIT_BYTES)


def _const_spec(shape):
    zeros = (0,) * len(shape)
    return pl.BlockSpec(shape, lambda *_: zeros, pipeline_mode=pl.Buffered(1))


def _rms_norm(x, g):
    ms = jnp.mean(x * x, axis=-1, keepdims=True)
    return x * lax.rsqrt(ms + RMS_EPS) * g


def _dot(a, b):
    return jnp.dot(a, b, preferred_element_type=F32)


def _dot_nt(a, b):
    return lax.dot_general(a, b, (((1,), (1,)), ((), ())), preferred_element_type=F32)


def _first_head_lanes(shape):
    return lax.broadcasted_iota(jnp.int32, shape, len(shape) - 1) % PAIR < HEAD_DIM


def _ffn_body(x_ref, gpre_ref, gpost_ref, wgu_ref, wd_ref, o_ref, xn_ref, act_ref, *, d_ff):
    xn_ref[...] = _rms_norm(x_ref[...], gpre_ref[...]).astype(BF16)
    for c in range(d_ff // FF_CHUNK):
        lo = c * FF_CHUNK
        gate = _dot(xn_ref[...], wgu_ref[:, lo:lo + FF_CHUNK])
        up = _dot(xn_ref[...], wgu_ref[:, d_ff + lo:d_ff + lo + FF_CHUNK])
        act_ref[:, lo:lo + FF_CHUNK] = (gate * jax.nn.sigmoid(gate) * up).astype(BF16)
    y = _dot(act_ref[...], wd_ref[...])
    o_ref[...] = x_ref[...] + 0.5 * _rms_norm(y, gpost_ref[...])


def _ffn(x, g_pre, g_post, w_gate_up, w_down):
    m, d = x.shape
    d_ff = w_down.shape[0]
    assert m % TM_FFN == 0 and d_ff % FF_CHUNK == 0
    row = pl.BlockSpec((TM_FFN, d), lambda i: (i, 0))
    return pl.pallas_call(
        functools.partial(_ffn_body, d_ff=d_ff),
        grid=(m // TM_FFN,),
        in_specs=[row, _const_spec((1, d)), _const_spec((1, d)),
                  _const_spec((d, 2 * d_ff)), _const_spec((d_ff, d))],
        out_specs=row,
        out_shape=jax.ShapeDtypeStruct((m, d), F32),
        scratch_shapes=[pltpu.VMEM((TM_FFN, d), BF16), pltpu.VMEM((TM_FFN, d_ff), BF16)],
        compiler_params=_params("parallel"),
        name="ffn",
    )(x, g_pre, g_post, w_gate_up, w_down)


def _mem_kv_body(mem_ref, g_ref, w_ref, o_ref):
    o_ref[0, 0] = _dot(_rms_norm(mem_ref[0], g_ref[0]).astype(BF16), w_ref[0]).astype(BF16)


def _mem_kv(mem, mem_norm_g, w_mem_kv):
    b, n_mem, d = mem.shape
    depth = w_mem_kv.shape[0]
    return pl.pallas_call(
        _mem_kv_body,
        grid=(depth, b),
        in_specs=[pl.BlockSpec((1, n_mem, d), lambda l, i: (i, 0, 0)),
                  pl.BlockSpec((1, 1, d), lambda l, i: (l, 0, 0)),
                  pl.BlockSpec((1, d, 2 * MEM_WIDTH), lambda l, i: (l, 0, 0))],
        out_specs=pl.BlockSpec((1, 1, n_mem, 2 * MEM_WIDTH), lambda l, i: (l, i, 0, 0)),
        out_shape=jax.ShapeDtypeStruct((depth, b, n_mem, 2 * MEM_WIDTH), BF16),
        compiler_params=_params("parallel", "parallel"),
        name="mem_kv",
    )(mem, mem_norm_g.reshape(depth, 1, d), w_mem_kv)


def _mem_attention(qm, kv_ref):
    outs = []
    for p in range(MEM_WIDTH // PAIR):
        lanes = slice(p * PAIR, (p + 1) * PAIR)
        qp = qm[:, lanes]
        k = kv_ref[0, :, p * PAIR:(p + 1) * PAIR]
        v = kv_ref[0, :, MEM_WIDTH + p * PAIR:MEM_WIDTH + (p + 1) * PAIR]
        first = _first_head_lanes(qp.shape)
        heads = []
        for h in range(2):
            qh = jnp.where(first if h == 0 else ~first, qp, jnp.zeros_like(qp))
            s = _dot_nt(qh, k)
            e = jnp.exp(s - jnp.max(s, axis=-1, keepdims=True))
            denom = jnp.sum(e, axis=-1, keepdims=True)
            heads.append(_dot(e.astype(BF16), v) / denom)
        outs.append(jnp.where(first, heads[0], heads[1]))
    return jnp.concatenate(outs, axis=-1)


def _finish_mixer(x_ref, y, g_ref, o_ref):
    o_ref[...] = x_ref[...] + _rms_norm(y, g_ref[...])


def _mixer_a_body(x_ref, gpre_ref, win_ref, vg_ref, sw_ref, sb_ref, kv_ref, wout_ref, gpost_ref,
                  o_ref, u_ref, v_ref, gated_ref, wc_ref):
    tm = x_ref.shape[0]
    h = _rms_norm(x_ref[...], gpre_ref[...]).astype(BF16)
    u_ref[...] = jax.nn.gelu(_dot(h, win_ref[:, :MIX_WIDTH]))
    v = jax.nn.gelu(_dot(h, win_ref[:, MIX_WIDTH:2 * MIX_WIDTH]))
    mu = jnp.mean(v, axis=-1, keepdims=True)
    vc = v - mu
    var = jnp.mean(vc * vc, axis=-1, keepdims=True)
    v_ref[...] = (vc * lax.rsqrt(var + LN_EPS) * vg_ref[...]).astype(BF16)
    qm = (_dot(h, win_ref[:, 2 * MIX_WIDTH:]) * ATTN_SCALE).astype(BF16)

    t_idx = lax.broadcasted_iota(jnp.int32, (CHUNK, CHUNK), 0)
    s_idx = lax.broadcasted_iota(jnp.int32, (CHUNK, CHUNK), 1)
    for g in range(N_MIX_HEADS):
        wc_ref[g] = jnp.where(s_idx <= t_idx, sw_ref[g], jnp.zeros((CHUNK, CHUNK), BF16))

    first = _first_head_lanes((CHUNK, PAIR))
    for c in range(tm // CHUNK):
        rows = slice(c * CHUNK, (c + 1) * CHUNK)
        for p in range(MIX_WIDTH // PAIR):
            lanes = slice(p * PAIR, (p + 1) * PAIR)
            vp = v_ref[rows, lanes]
            mixed = jnp.where(first, _dot(wc_ref[2 * p], vp), _dot(wc_ref[2 * p + 1], vp))
            gated_ref[rows, lanes] = (u_ref[rows, lanes] * (mixed + sb_ref[:, lanes])).astype(BF16)

    mem_out = _mem_attention(qm, kv_ref).astype(BF16)
    y = _dot(gated_ref[...], wout_ref[:MIX_WIDTH, :]) + _dot(mem_out, wout_ref[MIX_WIDTH:, :])
    _finish_mixer(x_ref, y, gpost_ref, o_ref)


def _mixer_a(x, seq, g_pre, g_post, kv, w_in, spatial_w, spatial_b, v_norm_g, w_out):
    m, d = x.shape
    tm = TM_MIX_A
    assert seq % tm == 0 and tm % CHUNK == 0
    tiles_per_seq = seq // tm
    n_mem = kv.shape[1]
    row = pl.BlockSpec((tm, d), lambda i: (i, 0))
    bias = jnp.repeat(spatial_b.T, HEAD_DIM, axis=1)
    return pl.pallas_call(
        _mixer_a_body,
        grid=(m // tm,),
        in_specs=[row, _const_spec((1, d)), _const_spec(w_in.shape), _const_spec((1, MIX_WIDTH)),
                  _const_spec(spatial_w.shape), _const_spec(bias.shape),
                  pl.BlockSpec((1, n_mem, 2 * MEM_WIDTH), lambda i: (i // tiles_per_seq, 0, 0)),
                  _const_spec(w_out.shape), _const_spec((1, d))],
        out_specs=row,
        out_shape=jax.ShapeDtypeStruct((m, d), F32),
        scratch_shapes=[pltpu.VMEM((tm, MIX_WIDTH), F32), pltpu.VMEM((tm, MIX_WIDTH), BF16),
                        pltpu.VMEM((tm, MIX_WIDTH), BF16),
                        pltpu.VMEM((N_MIX_HEADS, CHUNK, CHUNK), BF16)],
        compiler_params=_params("parallel"),
        name="mixer_a",
    )(x, g_pre, w_in, v_norm_g.reshape(1, MIX_WIDTH), spatial_w, bias, kv, w_out, g_post)


def _rope_body(pos_ref, freq_ref, sign_ref, cos_ref, sin_ref):
    ang = pos_ref[...].astype(F32) * freq_ref[...]
    cos_ref[...] = jnp.cos(ang)
    sin_ref[...] = jnp.sin(ang) * sign_ref[...]


def _rope_tables(positions):
    m = positions.size
    tm = TM_PROJ
    inv_freq = ROPE_THETA ** (-jnp.arange(0, HEAD_DIM, 2, dtype=F32) / HEAD_DIM)
    freq = jnp.tile(inv_freq, LANES // (HEAD_DIM // 2)).reshape(1, LANES)
    half = jnp.concatenate([-jnp.ones((HEAD_DIM // 2,), F32), jnp.ones((HEAD_DIM // 2,), F32)])
    sign = jnp.tile(half, LANES // HEAD_DIM).reshape(1, LANES)
    table = pl.BlockSpec((tm, LANES), lambda i: (i, 0))
    return pl.pallas_call(
        _rope_body,
        grid=(m // tm,),
        in_specs=[pl.BlockSpec((tm, 1), lambda i: (i, 0)), _const_spec((1, LANES)),
                  _const_spec((1, LANES))],
        out_specs=[table, table],
        out_shape=[jax.ShapeDtypeStruct((m, LANES), F32)] * 2,
        compiler_params=_params("parallel"),
        name="rope_tables",
    )(positions.reshape(m, 1), freq, sign)


def _rope(t, cos, sin_signed):
    first_half = lax.broadcasted_iota(jnp.int32, t.shape, 1) % HEAD_DIM < HEAD_DIM // 2
    swapped = jnp.where(first_half,
                        pltpu.roll(t, LANES - HEAD_DIM // 2, axis=1),
                        pltpu.roll(t, HEAD_DIM // 2, axis=1))
    return t * cos + swapped * sin_signed


def _proj_b_body(x_ref, g_ref, w_ref, cos_ref, sin_ref, qkv_ref, qm_ref, h_ref):
    h_ref[...] = _rms_norm(x_ref[...], g_ref[...]).astype(BF16)
    cos = cos_ref[...]
    sin = sin_ref[...]
    for c in range(3 * MIX_WIDTH // MXU_N):
        lo = c * MXU_N
        t = _dot(h_ref[...], w_ref[:, lo:lo + MXU_N])
        if lo < 2 * MIX_WIDTH:
            scale = ATTN_SCALE if lo < MIX_WIDTH else 1.0
            t = jnp.concatenate(
                [_rope(t[:, j * LANES:(j + 1) * LANES], cos, sin) for j in range(MXU_N // LANES)],
                axis=-1) * scale
        qkv_ref[:, lo:lo + MXU_N] = t.astype(BF16)
    qm_ref[...] = (_dot(h_ref[...], w_ref[:, 3 * MIX_WIDTH:]) * ATTN_SCALE).astype(BF16)


def _proj_b(x, g_pre, w_in, cos, sin):
    m, d = x.shape
    tm = TM_PROJ
    row = lambda w: pl.BlockSpec((tm, w), lambda i: (i, 0))
    return pl.pallas_call(
        _proj_b_body,
        grid=(m // tm,),
        in_specs=[row(d), _const_spec((1, d)), _const_spec(w_in.shape), row(LANES), row(LANES)],
        out_specs=[row(3 * MIX_WIDTH), row(MEM_WIDTH)],
        out_shape=[jax.ShapeDtypeStruct((m, 3 * MIX_WIDTH), BF16),
                   jax.ShapeDtypeStruct((m, MEM_WIDTH), BF16)],
        scratch_shapes=[pltpu.VMEM((tm, d), BF16)],
        compiler_params=_params("parallel"),
        name="proj_b",
    )(x, g_pre, w_in, cos, sin)


def _dilated_body(q_ref, k_ref, v_ref, o_ref, lse_ref, *, n_blocks, span):
    row = lax.broadcasted_iota(jnp.int32, (span, span), 0)
    col = lax.broadcasted_iota(jnp.int32, (span, span), 1)
    own_ok = col <= row
    prev_ok = col >= row
    first = _first_head_lanes((span, PAIR))

    def block(r0, has_prev):
        own = pl.ds(r0, span)
        prev = pl.ds(r0 - span, span)
        for p in range(DIL_OUT_WIDTH // PAIR):
            lanes = slice(p * PAIR, (p + 1) * PAIR)
            qp = q_ref[0, own, lanes]
            k_own = k_ref[0, own, lanes]
            v_own = v_ref[0, own, lanes]
            if has_prev:
                k_prev = k_ref[0, prev, lanes]
                v_prev = v_ref[0, prev, lanes]
            outs, lses = [], []
            for h in range(2):
                qh = jnp.where(first if h == 0 else ~first, qp, jnp.zeros_like(qp))
                s_own = jnp.where(own_ok, _dot_nt(qh, k_own), NEG_INF)
                mx = jnp.max(s_own, axis=-1, keepdims=True)
                if has_prev:
                    s_prev = jnp.where(prev_ok, _dot_nt(qh, k_prev), NEG_INF)
                    mx = jnp.maximum(mx, jnp.max(s_prev, axis=-1, keepdims=True))
                e_own = jnp.exp(s_own - mx)
                denom = jnp.sum(e_own, axis=-1, keepdims=True)
                acc = _dot(e_own.astype(BF16), v_own)
                if has_prev:
                    e_prev = jnp.exp(s_prev - mx)
                    denom = denom + jnp.sum(e_prev, axis=-1, keepdims=True)
                    acc = acc + _dot(e_prev.astype(BF16), v_prev)
                outs.append(acc / denom)
                lses.append(jnp.broadcast_to(mx + jnp.log(denom), (span, PAIR)))
            o_ref[0, own, lanes] = jnp.where(first, outs[0], outs[1]).astype(BF16)
            lse_ref[0, own, lanes] = jnp.where(first, lses[0], lses[1])

    block(0, False)

    def step(blk, carry):
        block(pl.multiple_of(blk * span, span), True)
        return carry

    lax.fori_loop(1, n_blocks, step, 0)


def _dilated_group(qkv, group, window, dilation):
    b, s, w = qkv.shape
    span = window // dilation
    length = s // dilation
    assert s % dilation == 0 and length % span == 0
    cols = w // DIL_OUT_WIDTH
    third = MIX_WIDTH // DIL_OUT_WIDTH
    view = qkv.reshape(b, length, dilation * w)
    spec = lambda off: pl.BlockSpec((1, length, DIL_OUT_WIDTH),
                                    lambda i, r: (i, 0, r * cols + off + group))
    out_spec = pl.BlockSpec((1, length, DIL_OUT_WIDTH), lambda i, r: (i, 0, r))
    out, lse = pl.pallas_call(
        functools.partial(_dilated_body, n_blocks=length // span, span=span),
        grid=(b, dilation),
        in_specs=[spec(0), spec(third), spec(2 * third)],
        out_specs=[out_spec, out_spec],
        out_shape=[jax.ShapeDtypeStruct((b, length, dilation * DIL_OUT_WIDTH), BF16),
                   jax.ShapeDtypeStruct((b, length, dilation * DIL_OUT_WIDTH), F32)],
        compiler_params=_params("parallel", "parallel"),
        name=f"dilated_{dilation}",
    )(view, view, view)
    return out.reshape(b * s, DIL_OUT_WIDTH), lse.reshape(b * s, DIL_OUT_WIDTH)


def _out_b_body(x_ref, o1_ref, o2_ref, o3_ref, l1_ref, l2_ref, l3_ref, qm_ref, kv_ref, w_ref,
                g_ref, o_ref):
    lses = [l1_ref[...], l2_ref[...], l3_ref[...]]
    mx = jnp.maximum(jnp.maximum(lses[0], lses[1]), lses[2])
    es = [jnp.exp(l - mx) for l in lses]
    denom = es[0] + es[1] + es[2]
    merged = sum((e / denom) * o[...].astype(F32) for e, o in zip(es, (o1_ref, o2_ref, o3_ref)))
    mem_out = _mem_attention(qm_ref[...], kv_ref).astype(BF16)
    y = (_dot(merged.astype(BF16), w_ref[:DIL_OUT_WIDTH, :])
         + _dot(mem_out, w_ref[DIL_OUT_WIDTH:, :]))
    _finish_mixer(x_ref, y, g_ref, o_ref)


def _out_b(x, seq, outs, lses, qm, kv, w_out, g_post):
    m, d = x.shape
    tm = TM_PROJ
    tiles_per_seq = seq // tm
    n_mem = kv.shape[1]
    row = lambda w: pl.BlockSpec((tm, w), lambda i: (i, 0))
    return pl.pallas_call(
        _out_b_body,
        grid=(m // tm,),
        in_specs=[row(d)] + [row(DIL_OUT_WIDTH)] * 6 + [
            row(MEM_WIDTH),
            pl.BlockSpec((1, n_mem, 2 * MEM_WIDTH), lambda i: (i // tiles_per_seq, 0, 0)),
            _const_spec(w_out.shape), _const_spec((1, d))],
        out_specs=row(d),
        out_shape=jax.ShapeDtypeStruct((m, d), F32),
        compiler_params=_params("parallel"),
        name="out_b",
    )(x, *outs, *lses, qm, kv, w_out, g_post)


def _split3(x):
    hi = x.astype(BF16)
    r1 = x - hi.astype(F32)
    mid = r1.astype(BF16)
    lo = (r1 - mid.astype(F32)).astype(BF16)
    return hi, mid, lo


def _proj_c_body(x_ref, g_ref, w_ref, fb_ref, qkv_ref, qm_ref, ccol_ref, crow_ref,
                 h_ref, carry_ref, *, tiles_per_seq):
    tm = x_ref.shape[0]
    qkv_w = 3 * MIX_WIDTH

    @pl.when(pl.program_id(0) % tiles_per_seq == 0)
    def _():
        carry_ref[...] = jnp.zeros_like(carry_ref)

    h_ref[...] = _rms_norm(x_ref[...], g_ref[...]).astype(BF16)
    for c in range(qkv_w // MXU_N):
        lo = c * MXU_N
        t = _dot(h_ref[...], w_ref[:, lo:lo + MXU_N])
        if lo < MIX_WIDTH:
            t = t * ATTN_SCALE
        qkv_ref[:, lo:lo + MXU_N] = t.astype(BF16)
    qm_ref[...] = (_dot(h_ref[...], w_ref[:, qkv_w + LANES:]) * ATTN_SCALE).astype(BF16)

    z = _dot(h_ref[...], w_ref[:, qkv_w:qkv_w + LANES]) + fb_ref[...]
    log_f = jnp.minimum(z, 0.0) - jnp.log1p(jnp.exp(-jnp.abs(z)))
    lane = lax.broadcasted_iota(jnp.int32, (tm, LANES), 1)
    log_f = jnp.where(lane < N_MIX_HEADS, log_f, 0.0)
    tri = (lax.broadcasted_iota(jnp.int32, (tm, tm), 1)
           <= lax.broadcasted_iota(jnp.int32, (tm, tm), 0)).astype(BF16)
    hi, mid, lo_ = _split3(log_f)
    c = carry_ref[...] + ((_dot(tri, lo_) + _dot(tri, mid)) + _dot(tri, hi))
    carry_ref[...] = c[tm - 1:tm, :]
    ccol_ref[...] = c
    crow_ref[...] = c.T[:crow_ref.shape[0], :]


def _proj_c(x, seq, g_pre, w_in, forget_bias):
    m, d = x.shape
    tm = TM_PROJ
    qkv_w = 3 * MIX_WIDTH
    pad = jnp.zeros((d, LANES - N_MIX_HEADS), w_in.dtype)
    w = jnp.concatenate([w_in[:, :qkv_w + N_MIX_HEADS], pad, w_in[:, qkv_w + N_MIX_HEADS:]], axis=1)
    fb = jnp.concatenate([forget_bias.astype(F32), jnp.zeros((LANES - N_MIX_HEADS,), F32)])
    row = lambda w_: pl.BlockSpec((tm, w_), lambda i: (i, 0))
    crow_rows = 16
    return pl.pallas_call(
        functools.partial(_proj_c_body, tiles_per_seq=seq // tm),
        grid=(m // tm,),
        in_specs=[row(d), _const_spec((1, d)), _const_spec(w.shape), _const_spec((1, LANES))],
        out_specs=[row(qkv_w), row(MEM_WIDTH), row(LANES),
                   pl.BlockSpec((crow_rows, tm), lambda i: (0, i))],
        out_shape=[jax.ShapeDtypeStruct((m, qkv_w), BF16),
                   jax.ShapeDtypeStruct((m, MEM_WIDTH), BF16),
                   jax.ShapeDtypeStruct((m, LANES), F32),
                   jax.ShapeDtypeStruct((crow_rows, m), F32)],
        scratch_shapes=[pltpu.VMEM((tm, d), BF16), pltpu.VMEM((1, LANES), F32)],
        compiler_params=_params("arbitrary"),
        name="proj_c",
    )(x, g_pre, w, fb.reshape(1, LANES))


def _fox_body(q_ref, k_ref, v_ref, crow_ref, ccol_ref, o_ref, *, tq):
    hp = pl.program_id(1)
    i = pl.program_id(2)
    q = q_ref[0]
    ccol = ccol_ref[...]
    lane = lax.broadcasted_iota(jnp.int32, (tq, LANES), 1)
    first = lane < HEAD_DIM
    causal = (lax.broadcasted_iota(jnp.int32, (tq, tq), 1)
              <= lax.broadcasted_iota(jnp.int32, (tq, tq), 0))
    heads = []
    for h in range(2):
        qh = jnp.where(first if h == 0 else ~first, q, jnp.zeros_like(q))
        cq = jnp.sum(jnp.where(lane == 2 * hp + h, ccol, 0.0), axis=-1, keepdims=True)

        def kv_step(j, carry, masked, qh=qh, cq=cq, h=h):
            m_run, l_run, acc = carry
            rows = pl.ds(pl.multiple_of(j * tq, tq), tq)
            s = _dot_nt(qh, k_ref[0, rows, :]) + (cq - crow_ref[0, h:h + 1, rows])
            if masked:
                s = jnp.where(causal, s, NEG_INF)
            m_new = jnp.maximum(m_run, jnp.max(s, axis=-1, keepdims=True))
            alpha = jnp.exp(m_run - m_new)
            e = jnp.exp(s - m_new)
            l_new = alpha * l_run + jnp.sum(e, axis=-1, keepdims=True)
            acc_new = alpha * acc + _dot(e.astype(BF16), v_ref[0, rows, :])
            return m_new, l_new, acc_new

        init = (jnp.full((tq, 1), NEG_INF, F32), jnp.zeros((tq, 1), F32),
                jnp.zeros((tq, LANES), F32))
        carry = lax.fori_loop(0, i, functools.partial(kv_step, masked=False), init)
        _, l_fin, acc = kv_step(i, carry, True)
        heads.append(acc / l_fin)
    o_ref[0] = jnp.where(first, heads[0], heads[1]).astype(BF16)


def _fox_attention(qkv, ccol, crow, batch, seq):
    tq = TQ_FOX
    nq = seq // tq
    pairs = MIX_WIDTH // PAIR
    qkv3 = qkv.reshape(batch, seq, 3 * MIX_WIDTH)
    crow3 = crow.reshape(crow.shape[0] // 2, 2, batch * seq)
    out = pl.pallas_call(
        functools.partial(_fox_body, tq=tq),
        grid=(batch, pairs, nq),
        in_specs=[pl.BlockSpec((1, tq, PAIR), lambda b, p, i: (b, i, p)),
                  pl.BlockSpec((1, seq, PAIR), lambda b, p, i: (b, 0, pairs + p)),
                  pl.BlockSpec((1, seq, PAIR), lambda b, p, i: (b, 0, 2 * pairs + p)),
                  pl.BlockSpec((1, 2, seq), lambda b, p, i: (p, 0, b)),
                  pl.BlockSpec((tq, LANES), lambda b, p, i: (b * nq + i, 0))],
        out_specs=pl.BlockSpec((1, tq, PAIR), lambda b, p, i: (b, i, p)),
        out_shape=jax.ShapeDtypeStruct((batch, seq, MIX_WIDTH), BF16),
        compiler_params=_params("parallel", "parallel", "arbitrary"),
        name="fox_attention",
    )(qkv3, qkv3, qkv3, crow3, ccol)
    return out.reshape(batch * seq, MIX_WIDTH)


def _out_c_body(x_ref, att_ref, qm_ref, kv_ref, w_ref, g_ref, o_ref):
    mem_out = _mem_attention(qm_ref[...], kv_ref).astype(BF16)
    y = _dot(att_ref[...], w_ref[:MIX_WIDTH, :]) + _dot(mem_out, w_ref[MIX_WIDTH:, :])
    _finish_mixer(x_ref, y, g_ref, o_ref)


def _out_c(x, seq, att, qm, kv, w_out, g_post):
    m, d = x.shape
    tm = TM_PROJ
    tiles_per_seq = seq // tm
    n_mem = kv.shape[1]
    row = lambda w: pl.BlockSpec((tm, w), lambda i: (i, 0))
    return pl.pallas_call(
        _out_c_body,
        grid=(m // tm,),
        in_specs=[row(d), row(MIX_WIDTH), row(MEM_WIDTH),
                  pl.BlockSpec((1, n_mem, 2 * MEM_WIDTH), lambda i: (i // tiles_per_seq, 0, 0)),
                  _const_spec(w_out.shape), _const_spec((1, d))],
        out_specs=row(d),
        out_shape=jax.ShapeDtypeStruct((m, d), F32),
        compiler_params=_params("parallel"),
        name="out_c",
    )(x, att, qm, kv, w_out, g_post)


def kernel(x, mem, positions, norm_g, mem_norm_g, w_mem_kv, ffn_w_gate_up, ffn_w_down,
           a_w_in, a_spatial_w, a_spatial_b, a_v_norm_g, a_w_out,
           b_w_in, b_w_out, c_w_in, c_forget_bias, c_w_out):
    batch, seq, d = x.shape
    depth = norm_g.shape[0]
    bf = lambda w: w.astype(BF16)

    kv_all = _mem_kv(mem, mem_norm_g, bf(w_mem_kv))
    cos = sin = None
    xf = x.reshape(batch * seq, d)
    for i in range(depth):
        kind, j = i % 3, i // 3
        g = norm_g[i].reshape(norm_g.shape[1], 1, d)
        xf = _ffn(xf, g[0], g[1], bf(ffn_w_gate_up[i, 0]), bf(ffn_w_down[i, 0]))
        kv = kv_all[i]
        if kind == 0:
            xf = _mixer_a(xf, seq, g[2], g[3], kv, bf(a_w_in[j]), bf(a_spatial_w[j]),
                          a_spatial_b[j], a_v_norm_g[j], bf(a_w_out[j]))
        elif kind == 1:
            if cos is None:
                cos, sin = _rope_tables(positions)
            qkv, qm = _proj_b(xf, g[2], bf(b_w_in[j]), cos, sin)
            qkv3 = qkv.reshape(batch, seq, 3 * MIX_WIDTH)
            outs, lses = zip(*[_dilated_group(qkv3, gi, w, dil)
                               for gi, (w, dil) in enumerate(DILATED_GROUPS)])
            xf = _out_b(xf, seq, outs, lses, qm, kv, bf(b_w_out[j]), g[3])
        else:
            qkv, qm, ccol, crow = _proj_c(xf, seq, g[2], bf(c_w_in[j]), c_forget_bias[j])
            att = _fox_attention(qkv, ccol, crow, batch, seq)
            xf = _out_c(xf, seq, att, qm, kv, bf(c_w_out[j]), g[3])
        xf = _ffn(xf, g[4], g[5], bf(ffn_w_gate_up[i, 1]), bf(ffn_w_down[i, 1]))
    return xf.reshape(batch, seq, d)
```

```python
import functools

import jax
import jax.numpy as jnp
from jax import lax
from jax.experimental import pallas as pl
from jax.experimental.pallas import tpu as pltpu

F32 = jnp.float32
BF16 = jnp.bfloat16

HEAD_DIM = 64
N_MIX_HEADS = 12
MIX_WIDTH = N_MIX_HEADS * HEAD_DIM
N_MEM_HEADS = 4
MEM_WIDTH = N_MEM_HEADS * HEAD_DIM
CHUNK = 128
ROPE_THETA = 10000.0
DILATED_GROUPS = ((128, 1), (512, 4), (2048, 16))
DIL_OUT_WIDTH = (N_MIX_HEADS // len(DILATED_GROUPS)) * HEAD_DIM
RMS_EPS = 1e-6
LN_EPS = 1e-5
NEG_INF = -1e30
ATTN_SCALE = HEAD_DIM ** -0.5

LANES = 128
PAIR = 2 * HEAD_DIM
MXU_N = 256
VMEM_LIMIT_BYTES = 56 * 1024 * 1024

TM_FFN = 512
TM_PROJ = 512
TM_MIX_A = 256
TQ_FOX = 512
FOX_HEADS = 4
FF_CHUNK = 256


def _params(*semantics):
    return pltpu.CompilerParams(dimension_semantics=semantics,
                                vmem_limit_bytes=VMEM_LIMIT_BYTES)


def _const_spec(shape):
    zeros = (0,) * len(shape)
    return pl.BlockSpec(shape, lambda *_: zeros, pipeline_mode=pl.Buffered(1))


def _rms_norm(x, g):
    ms = jnp.mean(x * x, axis=-1, keepdims=True)
    return x * lax.rsqrt(ms + RMS_EPS) * g


def _dot(a, b):
    return jnp.dot(a, b, preferred_element_type=F32)


def _dot_nt(a, b):
    return lax.dot_general(a, b, (((1,), (1,)), ((), ())), preferred_element_type=F32)


def _first_head_lanes(shape):
    return lax.broadcasted_iota(jnp.int32, shape, len(shape) - 1) % PAIR < HEAD_DIM


def _ffn_body(x_ref, gpre_ref, gpost_ref, wgu_ref, wd_ref, o_ref, xn_ref, act_ref, *, d_ff):
    xn_ref[...] = _rms_norm(x_ref[...], gpre_ref[...]).astype(BF16)
    for c in range(d_ff // FF_CHUNK):
        lo = c * FF_CHUNK
        gate = _dot(xn_ref[...], wgu_ref[:, lo:lo + FF_CHUNK])
        up = _dot(xn_ref[...], wgu_ref[:, d_ff + lo:d_ff + lo + FF_CHUNK])
        act_ref[:, lo:lo + FF_CHUNK] = (gate * jax.nn.sigmoid(gate) * up).astype(BF16)
    y = _dot(act_ref[...], wd_ref[...])
    o_ref[...] = x_ref[...] + 0.5 * _rms_norm(y, gpost_ref[...])


def _ffn(x, g_pre, g_post, w_gate_up, w_down):
    m, d = x.shape
    d_ff = w_down.shape[0]
    assert m % TM_FFN == 0 and d_ff % FF_CHUNK == 0
    row = pl.BlockSpec((TM_FFN, d), lambda i: (i, 0))
    return pl.pallas_call(
        functools.partial(_ffn_body, d_ff=d_ff),
        grid=(m // TM_FFN,),
        in_specs=[row, _const_spec((1, d)), _const_spec((1, d)),
                  _const_spec((d, 2 * d_ff)), _const_spec((d_ff, d))],
        out_specs=row,
        out_shape=jax.ShapeDtypeStruct((m, d), F32),
        scratch_shapes=[pltpu.VMEM((TM_FFN, d), BF16), pltpu.VMEM((TM_FFN, d_ff), BF16)],
        compiler_params=_params("parallel"),
        name="ffn",
    )(x, g_pre, g_post, w_gate_up, w_down)


def _mem_kv_body(mem_ref, g_ref, w_ref, o_ref):
    o_ref[0, 0] = _dot(_rms_norm(mem_ref[0], g_ref[0]).astype(BF16), w_ref[0]).astype(BF16)


def _mem_kv(mem, mem_norm_g, w_mem_kv):
    b, n_mem, d = mem.shape
    depth = w_mem_kv.shape[0]
    return pl.pallas_call(
        _mem_kv_body,
        grid=(depth, b),
        in_specs=[pl.BlockSpec((1, n_mem, d), lambda l, i: (i, 0, 0)),
                  pl.BlockSpec((1, 1, d), lambda l, i: (l, 0, 0)),
                  pl.BlockSpec((1, d, 2 * MEM_WIDTH), lambda l, i: (l, 0, 0))],
        out_specs=pl.BlockSpec((1, 1, n_mem, 2 * MEM_WIDTH), lambda l, i: (l, i, 0, 0)),
        out_shape=jax.ShapeDtypeStruct((depth, b, n_mem, 2 * MEM_WIDTH), BF16),
        compiler_params=_params("parallel", "parallel"),
        name="mem_kv",
    )(mem, mem_norm_g.reshape(depth, 1, d), w_mem_kv)


def _stack_heads(qp, first):
    zero = jnp.zeros_like(qp)
    return jnp.concatenate([jnp.where(first, qp, zero), jnp.where(first, zero, qp)], axis=0)


def _softmax_pv(s, v, with_lse=False):
    mx = jnp.max(s, axis=-1, keepdims=True)
    e = jnp.exp(s - mx)
    denom = jnp.sum(e, axis=-1, keepdims=True)
    out = _dot(e.astype(BF16), v) / denom
    if with_lse:
        return out, mx + jnp.log(denom)
    return out


def _mem_attention(qm, kv_ref):
    tm = qm.shape[0]
    outs = []
    for p in range(MEM_WIDTH // PAIR):
        qp = qm[:, p * PAIR:(p + 1) * PAIR]
        k = kv_ref[0, :, p * PAIR:(p + 1) * PAIR]
        v = kv_ref[0, :, MEM_WIDTH + p * PAIR:MEM_WIDTH + (p + 1) * PAIR]
        first = _first_head_lanes(qp.shape)
        acc = _softmax_pv(_dot_nt(_stack_heads(qp, first), k), v)
        outs.append(jnp.where(first, acc[:tm], acc[tm:]))
    return jnp.concatenate(outs, axis=-1)


def _finish_mixer(x_ref, y, g_ref, o_ref):
    o_ref[...] = x_ref[...] + _rms_norm(y, g_ref[...])


def _mixer_a_body(x_ref, gpre_ref, win_ref, vg_ref, sw_ref, sb_ref, kv_ref, wout_ref, gpost_ref,
                  o_ref, u_ref, v_ref, gated_ref, wc_ref):
    tm = x_ref.shape[0]
    h = _rms_norm(x_ref[...], gpre_ref[...]).astype(BF16)
    u_ref[...] = jax.nn.gelu(_dot(h, win_ref[:, :MIX_WIDTH]))
    v = jax.nn.gelu(_dot(h, win_ref[:, MIX_WIDTH:2 * MIX_WIDTH]))
    mu = jnp.mean(v, axis=-1, keepdims=True)
    vc = v - mu
    var = jnp.mean(vc * vc, axis=-1, keepdims=True)
    v_ref[...] = (vc * lax.rsqrt(var + LN_EPS) * vg_ref[...]).astype(BF16)
    qm = (_dot(h, win_ref[:, 2 * MIX_WIDTH:]) * ATTN_SCALE).astype(BF16)

    t_idx = lax.broadcasted_iota(jnp.int32, (CHUNK, CHUNK), 0)
    s_idx = lax.broadcasted_iota(jnp.int32, (CHUNK, CHUNK), 1)
    for g in range(N_MIX_HEADS):
        wc_ref[g] = jnp.where(s_idx <= t_idx, sw_ref[g], jnp.zeros((CHUNK, CHUNK), BF16))

    first = _first_head_lanes((CHUNK, PAIR))
    for c in range(tm // CHUNK):
        rows = slice(c * CHUNK, (c + 1) * CHUNK)
        for p in range(MIX_WIDTH // PAIR):
            lanes = slice(p * PAIR, (p + 1) * PAIR)
            vp = v_ref[rows, lanes]
            mixed = jnp.where(first, _dot(wc_ref[2 * p], vp), _dot(wc_ref[2 * p + 1], vp))
            gated_ref[rows, lanes] = (u_ref[rows, lanes] * (mixed + sb_ref[:, lanes])).astype(BF16)

    mem_out = _mem_attention(qm, kv_ref).astype(BF16)
    y = _dot(gated_ref[...], wout_ref[:MIX_WIDTH, :]) + _dot(mem_out, wout_ref[MIX_WIDTH:, :])
    _finish_mixer(x_ref, y, gpost_ref, o_ref)


def _mixer_a(x, seq, g_pre, g_post, kv, w_in, spatial_w, spatial_b, v_norm_g, w_out):
    m, d = x.shape
    tm = TM_MIX_A
    assert seq % tm == 0 and tm % CHUNK == 0
    tiles_per_seq = seq // tm
    n_mem = kv.shape[1]
    row = pl.BlockSpec((tm, d), lambda i: (i, 0))
    bias = jnp.repeat(spatial_b.T, HEAD_DIM, axis=1)
    return pl.pallas_call(
        _mixer_a_body,
        grid=(m // tm,),
        in_specs=[row, _const_spec((1, d)), _const_spec(w_in.shape), _const_spec((1, MIX_WIDTH)),
                  _const_spec(spatial_w.shape), _const_spec(bias.shape),
                  pl.BlockSpec((1, n_mem, 2 * MEM_WIDTH), lambda i: (i // tiles_per_seq, 0, 0)),
                  _const_spec(w_out.shape), _const_spec((1, d))],
        out_specs=row,
        out_shape=jax.ShapeDtypeStruct((m, d), F32),
        scratch_shapes=[pltpu.VMEM((tm, MIX_WIDTH), F32), pltpu.VMEM((tm, MIX_WIDTH), BF16),
                        pltpu.VMEM((tm, MIX_WIDTH), BF16),
                        pltpu.VMEM((N_MIX_HEADS, CHUNK, CHUNK), BF16)],
        compiler_params=_params("parallel"),
        name="mixer_a",
    )(x, g_pre, w_in, v_norm_g.reshape(1, MIX_WIDTH), spatial_w, bias, kv, w_out, g_post)


def _rope_body(pos_ref, freq_ref, sign_ref, cos_ref, sin_ref):
    ang = pos_ref[...].astype(F32) * freq_ref[...]
    cos_ref[...] = jnp.cos(ang)
    sin_ref[...] = jnp.sin(ang) * sign_ref[...]


def _rope_tables(positions):
    m = positions.size
    tm = TM_PROJ
    inv_freq = ROPE_THETA ** (-jnp.arange(0, HEAD_DIM, 2, dtype=F32) / HEAD_DIM)
    freq = jnp.tile(inv_freq, LANES // (HEAD_DIM // 2)).reshape(1, LANES)
    half = jnp.concatenate([-jnp.ones((HEAD_DIM // 2,), F32), jnp.ones((HEAD_DIM // 2,), F32)])
    sign = jnp.tile(half, LANES // HEAD_DIM).reshape(1, LANES)
    table = pl.BlockSpec((tm, LANES), lambda i: (i, 0))
    return pl.pallas_call(
        _rope_body,
        grid=(m // tm,),
        in_specs=[pl.BlockSpec((tm, 1), lambda i: (i, 0)), _const_spec((1, LANES)),
                  _const_spec((1, LANES))],
        out_specs=[table, table],
        out_shape=[jax.ShapeDtypeStruct((m, LANES), F32)] * 2,
        compiler_params=_params("parallel"),
        name="rope_tables",
    )(positions.reshape(m, 1), freq, sign)


def _rope(t, cos, sin_signed):
    first_half = lax.broadcasted_iota(jnp.int32, t.shape, 1) % HEAD_DIM < HEAD_DIM // 2
    swapped = jnp.where(first_half,
                        pltpu.roll(t, LANES - HEAD_DIM // 2, axis=1),
                        pltpu.roll(t, HEAD_DIM // 2, axis=1))
    return t * cos + swapped * sin_signed


def _residue_spec(tm, dilation, width, tiles_per_seq):
    return pl.BlockSpec((1, dilation, tm // dilation, width),
                        lambda i: (i // tiles_per_seq, 0, i % tiles_per_seq, 0))


def _proj_b_body(x_ref, g_ref, w_ref, cos_ref, sin_ref, g0_ref, g1_ref, g2_ref, qm_ref,
                 h_ref, t_ref):
    tm = x_ref.shape[0]
    h_ref[...] = _rms_norm(x_ref[...], g_ref[...]).astype(BF16)
    cos = cos_ref[...]
    sin = sin_ref[...]
    outs = (g0_ref, g1_ref, g2_ref)
    n_groups = len(DILATED_GROUPS)
    for c in range(3 * n_groups):
        kind, grp = divmod(c, n_groups)
        lo = c * DIL_OUT_WIDTH
        t = _dot(h_ref[...], w_ref[:, lo:lo + DIL_OUT_WIDTH])
        if kind < 2:
            scale = ATTN_SCALE if kind == 0 else 1.0
            t = jnp.concatenate(
                [_rope(t[:, j * LANES:(j + 1) * LANES], cos, sin)
                 for j in range(DIL_OUT_WIDTH // LANES)], axis=-1) * scale
        dil = DILATED_GROUPS[grp][1]
        cols = slice(kind * DIL_OUT_WIDTH, (kind + 1) * DIL_OUT_WIDTH)
        if dil == 1:
            outs[grp][0, 0, :, cols] = t.astype(BF16)
        else:
            halves = DIL_OUT_WIDTH // LANES
            for j in range(halves):
                t_ref[j] = t[:, j * LANES:(j + 1) * LANES]
            for r in range(dil):
                outs[grp][0, r, :, cols] = jnp.concatenate(
                    [t_ref[j, pl.ds(r, tm // dil, stride=dil), :] for j in range(halves)],
                    axis=-1).astype(BF16)
    qm_ref[...] = (_dot(h_ref[...], w_ref[:, 3 * MIX_WIDTH:]) * ATTN_SCALE).astype(BF16)


def _proj_b(x, batch, seq, g_pre, w_in, cos, sin):
    m, d = x.shape
    tm = TM_PROJ
    tiles_per_seq = seq // tm
    row = lambda w: pl.BlockSpec((tm, w), lambda i: (i, 0))
    width = 3 * DIL_OUT_WIDTH
    group_specs = [_residue_spec(tm, dil, width, tiles_per_seq) for _, dil in DILATED_GROUPS]
    group_shapes = [jax.ShapeDtypeStruct((batch, dil, seq // dil, width), BF16)
                    for _, dil in DILATED_GROUPS]
    return pl.pallas_call(
        _proj_b_body,
        grid=(m // tm,),
        in_specs=[row(d), _const_spec((1, d)), _const_spec(w_in.shape), row(LANES), row(LANES)],
        out_specs=group_specs + [row(MEM_WIDTH)],
        out_shape=group_shapes + [jax.ShapeDtypeStruct((m, MEM_WIDTH), BF16)],
        scratch_shapes=[pltpu.VMEM((tm, d), BF16),
                        pltpu.VMEM((DIL_OUT_WIDTH // LANES, tm, LANES), F32)],
        compiler_params=_params("parallel"),
        name="proj_b",
    )(x, g_pre, w_in, cos, sin)


def _dilated_body(qkv_ref, o_ref, lse_ref, *, n_blocks, span):
    w = DIL_OUT_WIDTH
    qi = lax.broadcasted_iota(jnp.int32, (2 * span, 2 * span), 0) % span
    ki = lax.broadcasted_iota(jnp.int32, (2 * span, 2 * span), 1)
    band = ((ki < span) & (ki >= qi)) | ((ki >= span) & (ki - span <= qi))
    own_only = (lax.broadcasted_iota(jnp.int32, (2 * span, span), 1)
                <= lax.broadcasted_iota(jnp.int32, (2 * span, span), 0) % span)
    first = _first_head_lanes((span, PAIR))
    own_lanes = jnp.concatenate([first, ~first], axis=0)

    def block(r0, has_prev):
        own = pl.ds(r0, span)
        keys = pl.ds(r0 - span, 2 * span) if has_prev else own
        for p in range(w // PAIR):
            ql = slice(p * PAIR, (p + 1) * PAIR)
            q2 = _stack_heads(qkv_ref[0, 0, own, ql], first)
            k2 = qkv_ref[0, 0, keys, w + p * PAIR:w + (p + 1) * PAIR]
            v2 = qkv_ref[0, 0, keys, 2 * w + p * PAIR:2 * w + (p + 1) * PAIR]
            s = jnp.where(band if has_prev else own_only, _dot_nt(q2, k2), NEG_INF)
            acc, lse = _softmax_pv(s, v2, with_lse=True)
            o_ref[0, 0, own, ql] = jnp.where(first, acc[:span], acc[span:]).astype(BF16)
            lse = jnp.where(own_lanes, lse, 0.0)
            lse_ref[0, 0, own, ql] = lse[:span] + lse[span:]

    block(0, False)

    def step(blk, carry):
        block(pl.multiple_of(blk * span, span), True)
        return carry

    lax.fori_loop(1, n_blocks, step, 0, unroll=2)


def _dilated_group(qkv, window, dilation):
    b, _, length, width = qkv.shape
    span = window // dilation
    assert length % span == 0
    out_spec = pl.BlockSpec((1, 1, length, DIL_OUT_WIDTH), lambda i, r: (i, r, 0, 0))
    return pl.pallas_call(
        functools.partial(_dilated_body, n_blocks=length // span, span=span),
        grid=(b, dilation),
        in_specs=[pl.BlockSpec((1, 1, length, width), lambda i, r: (i, r, 0, 0))],
        out_specs=[out_spec, out_spec],
        out_shape=[jax.ShapeDtypeStruct((b, dilation, length, DIL_OUT_WIDTH), BF16),
                   jax.ShapeDtypeStruct((b, dilation, length, DIL_OUT_WIDTH), F32)],
        compiler_params=_params("parallel", "parallel"),
        name=f"dilated_{dilation}",
    )(qkv)


def _out_b_body(x_ref, o0_ref, o1_ref, o2_ref, l0_ref, l1_ref, l2_ref, qm_ref, kv_ref, w_ref,
                g_ref, o_ref, *scratch):
    tm = x_ref.shape[0]

    def token_major(ref, buf):
        dil = ref.shape[1]
        if dil == 1:
            return ref[0, 0].astype(F32)
        halves = DIL_OUT_WIDTH // LANES
        for r in range(dil):
            rows = ref[0, r].astype(F32)
            for j in range(halves):
                buf[j, pl.ds(r, tm // dil, stride=dil), :] = rows[:, j * LANES:(j + 1) * LANES]
        return jnp.concatenate([buf[j] for j in range(halves)], axis=-1)

    outs = [token_major(r, scratch[2 * g]) for g, r in enumerate((o0_ref, o1_ref, o2_ref))]
    lses = [token_major(r, scratch[2 * g + 1]) for g, r in enumerate((l0_ref, l1_ref, l2_ref))]
    mx = jnp.maximum(jnp.maximum(lses[0], lses[1]), lses[2])
    es = [jnp.exp(l - mx) for l in lses]
    denom = es[0] + es[1] + es[2]
    merged = sum((e / denom) * o for e, o in zip(es, outs))
    mem_out = _mem_attention(qm_ref[...], kv_ref).astype(BF16)
    y = (_dot(merged.astype(BF16), w_ref[:DIL_OUT_WIDTH, :])
         + _dot(mem_out, w_ref[DIL_OUT_WIDTH:, :]))
    _finish_mixer(x_ref, y, g_ref, o_ref)


def _out_b(x, seq, outs, lses, qm, kv, w_out, g_post):
    m, d = x.shape
    tm = TM_PROJ
    tiles_per_seq = seq // tm
    n_mem = kv.shape[1]
    row = lambda w: pl.BlockSpec((tm, w), lambda i: (i, 0))
    group_specs = [_residue_spec(tm, dil, DIL_OUT_WIDTH, tiles_per_seq)
                   for _, dil in DILATED_GROUPS]
    return pl.pallas_call(
        _out_b_body,
        grid=(m // tm,),
        in_specs=[row(d)] + group_specs * 2 + [
            row(MEM_WIDTH),
            pl.BlockSpec((1, n_mem, 2 * MEM_WIDTH), lambda i: (i // tiles_per_seq, 0, 0)),
            _const_spec(w_out.shape), _const_spec((1, d))],
        out_specs=row(d),
        out_shape=jax.ShapeDtypeStruct((m, d), F32),
        scratch_shapes=[pltpu.VMEM((DIL_OUT_WIDTH // LANES, tm, LANES), F32)]
        * (2 * len(DILATED_GROUPS)),
        compiler_params=_params("parallel"),
        name="out_b",
    )(x, *outs, *lses, qm, kv, w_out, g_post)


def _split3(x):
    hi = x.astype(BF16)
    r1 = x - hi.astype(F32)
    mid = r1.astype(BF16)
    lo = (r1 - mid.astype(F32)).astype(BF16)
    return hi, mid, lo


N_BIAS_TERMS = 3
AUG_WIDTH = N_MIX_HEADS * LANES


def _pack_terms(hi, mid, lo):
    return (hi.astype(F32) + pltpu.roll(mid.astype(F32), N_MIX_HEADS, axis=1)
            + pltpu.roll(lo.astype(F32), 2 * N_MIX_HEADS, axis=1)).astype(BF16)


def _bias_selectors():
    rows = jnp.arange(LANES)
    term, head = rows // N_MIX_HEADS, rows % N_MIX_HEADS
    valid = rows < N_BIAS_TERMS * N_MIX_HEADS
    cols = jnp.arange(MIX_WIDTH)
    base = (head // 2) * PAIR + jnp.where(head % 2 == 0, HEAD_DIM, 0)

    def select(offset):
        target = base + offset + term
        return ((cols[None, :] == target[:, None]) & valid[:, None]).astype(BF16)

    def ones(offset):
        lane = cols % HEAD_DIM
        return ((lane >= offset) & (lane < offset + N_BIAS_TERMS)).astype(F32)[None, :]

    q_sel, k_sel = select(0), select(N_BIAS_TERMS)
    q_one, k_one = ones(N_BIAS_TERMS), ones(0)
    return q_sel, k_sel, q_one, k_one


def _proj_c_body(x_ref, g_ref, w_ref, fb_ref, qsel_ref, ksel_ref, qone_ref, kone_ref,
                 q_ref, k_ref, v_ref, qm_ref, h_ref, carry_ref, qext_ref, kext_ref,
                 *, tiles_per_seq):
    tm = x_ref.shape[0]
    qkv_w = 3 * MIX_WIDTH

    @pl.when(pl.program_id(0) % tiles_per_seq == 0)
    def _():
        carry_ref[...] = jnp.zeros_like(carry_ref)

    h_ref[...] = _rms_norm(x_ref[...], g_ref[...]).astype(BF16)

    z = _dot(h_ref[...], w_ref[:, qkv_w:qkv_w + LANES]) + fb_ref[...]
    log_f = jnp.minimum(z, 0.0) - jnp.log1p(jnp.exp(-jnp.abs(z)))
    lane = lax.broadcasted_iota(jnp.int32, (tm, LANES), 1)
    log_f = jnp.where(lane < N_MIX_HEADS, log_f, 0.0)
    tri = (lax.broadcasted_iota(jnp.int32, (tm, tm), 1)
           <= lax.broadcasted_iota(jnp.int32, (tm, tm), 0)).astype(BF16)
    sums = _dot(tri, _pack_terms(*_split3(log_f)))
    local = (sums + pltpu.roll(sums, LANES - N_MIX_HEADS, axis=1)
             + pltpu.roll(sums, LANES - 2 * N_MIX_HEADS, axis=1))
    c = carry_ref[...] + jnp.where(lane < N_MIX_HEADS, local, 0.0)
    carry_ref[...] = c[tm - 1:tm, :]
    terms = _pack_terms(*_split3(c))
    qext_ref[...] = _dot(terms, qsel_ref[...]) + qone_ref[...]
    kext_ref[...] = kone_ref[...] - _dot(terms, ksel_ref[...])

    first = _first_head_lanes((tm, PAIR))
    for c_idx in range(2 * MIX_WIDTH // MXU_N):
        lo = c_idx * MXU_N
        is_q = lo < MIX_WIDTH
        t = _dot(h_ref[...], w_ref[:, lo:lo + MXU_N])
        if is_q:
            t = t * ATTN_SCALE
        dst, ext_ref = (q_ref, qext_ref) if is_q else (k_ref, kext_ref)
        for j in range(MXU_N // PAIR):
            p = (lo % MIX_WIDTH) // PAIR + j
            pair = t[:, j * PAIR:(j + 1) * PAIR]
            ext = ext_ref[:, p * PAIR:(p + 1) * PAIR]
            dst[:, 2 * p * LANES:(2 * p + 1) * LANES] = jnp.where(first, pair, ext).astype(BF16)
            dst[:, (2 * p + 1) * LANES:(2 * p + 2) * LANES] = pltpu.roll(
                jnp.where(first, ext, pair), HEAD_DIM, axis=1).astype(BF16)
    for c_idx in range(MIX_WIDTH // MXU_N):
        lo = c_idx * MXU_N
        v_ref[:, lo:lo + MXU_N] = _dot(
            h_ref[...], w_ref[:, 2 * MIX_WIDTH + lo:2 * MIX_WIDTH + lo + MXU_N]).astype(BF16)
    qm_ref[...] = (_dot(h_ref[...], w_ref[:, qkv_w + LANES:]) * ATTN_SCALE).astype(BF16)


def _proj_c(x, seq, g_pre, w_in, forget_bias):
    m, d = x.shape
    tm = TM_PROJ
    qkv_w = 3 * MIX_WIDTH
    pad = jnp.zeros((d, LANES - N_MIX_HEADS), w_in.dtype)
    w = jnp.concatenate([w_in[:, :qkv_w + N_MIX_HEADS], pad, w_in[:, qkv_w + N_MIX_HEADS:]], axis=1)
    fb = jnp.concatenate([forget_bias.astype(F32), jnp.zeros((LANES - N_MIX_HEADS,), F32)])
    consts = _bias_selectors()
    row = lambda w_: pl.BlockSpec((tm, w_), lambda i: (i, 0))
    return pl.pallas_call(
        functools.partial(_proj_c_body, tiles_per_seq=seq // tm),
        grid=(m // tm,),
        in_specs=[row(d), _const_spec((1, d)), _const_spec(w.shape), _const_spec((1, LANES))]
        + [_const_spec(c.shape) for c in consts],
        out_specs=[row(AUG_WIDTH), row(AUG_WIDTH), row(MIX_WIDTH), row(MEM_WIDTH)],
        out_shape=[jax.ShapeDtypeStruct((m, AUG_WIDTH), BF16),
                   jax.ShapeDtypeStruct((m, AUG_WIDTH), BF16),
                   jax.ShapeDtypeStruct((m, MIX_WIDTH), BF16),
                   jax.ShapeDtypeStruct((m, MEM_WIDTH), BF16)],
        scratch_shapes=[pltpu.VMEM((tm, d), BF16), pltpu.VMEM((1, LANES), F32),
                        pltpu.VMEM((tm, MIX_WIDTH), F32), pltpu.VMEM((tm, MIX_WIDTH), F32)],
        compiler_params=_params("arbitrary"),
        name="proj_c",
    )(x, g_pre, w, fb.reshape(1, LANES), *consts)


def _fox_body(q_ref, k_ref, v_ref, o_ref, acc_ref, *, tq):
    i = pl.program_id(2)
    first = _first_head_lanes((tq, PAIR))
    causal = (lax.broadcasted_iota(jnp.int32, (tq, tq), 1)
              <= lax.broadcasted_iota(jnp.int32, (tq, tq), 0))
    acc_ref[...] = jnp.zeros_like(acc_ref)

    def kv_step(j, carry, masked):
        rows = pl.ds(pl.multiple_of(j * tq, tq), tq)
        new = []
        for h in range(FOX_HEADS):
            m_run, l_run = carry[h]
            lanes = slice(h * LANES, (h + 1) * LANES)
            s = _dot_nt(q_ref[0, :, lanes], k_ref[0, rows, lanes])
            if masked:
                s = jnp.where(causal, s, NEG_INF)
            m_new = jnp.maximum(m_run, jnp.max(s, axis=-1, keepdims=True))
            alpha = jnp.exp(m_run - m_new)
            e = jnp.exp(s - m_new)
            l_new = alpha * l_run + jnp.sum(e, axis=-1, keepdims=True)
            v = v_ref[0, rows, (h // 2) * PAIR:(h // 2 + 1) * PAIR]
            acc_ref[h] = alpha * acc_ref[h] + _dot(e.astype(BF16), v)
            new.append((m_new, l_new))
        return tuple(new)

    init = tuple((jnp.full((tq, 1), NEG_INF, F32), jnp.zeros((tq, 1), F32))
                 for _ in range(FOX_HEADS))
    carry = lax.fori_loop(0, i, functools.partial(kv_step, masked=False), init)
    final = kv_step(i, carry, True)
    for p in range(FOX_HEADS // 2):
        o_ref[0, :, p * PAIR:(p + 1) * PAIR] = jnp.where(
            first, acc_ref[2 * p] / final[2 * p][1],
            acc_ref[2 * p + 1] / final[2 * p + 1][1]).astype(BF16)


def _fox_attention(q_aug, k_aug, v, batch, seq):
    tq = TQ_FOX
    q3 = q_aug.reshape(batch, seq, AUG_WIDTH)
    k3 = k_aug.reshape(batch, seq, AUG_WIDTH)
    v3 = v.reshape(batch, seq, MIX_WIDTH)
    aug_w = FOX_HEADS * LANES
    v_w = FOX_HEADS * HEAD_DIM
    out = pl.pallas_call(
        functools.partial(_fox_body, tq=tq),
        grid=(batch, N_MIX_HEADS // FOX_HEADS, seq // tq),
        in_specs=[pl.BlockSpec((1, tq, aug_w), lambda b, p, i: (b, i, p)),
                  pl.BlockSpec((1, seq, aug_w), lambda b, p, i: (b, 0, p)),
                  pl.BlockSpec((1, seq, v_w), lambda b, p, i: (b, 0, p))],
        out_specs=pl.BlockSpec((1, tq, v_w), lambda b, p, i: (b, i, p)),
        out_shape=jax.ShapeDtypeStruct((batch, seq, MIX_WIDTH), BF16),
        scratch_shapes=[pltpu.VMEM((FOX_HEADS, tq, PAIR), F32)],
        compiler_params=_params("parallel", "parallel", "arbitrary"),
        name="fox_attention",
    )(q3, k3, v3)
    return out.reshape(batch * seq, MIX_WIDTH)


def _out_c_body(x_ref, att_ref, qm_ref, kv_ref, w_ref, g_ref, o_ref):
    mem_out = _mem_attention(qm_ref[...], kv_ref).astype(BF16)
    y = _dot(att_ref[...], w_ref[:MIX_WIDTH, :]) + _dot(mem_out, w_ref[MIX_WIDTH:, :])
    _finish_mixer(x_ref, y, g_ref, o_ref)


def _out_c(x, seq, att, qm, kv, w_out, g_post):
    m, d = x.shape
    tm = TM_PROJ
    tiles_per_seq = seq // tm
    n_mem = kv.shape[1]
    row = lambda w: pl.BlockSpec((tm, w), lambda i: (i, 0))
    return pl.pallas_call(
        _out_c_body,
        grid=(m // tm,),
        in_specs=[row(d), row(MIX_WIDTH), row(MEM_WIDTH),
                  pl.BlockSpec((1, n_mem, 2 * MEM_WIDTH), lambda i: (i // tiles_per_seq, 0, 0)),
                  _const_spec(w_out.shape), _const_spec((1, d))],
        out_specs=row(d),
        out_shape=jax.ShapeDtypeStruct((m, d), F32),
        compiler_params=_params("parallel"),
        name="out_c",
    )(x, att, qm, kv, w_out, g_post)


def kernel(x, mem, positions, norm_g, mem_norm_g, w_mem_kv, ffn_w_gate_up, ffn_w_down,
           a_w_in, a_spatial_w, a_spatial_b, a_v_norm_g, a_w_out,
           b_w_in, b_w_out, c_w_in, c_forget_bias, c_w_out):
    batch, seq, d = x.shape
    depth = norm_g.shape[0]
    bf = lambda w: w.astype(BF16)

    kv_all = _mem_kv(mem, mem_norm_g, bf(w_mem_kv))
    cos = sin = None
    xf = x.reshape(batch * seq, d)
    for i in range(depth):
        kind, j = i % 3, i // 3
        g = norm_g[i].reshape(norm_g.shape[1], 1, d)
        xf = _ffn(xf, g[0], g[1], bf(ffn_w_gate_up[i, 0]), bf(ffn_w_down[i, 0]))
        kv = kv_all[i]
        if kind == 0:
            xf = _mixer_a(xf, seq, g[2], g[3], kv, bf(a_w_in[j]), bf(a_spatial_w[j]),
                          a_spatial_b[j], a_v_norm_g[j], bf(a_w_out[j]))
        elif kind == 1:
            if cos is None:
                cos, sin = _rope_tables(positions)
            *groups, qm = _proj_b(xf, batch, seq, g[2], bf(b_w_in[j]), cos, sin)
            outs, lses = zip(*[_dilated_group(qkv_g, w, dil)
                               for qkv_g, (w, dil) in zip(groups, DILATED_GROUPS)])
            xf = _out_b(xf, seq, outs, lses, qm, kv, bf(b_w_out[j]), g[3])
        else:
            q_aug, k_aug, v, qm = _proj_c(xf, seq, g[2], bf(c_w_in[j]), c_forget_bias[j])
            att = _fox_attention(q_aug, k_aug, v, batch, seq)
            xf = _out_c(xf, seq, att, qm, kv, bf(c_w_out[j]), g[3])
        xf = _ffn(xf, g[4], g[5], bf(ffn_w_gate_up[i, 1]), bf(ffn_w_down[i, 1]))
    return xf.reshape(batch, seq, d)
```

```python
import functools

import jax
import jax.numpy as jnp
from jax import lax
from jax.experimental import pallas as pl
from jax.experimental.pallas import tpu as pltpu

F32 = jnp.float32
BF16 = jnp.bfloat16

HEAD_DIM = 64
N_MIX_HEADS = 12
MIX_WIDTH = N_MIX_HEADS * HEAD_DIM
N_MEM_HEADS = 4
MEM_WIDTH = N_MEM_HEADS * HEAD_DIM
CHUNK = 128
ROPE_THETA = 10000.0
DILATED_GROUPS = ((128, 1), (512, 4), (2048, 16))
DIL_OUT_WIDTH = (N_MIX_HEADS // len(DILATED_GROUPS)) * HEAD_DIM
RMS_EPS = 1e-6
LN_EPS = 1e-5
NEG_INF = -1e30
ATTN_SCALE = HEAD_DIM ** -0.5

LANES = 128
PAIR = 2 * HEAD_DIM
MXU_N = 256
VMEM_LIMIT_BYTES = 56 * 1024 * 1024

TM_FFN = 1024
FFN_SUB = 512
TM_PROJ = 512
TM_MIX_A = 512
TQ_FOX = 512
TK_FOX = 512
FOX_HEADS = 4
FF_CHUNK = 256


def _params(*semantics):
    return pltpu.CompilerParams(dimension_semantics=semantics,
                                vmem_limit_bytes=VMEM_LIMIT_BYTES)


def _const_spec(shape):
    zeros = (0,) * len(shape)
    return pl.BlockSpec(shape, lambda *_: zeros, pipeline_mode=pl.Buffered(1))


def _rms_norm(x, g):
    ms = jnp.mean(x * x, axis=-1, keepdims=True)
    return x * lax.rsqrt(ms + RMS_EPS) * g


def _dot(a, b):
    return jnp.dot(a, b, preferred_element_type=F32)


def _dot_nt(a, b):
    return lax.dot_general(a, b, (((1,), (1,)), ((), ())), preferred_element_type=F32)


def _first_head_lanes(shape):
    return lax.broadcasted_iota(jnp.int32, shape, len(shape) - 1) % PAIR < HEAD_DIM


def _ffn_body(x_ref, gpre_ref, gpost_ref, wgu_ref, wd_ref, o_ref, xn_ref, act_ref, *, d_ff):
    for sub in range(x_ref.shape[0] // FFN_SUB):
        rows = slice(sub * FFN_SUB, (sub + 1) * FFN_SUB)
        xn_ref[rows, :] = _rms_norm(x_ref[rows, :], gpre_ref[...]).astype(BF16)
        for c in range(d_ff // FF_CHUNK):
            lo = c * FF_CHUNK
            gate = _dot(xn_ref[rows, :], wgu_ref[:, lo:lo + FF_CHUNK])
            up = _dot(xn_ref[rows, :], wgu_ref[:, d_ff + lo:d_ff + lo + FF_CHUNK])
            act_ref[rows, lo:lo + FF_CHUNK] = (gate * jax.nn.sigmoid(gate) * up).astype(BF16)
        y = _dot(act_ref[rows, :], wd_ref[...])
        o_ref[rows, :] = x_ref[rows, :] + 0.5 * _rms_norm(y, gpost_ref[...])


def _ffn(x, g_pre, g_post, w_gate_up, w_down):
    m, d = x.shape
    d_ff = w_down.shape[0]
    assert m % TM_FFN == 0 and d_ff % FF_CHUNK == 0
    row = pl.BlockSpec((TM_FFN, d), lambda i: (i, 0))
    return pl.pallas_call(
        functools.partial(_ffn_body, d_ff=d_ff),
        grid=(m // TM_FFN,),
        in_specs=[row, _const_spec((1, d)), _const_spec((1, d)),
                  _const_spec((d, 2 * d_ff)), _const_spec((d_ff, d))],
        out_specs=row,
        out_shape=jax.ShapeDtypeStruct((m, d), F32),
        scratch_shapes=[pltpu.VMEM((TM_FFN, d), BF16), pltpu.VMEM((TM_FFN, d_ff), BF16)],
        compiler_params=_params("parallel"),
        name="ffn",
    )(x, g_pre, g_post, w_gate_up, w_down)


def _mem_kv_body(mem_ref, g_ref, w_ref, o_ref):
    o_ref[0, 0] = _dot(_rms_norm(mem_ref[0], g_ref[0]).astype(BF16), w_ref[0]).astype(BF16)


def _mem_kv(mem, mem_norm_g, w_mem_kv):
    b, n_mem, d = mem.shape
    depth = w_mem_kv.shape[0]
    return pl.pallas_call(
        _mem_kv_body,
        grid=(depth, b),
        in_specs=[pl.BlockSpec((1, n_mem, d), lambda l, i: (i, 0, 0)),
                  pl.BlockSpec((1, 1, d), lambda l, i: (l, 0, 0)),
                  pl.BlockSpec((1, d, 2 * MEM_WIDTH), lambda l, i: (l, 0, 0))],
        out_specs=pl.BlockSpec((1, 1, n_mem, 2 * MEM_WIDTH), lambda l, i: (l, i, 0, 0)),
        out_shape=jax.ShapeDtypeStruct((depth, b, n_mem, 2 * MEM_WIDTH), BF16),
        compiler_params=_params("parallel", "parallel"),
        name="mem_kv",
    )(mem, mem_norm_g.reshape(depth, 1, d), w_mem_kv)


def _stack_heads(qp, first):
    zero = jnp.zeros_like(qp)
    return jnp.concatenate([jnp.where(first, qp, zero), jnp.where(first, zero, qp)], axis=0)


def _softmax_pv(s, v, with_lse=False):
    mx = jnp.max(s, axis=-1, keepdims=True)
    e = jnp.exp(s - mx)
    denom = jnp.sum(e, axis=-1, keepdims=True)
    out = _dot(e.astype(BF16), v) / denom
    if with_lse:
        return out, mx + jnp.log(denom)
    return out


def _mem_attention(qm, kv_ref):
    tm = qm.shape[0]
    outs = []
    for p in range(MEM_WIDTH // PAIR):
        qp = qm[:, p * PAIR:(p + 1) * PAIR]
        k = kv_ref[0, :, p * PAIR:(p + 1) * PAIR]
        v = kv_ref[0, :, MEM_WIDTH + p * PAIR:MEM_WIDTH + (p + 1) * PAIR]
        first = _first_head_lanes(qp.shape)
        acc = _softmax_pv(_dot_nt(_stack_heads(qp, first), k), v)
        outs.append(jnp.where(first, acc[:tm], acc[tm:]))
    return jnp.concatenate(outs, axis=-1)


def _finish_mixer(x_ref, y, g_ref, o_ref):
    o_ref[...] = x_ref[...] + _rms_norm(y, g_ref[...])


def _mixer_a_body(x_ref, gpre_ref, win_ref, vg_ref, sw_ref, sb_ref, kv_ref, wout_ref, gpost_ref,
                  o_ref, u_ref, v_ref, gated_ref, wc_ref):
    tm = x_ref.shape[0]
    h = _rms_norm(x_ref[...], gpre_ref[...]).astype(BF16)
    u_ref[...] = jax.nn.gelu(_dot(h, win_ref[:, :MIX_WIDTH]))
    v = jax.nn.gelu(_dot(h, win_ref[:, MIX_WIDTH:2 * MIX_WIDTH]))
    mu = jnp.mean(v, axis=-1, keepdims=True)
    vc = v - mu
    var = jnp.mean(vc * vc, axis=-1, keepdims=True)
    v_ref[...] = (vc * lax.rsqrt(var + LN_EPS) * vg_ref[...]).astype(BF16)
    qm = (_dot(h, win_ref[:, 2 * MIX_WIDTH:]) * ATTN_SCALE).astype(BF16)

    t_idx = lax.broadcasted_iota(jnp.int32, (CHUNK, CHUNK), 0)
    s_idx = lax.broadcasted_iota(jnp.int32, (CHUNK, CHUNK), 1)
    for g in range(N_MIX_HEADS):
        wc_ref[g] = jnp.where(s_idx <= t_idx, sw_ref[g], jnp.zeros((CHUNK, CHUNK), BF16))

    first = _first_head_lanes((CHUNK, PAIR))
    for c in range(tm // CHUNK):
        rows = slice(c * CHUNK, (c + 1) * CHUNK)
        for p in range(MIX_WIDTH // PAIR):
            lanes = slice(p * PAIR, (p + 1) * PAIR)
            vp = v_ref[rows, lanes]
            mixed = jnp.where(first, _dot(wc_ref[2 * p], vp), _dot(wc_ref[2 * p + 1], vp))
            gated_ref[rows, lanes] = (u_ref[rows, lanes] * (mixed + sb_ref[:, lanes])).astype(BF16)

    mem_out = _mem_attention(qm, kv_ref).astype(BF16)
    y = _dot(gated_ref[...], wout_ref[:MIX_WIDTH, :]) + _dot(mem_out, wout_ref[MIX_WIDTH:, :])
    _finish_mixer(x_ref, y, gpost_ref, o_ref)


def _mixer_a(x, seq, g_pre, g_post, kv, w_in, spatial_w, spatial_b, v_norm_g, w_out):
    m, d = x.shape
    tm = TM_MIX_A
    assert seq % tm == 0 and tm % CHUNK == 0
    tiles_per_seq = seq // tm
    n_mem = kv.shape[1]
    row = pl.BlockSpec((tm, d), lambda i: (i, 0))
    bias = jnp.repeat(spatial_b.T, HEAD_DIM, axis=1)
    return pl.pallas_call(
        _mixer_a_body,
        grid=(m // tm,),
        in_specs=[row, _const_spec((1, d)), _const_spec(w_in.shape), _const_spec((1, MIX_WIDTH)),
                  _const_spec(spatial_w.shape), _const_spec(bias.shape),
                  pl.BlockSpec((1, n_mem, 2 * MEM_WIDTH), lambda i: (i // tiles_per_seq, 0, 0)),
                  _const_spec(w_out.shape), _const_spec((1, d))],
        out_specs=row,
        out_shape=jax.ShapeDtypeStruct((m, d), F32),
        scratch_shapes=[pltpu.VMEM((tm, MIX_WIDTH), F32), pltpu.VMEM((tm, MIX_WIDTH), BF16),
                        pltpu.VMEM((tm, MIX_WIDTH), BF16),
                        pltpu.VMEM((N_MIX_HEADS, CHUNK, CHUNK), BF16)],
        compiler_params=_params("parallel"),
        name="mixer_a",
    )(x, g_pre, w_in, v_norm_g.reshape(1, MIX_WIDTH), spatial_w, bias, kv, w_out, g_post)


def _rope_body(pos_ref, freq_ref, sign_ref, cos_ref, sin_ref):
    ang = pos_ref[...].astype(F32) * freq_ref[...]
    cos_ref[...] = jnp.cos(ang)
    sin_ref[...] = jnp.sin(ang) * sign_ref[...]


def _rope_tables(positions):
    m = positions.size
    tm = TM_PROJ
    inv_freq = ROPE_THETA ** (-jnp.arange(0, HEAD_DIM, 2, dtype=F32) / HEAD_DIM)
    freq = jnp.tile(inv_freq, LANES // (HEAD_DIM // 2)).reshape(1, LANES)
    half = jnp.concatenate([-jnp.ones((HEAD_DIM // 2,), F32), jnp.ones((HEAD_DIM // 2,), F32)])
    sign = jnp.tile(half, LANES // HEAD_DIM).reshape(1, LANES)
    table = pl.BlockSpec((tm, LANES), lambda i: (i, 0))
    return pl.pallas_call(
        _rope_body,
        grid=(m // tm,),
        in_specs=[pl.BlockSpec((tm, 1), lambda i: (i, 0)), _const_spec((1, LANES)),
                  _const_spec((1, LANES))],
        out_specs=[table, table],
        out_shape=[jax.ShapeDtypeStruct((m, LANES), F32)] * 2,
        compiler_params=_params("parallel"),
        name="rope_tables",
    )(positions.reshape(m, 1), freq, sign)


def _rope(t, cos, sin_signed):
    first_half = lax.broadcasted_iota(jnp.int32, t.shape, 1) % HEAD_DIM < HEAD_DIM // 2
    swapped = jnp.where(first_half,
                        pltpu.roll(t, LANES - HEAD_DIM // 2, axis=1),
                        pltpu.roll(t, HEAD_DIM // 2, axis=1))
    return t * cos + swapped * sin_signed


def _residue_spec(tm, dilation, width, tiles_per_seq):
    return pl.BlockSpec((1, dilation, tm // dilation, width),
                        lambda i: (i // tiles_per_seq, 0, i % tiles_per_seq, 0))


def _proj_b_body(x_ref, g_ref, w_ref, cos_ref, sin_ref, g0_ref, g1_ref, g2_ref, qm_ref,
                 h_ref, t_ref):
    tm = x_ref.shape[0]
    h_ref[...] = _rms_norm(x_ref[...], g_ref[...]).astype(BF16)
    cos = cos_ref[...]
    sin = sin_ref[...]
    outs = (g0_ref, g1_ref, g2_ref)
    n_groups = len(DILATED_GROUPS)
    for c in range(3 * n_groups):
        kind, grp = divmod(c, n_groups)
        lo = c * DIL_OUT_WIDTH
        t = _dot(h_ref[...], w_ref[:, lo:lo + DIL_OUT_WIDTH])
        if kind < 2:
            scale = ATTN_SCALE if kind == 0 else 1.0
            t = jnp.concatenate(
                [_rope(t[:, j * LANES:(j + 1) * LANES], cos, sin)
                 for j in range(DIL_OUT_WIDTH // LANES)], axis=-1) * scale
        dil = DILATED_GROUPS[grp][1]
        cols = slice(kind * DIL_OUT_WIDTH, (kind + 1) * DIL_OUT_WIDTH)
        if dil == 1:
            outs[grp][0, 0, :, cols] = t.astype(BF16)
        else:
            halves = DIL_OUT_WIDTH // LANES
            for j in range(halves):
                t_ref[j] = t[:, j * LANES:(j + 1) * LANES]
            for r in range(dil):
                outs[grp][0, r, :, cols] = jnp.concatenate(
                    [t_ref[j, pl.ds(r, tm // dil, stride=dil), :] for j in range(halves)],
                    axis=-1).astype(BF16)
    qm_ref[...] = (_dot(h_ref[...], w_ref[:, 3 * MIX_WIDTH:]) * ATTN_SCALE).astype(BF16)


def _proj_b(x, batch, seq, g_pre, w_in, cos, sin):
    m, d = x.shape
    tm = TM_PROJ
    tiles_per_seq = seq // tm
    row = lambda w: pl.BlockSpec((tm, w), lambda i: (i, 0))
    width = 3 * DIL_OUT_WIDTH
    group_specs = [_residue_spec(tm, dil, width, tiles_per_seq) for _, dil in DILATED_GROUPS]
    group_shapes = [jax.ShapeDtypeStruct((batch, dil, seq // dil, width), BF16)
                    for _, dil in DILATED_GROUPS]
    return pl.pallas_call(
        _proj_b_body,
        grid=(m // tm,),
        in_specs=[row(d), _const_spec((1, d)), _const_spec(w_in.shape), row(LANES), row(LANES)],
        out_specs=group_specs + [row(MEM_WIDTH)],
        out_shape=group_shapes + [jax.ShapeDtypeStruct((m, MEM_WIDTH), BF16)],
        scratch_shapes=[pltpu.VMEM((tm, d), BF16),
                        pltpu.VMEM((DIL_OUT_WIDTH // LANES, tm, LANES), F32)],
        compiler_params=_params("parallel"),
        name="proj_b",
    )(x, g_pre, w_in, cos, sin)


def _dilated_body(qkv_ref, o_ref, lse_ref, *, n_blocks, span):
    w = DIL_OUT_WIDTH
    qi = lax.broadcasted_iota(jnp.int32, (2 * span, 2 * span), 0) % span
    ki = lax.broadcasted_iota(jnp.int32, (2 * span, 2 * span), 1)
    band = ((ki < span) & (ki >= qi)) | ((ki >= span) & (ki - span <= qi))
    own_only = (lax.broadcasted_iota(jnp.int32, (2 * span, span), 1)
                <= lax.broadcasted_iota(jnp.int32, (2 * span, span), 0) % span)
    first = _first_head_lanes((span, PAIR))
    own_lanes = jnp.concatenate([first, ~first], axis=0)

    def block(r0, has_prev):
        own = pl.ds(r0, span)
        keys = pl.ds(r0 - span, 2 * span) if has_prev else own
        for p in range(w // PAIR):
            ql = slice(p * PAIR, (p + 1) * PAIR)
            q2 = _stack_heads(qkv_ref[0, 0, own, ql], first)
            k2 = qkv_ref[0, 0, keys, w + p * PAIR:w + (p + 1) * PAIR]
            v2 = qkv_ref[0, 0, keys, 2 * w + p * PAIR:2 * w + (p + 1) * PAIR]
            s = jnp.where(band if has_prev else own_only, _dot_nt(q2, k2), NEG_INF)
            acc, lse = _softmax_pv(s, v2, with_lse=True)
            o_ref[0, 0, own, ql] = jnp.where(first, acc[:span], acc[span:]).astype(BF16)
            lse = jnp.where(own_lanes, lse, 0.0)
            lse_ref[0, 0, own, ql] = lse[:span] + lse[span:]

    block(0, False)

    def step(blk, carry):
        block(pl.multiple_of(blk * span, span), True)
        return carry

    lax.fori_loop(1, n_blocks, step, 0, unroll=2)


def _dilated_group(qkv, window, dilation):
    b, _, length, width = qkv.shape
    span = window // dilation
    assert length % span == 0
    out_spec = pl.BlockSpec((1, 1, length, DIL_OUT_WIDTH), lambda i, r: (i, r, 0, 0))
    return pl.pallas_call(
        functools.partial(_dilated_body, n_blocks=length // span, span=span),
        grid=(b, dilation),
        in_specs=[pl.BlockSpec((1, 1, length, width), lambda i, r: (i, r, 0, 0))],
        out_specs=[out_spec, out_spec],
        out_shape=[jax.ShapeDtypeStruct((b, dilation, length, DIL_OUT_WIDTH), BF16),
                   jax.ShapeDtypeStruct((b, dilation, length, DIL_OUT_WIDTH), F32)],
        compiler_params=_params("parallel", "parallel"),
        name=f"dilated_{dilation}",
    )(qkv)


def _out_b_body(x_ref, o0_ref, o1_ref, o2_ref, l0_ref, l1_ref, l2_ref, qm_ref, kv_ref, w_ref,
                g_ref, o_ref, *scratch):
    tm = x_ref.shape[0]

    def token_major(ref, buf):
        dil = ref.shape[1]
        if dil == 1:
            return ref[0, 0].astype(F32)
        halves = DIL_OUT_WIDTH // LANES
        for r in range(dil):
            rows = ref[0, r].astype(F32)
            for j in range(halves):
                buf[j, pl.ds(r, tm // dil, stride=dil), :] = rows[:, j * LANES:(j + 1) * LANES]
        return jnp.concatenate([buf[j] for j in range(halves)], axis=-1)

    outs = [token_major(r, scratch[2 * g]) for g, r in enumerate((o0_ref, o1_ref, o2_ref))]
    lses = [token_major(r, scratch[2 * g + 1]) for g, r in enumerate((l0_ref, l1_ref, l2_ref))]
    mx = jnp.maximum(jnp.maximum(lses[0], lses[1]), lses[2])
    es = [jnp.exp(l - mx) for l in lses]
    denom = es[0] + es[1] + es[2]
    merged = sum((e / denom) * o for e, o in zip(es, outs))
    mem_out = _mem_attention(qm_ref[...], kv_ref).astype(BF16)
    y = (_dot(merged.astype(BF16), w_ref[:DIL_OUT_WIDTH, :])
         + _dot(mem_out, w_ref[DIL_OUT_WIDTH:, :]))
    _finish_mixer(x_ref, y, g_ref, o_ref)


def _out_b(x, seq, outs, lses, qm, kv, w_out, g_post):
    m, d = x.shape
    tm = TM_PROJ
    tiles_per_seq = seq // tm
    n_mem = kv.shape[1]
    row = lambda w: pl.BlockSpec((tm, w), lambda i: (i, 0))
    group_specs = [_residue_spec(tm, dil, DIL_OUT_WIDTH, tiles_per_seq)
                   for _, dil in DILATED_GROUPS]
    return pl.pallas_call(
        _out_b_body,
        grid=(m // tm,),
        in_specs=[row(d)] + group_specs * 2 + [
            row(MEM_WIDTH),
            pl.BlockSpec((1, n_mem, 2 * MEM_WIDTH), lambda i: (i // tiles_per_seq, 0, 0)),
            _const_spec(w_out.shape), _const_spec((1, d))],
        out_specs=row(d),
        out_shape=jax.ShapeDtypeStruct((m, d), F32),
        scratch_shapes=[pltpu.VMEM((DIL_OUT_WIDTH // LANES, tm, LANES), F32)]
        * (2 * len(DILATED_GROUPS)),
        compiler_params=_params("parallel"),
        name="out_b",
    )(x, *outs, *lses, qm, kv, w_out, g_post)


def _split3(x):
    hi = x.astype(BF16)
    r1 = x - hi.astype(F32)
    mid = r1.astype(BF16)
    lo = (r1 - mid.astype(F32)).astype(BF16)
    return hi, mid, lo


N_BIAS_TERMS = 3
AUG_WIDTH = N_MIX_HEADS * LANES


def _pack_terms(hi, mid, lo):
    return (hi.astype(F32) + pltpu.roll(mid.astype(F32), N_MIX_HEADS, axis=1)
            + pltpu.roll(lo.astype(F32), 2 * N_MIX_HEADS, axis=1)).astype(BF16)


def _bias_selectors():
    rows = jnp.arange(LANES)
    term, head = rows // N_MIX_HEADS, rows % N_MIX_HEADS
    valid = rows < N_BIAS_TERMS * N_MIX_HEADS
    cols = jnp.arange(MIX_WIDTH)
    base = (head // 2) * PAIR + jnp.where(head % 2 == 0, HEAD_DIM, 0)

    def select(offset):
        target = base + offset + term
        return ((cols[None, :] == target[:, None]) & valid[:, None]).astype(BF16)

    def ones(offset):
        lane = cols % HEAD_DIM
        return ((lane >= offset) & (lane < offset + N_BIAS_TERMS)).astype(F32)[None, :]

    q_sel, k_sel = select(0), select(N_BIAS_TERMS)
    q_one, k_one = ones(N_BIAS_TERMS), ones(0)
    return q_sel, k_sel, q_one, k_one


def _proj_c_body(x_ref, g_ref, w_ref, wvt_ref, fb_ref, qsel_ref, ksel_ref, qone_ref, kone_ref,
                 q_ref, k_ref, vt_ref, qm_ref, h_ref, carry_ref, qext_ref, kext_ref,
                 *, tiles_per_seq):
    tm = x_ref.shape[0]
    qkv_w = 3 * MIX_WIDTH

    @pl.when(pl.program_id(0) % tiles_per_seq == 0)
    def _():
        carry_ref[...] = jnp.zeros_like(carry_ref)

    h_ref[...] = _rms_norm(x_ref[...], g_ref[...]).astype(BF16)

    z = _dot(h_ref[...], w_ref[:, qkv_w:qkv_w + LANES]) + fb_ref[...]
    log_f = jnp.minimum(z, 0.0) - jnp.log1p(jnp.exp(-jnp.abs(z)))
    lane = lax.broadcasted_iota(jnp.int32, (tm, LANES), 1)
    log_f = jnp.where(lane < N_MIX_HEADS, log_f, 0.0)
    tri = (lax.broadcasted_iota(jnp.int32, (tm, tm), 1)
           <= lax.broadcasted_iota(jnp.int32, (tm, tm), 0)).astype(BF16)
    sums = _dot(tri, _pack_terms(*_split3(log_f)))
    local = (sums + pltpu.roll(sums, LANES - N_MIX_HEADS, axis=1)
             + pltpu.roll(sums, LANES - 2 * N_MIX_HEADS, axis=1))
    c = carry_ref[...] + jnp.where(lane < N_MIX_HEADS, local, 0.0)
    carry_ref[...] = c[tm - 1:tm, :]
    terms = _pack_terms(*_split3(c))
    qext_ref[...] = _dot(terms, qsel_ref[...]) + qone_ref[...]
    kext_ref[...] = kone_ref[...] - _dot(terms, ksel_ref[...])

    first = _first_head_lanes((tm, PAIR))
    for c_idx in range(2 * MIX_WIDTH // MXU_N):
        lo = c_idx * MXU_N
        is_q = lo < MIX_WIDTH
        t = _dot(h_ref[...], w_ref[:, lo:lo + MXU_N])
        if is_q:
            t = t * ATTN_SCALE
        dst, ext_ref = (q_ref, qext_ref) if is_q else (k_ref, kext_ref)
        for j in range(MXU_N // PAIR):
            p = (lo % MIX_WIDTH) // PAIR + j
            pair = t[:, j * PAIR:(j + 1) * PAIR]
            ext = ext_ref[:, p * PAIR:(p + 1) * PAIR]
            dst[:, 2 * p * LANES:(2 * p + 1) * LANES] = jnp.where(first, pair, ext).astype(BF16)
            dst[:, (2 * p + 1) * LANES:(2 * p + 2) * LANES] = pltpu.roll(
                jnp.where(first, ext, pair), HEAD_DIM, axis=1).astype(BF16)
    for c_idx in range(MIX_WIDTH // MXU_N):
        lo = c_idx * MXU_N
        vt_ref[0, lo:lo + MXU_N, :] = _dot_nt(wvt_ref[lo:lo + MXU_N, :], h_ref[...]).astype(BF16)
    qm_ref[...] = (_dot(h_ref[...], w_ref[:, qkv_w + LANES:]) * ATTN_SCALE).astype(BF16)


def _proj_c(x, seq, g_pre, w_in, forget_bias):
    m, d = x.shape
    tm = TK_FOX
    qkv_w = 3 * MIX_WIDTH
    pad = jnp.zeros((d, LANES - N_MIX_HEADS), w_in.dtype)
    w = jnp.concatenate([w_in[:, :qkv_w + N_MIX_HEADS], pad, w_in[:, qkv_w + N_MIX_HEADS:]], axis=1)
    w_v_t = w_in[:, 2 * MIX_WIDTH:qkv_w].T
    fb = jnp.concatenate([forget_bias.astype(F32), jnp.zeros((LANES - N_MIX_HEADS,), F32)])
    consts = _bias_selectors()
    row = lambda w_: pl.BlockSpec((tm, w_), lambda i: (i, 0))
    return pl.pallas_call(
        functools.partial(_proj_c_body, tiles_per_seq=seq // tm),
        grid=(m // tm,),
        in_specs=[row(d), _const_spec((1, d)), _const_spec(w.shape), _const_spec(w_v_t.shape),
                  _const_spec((1, LANES))] + [_const_spec(c.shape) for c in consts],
        out_specs=[row(AUG_WIDTH), row(AUG_WIDTH),
                   pl.BlockSpec((1, MIX_WIDTH, tm), lambda i: (i, 0, 0)), row(MEM_WIDTH)],
        out_shape=[jax.ShapeDtypeStruct((m, AUG_WIDTH), BF16),
                   jax.ShapeDtypeStruct((m, AUG_WIDTH), BF16),
                   jax.ShapeDtypeStruct((m // tm, MIX_WIDTH, tm), BF16),
                   jax.ShapeDtypeStruct((m, MEM_WIDTH), BF16)],
        scratch_shapes=[pltpu.VMEM((tm, d), BF16), pltpu.VMEM((1, LANES), F32),
                        pltpu.VMEM((tm, MIX_WIDTH), F32), pltpu.VMEM((tm, MIX_WIDTH), F32)],
        compiler_params=_params("arbitrary"),
        name="proj_c",
    )(x, g_pre, w, w_v_t, fb.reshape(1, LANES), *consts)


def _fox_body(q_ref, k_ref, vt_ref, o_ref, acc_ref, *, tq, tk):
    i = pl.program_id(2)
    tiles_per_q = tq // tk
    distance = (lax.broadcasted_iota(jnp.int32, (tk, tq), 1)
                - lax.broadcasted_iota(jnp.int32, (tk, tq), 0))
    acc_ref[...] = jnp.zeros_like(acc_ref)

    def kv_step(j, carry, diag):
        rows = pl.ds(pl.multiple_of(j * tk, tk), tk)
        new = []
        for h in range(FOX_HEADS):
            m_run, l_run = carry[h]
            lanes = slice(h * LANES, (h + 1) * LANES)
            st = _dot_nt(k_ref[0, rows, lanes], q_ref[0, :, lanes])
            if diag is not None:
                st = jnp.where(distance >= diag * tk, st, NEG_INF)
            m_new = jnp.maximum(m_run, jnp.max(st, axis=0, keepdims=True))
            alpha = jnp.exp(m_run - m_new)
            e = jnp.exp(st - m_new)
            l_new = alpha * l_run + jnp.sum(e, axis=0, keepdims=True)
            vt = vt_ref[j, (h // 2) * PAIR:(h // 2 + 1) * PAIR, :]
            acc_ref[h] = alpha * acc_ref[h] + _dot(vt, e.astype(BF16))
            new.append((m_new, l_new))
        return tuple(new)

    init = tuple((jnp.full((1, tq), NEG_INF, F32), jnp.zeros((1, tq), F32))
                 for _ in range(FOX_HEADS))
    carry = lax.fori_loop(0, i * tiles_per_q, functools.partial(kv_step, diag=None), init)
    for t in range(tiles_per_q):
        carry = kv_step(i * tiles_per_q + t, carry, t)
    even_rows = lax.broadcasted_iota(jnp.int32, (PAIR, tq), 0) < HEAD_DIM
    for p in range(FOX_HEADS // 2):
        denom = jnp.where(even_rows, carry[2 * p][1], carry[2 * p + 1][1])
        out_t = jnp.where(even_rows, acc_ref[2 * p], acc_ref[2 * p + 1]) / denom
        o_ref[0, :, p * PAIR:(p + 1) * PAIR] = out_t.T.astype(BF16)


def _fox_attention(q_aug, k_aug, v_t, batch, seq):
    tq, tk = TQ_FOX, TK_FOX
    q3 = q_aug.reshape(batch, seq, AUG_WIDTH)
    k3 = k_aug.reshape(batch, seq, AUG_WIDTH)
    aug_w = FOX_HEADS * LANES
    v_w = FOX_HEADS * HEAD_DIM
    out = pl.pallas_call(
        functools.partial(_fox_body, tq=tq, tk=tk),
        grid=(batch, N_MIX_HEADS // FOX_HEADS, seq // tq),
        in_specs=[pl.BlockSpec((1, tq, aug_w), lambda b, p, i: (b, i, p)),
                  pl.BlockSpec((1, seq, aug_w), lambda b, p, i: (b, 0, p)),
                  pl.BlockSpec((seq // tk, v_w, tk), lambda b, p, i: (b, p, 0))],
        out_specs=pl.BlockSpec((1, tq, v_w), lambda b, p, i: (b, i, p)),
        out_shape=jax.ShapeDtypeStruct((batch, seq, MIX_WIDTH), BF16),
        scratch_shapes=[pltpu.VMEM((FOX_HEADS, PAIR, tq), F32)],
        compiler_params=_params("parallel", "parallel", "arbitrary"),
        name="fox_attention",
    )(q3, k3, v_t)
    return out.reshape(batch * seq, MIX_WIDTH)


def _out_c_body(x_ref, att_ref, qm_ref, kv_ref, w_ref, g_ref, o_ref):
    mem_out = _mem_attention(qm_ref[...], kv_ref).astype(BF16)
    y = _dot(att_ref[...], w_ref[:MIX_WIDTH, :]) + _dot(mem_out, w_ref[MIX_WIDTH:, :])
    _finish_mixer(x_ref, y, g_ref, o_ref)


def _out_c(x, seq, att, qm, kv, w_out, g_post):
    m, d = x.shape
    tm = TM_PROJ
    tiles_per_seq = seq // tm
    n_mem = kv.shape[1]
    row = lambda w: pl.BlockSpec((tm, w), lambda i: (i, 0))
    return pl.pallas_call(
        _out_c_body,
        grid=(m // tm,),
        in_specs=[row(d), row(MIX_WIDTH), row(MEM_WIDTH),
                  pl.BlockSpec((1, n_mem, 2 * MEM_WIDTH), lambda i: (i // tiles_per_seq, 0, 0)),
                  _const_spec(w_out.shape), _const_spec((1, d))],
        out_specs=row(d),
        out_shape=jax.ShapeDtypeStruct((m, d), F32),
        compiler_params=_params("parallel"),
        name="out_c",
    )(x, att, qm, kv, w_out, g_post)


def kernel(x, mem, positions, norm_g, mem_norm_g, w_mem_kv, ffn_w_gate_up, ffn_w_down,
           a_w_in, a_spatial_w, a_spatial_b, a_v_norm_g, a_w_out,
           b_w_in, b_w_out, c_w_in, c_forget_bias, c_w_out):
    batch, seq, d = x.shape
    depth = norm_g.shape[0]
    bf = lambda w: w.astype(BF16)

    kv_all = _mem_kv(mem, mem_norm_g, bf(w_mem_kv))
    cos = sin = None
    xf = x.reshape(batch * seq, d)
    for i in range(depth):
        kind, j = i % 3, i // 3
        g = norm_g[i].reshape(norm_g.shape[1], 1, d)
        xf = _ffn(xf, g[0], g[1], bf(ffn_w_gate_up[i, 0]), bf(ffn_w_down[i, 0]))
        kv = kv_all[i]
        if kind == 0:
            xf = _mixer_a(xf, seq, g[2], g[3], kv, bf(a_w_in[j]), bf(a_spatial_w[j]),
                          a_spatial_b[j], a_v_norm_g[j], bf(a_w_out[j]))
        elif kind == 1:
            if cos is None:
                cos, sin = _rope_tables(positions)
            *groups, qm = _proj_b(xf, batch, seq, g[2], bf(b_w_in[j]), cos, sin)
            outs, lses = zip(*[_dilated_group(qkv_g, w, dil)
                               for qkv_g, (w, dil) in zip(groups, DILATED_GROUPS)])
            xf = _out_b(xf, seq, outs, lses, qm, kv, bf(b_w_out[j]), g[3])
        else:
            q_aug, k_aug, v_t, qm = _proj_c(xf, seq, g[2], bf(c_w_in[j]), c_forget_bias[j])
            att = _fox_attention(q_aug, k_aug, v_t, batch, seq)
            xf = _out_c(xf, seq, att, qm, kv, bf(c_w_out[j]), g[3])
        xf = _ffn(xf, g[4], g[5], bf(ffn_w_gate_up[i, 1]), bf(ffn_w_down[i, 1]))
    return xf.reshape(batch, seq, d)
```

```python
import functools

import jax
import jax.numpy as jnp
from jax import lax
from jax.experimental import pallas as pl
from jax.experimental.pallas import tpu as pltpu

F32 = jnp.float32
BF16 = jnp.bfloat16

HEAD_DIM = 64
N_MIX_HEADS = 12
MIX_WIDTH = N_MIX_HEADS * HEAD_DIM
N_MEM_HEADS = 4
MEM_WIDTH = N_MEM_HEADS * HEAD_DIM
CHUNK = 128
ROPE_THETA = 10000.0
DILATED_GROUPS = ((128, 1), (512, 4), (2048, 16))
DIL_OUT_WIDTH = (N_MIX_HEADS // len(DILATED_GROUPS)) * HEAD_DIM
RMS_EPS = 1e-6
LN_EPS = 1e-5
NEG_INF = -1e30
ATTN_SCALE = HEAD_DIM ** -0.5

LANES = 128
PAIR = 2 * HEAD_DIM
MXU_N = 256
VMEM_LIMIT_BYTES = 56 * 1024 * 1024

TM_FFN = 1024
FFN_SUB = 512
TM_PROJ = 512
TM_MIX_A = 512
TQ_FOX = 512
TK_FOX = 512
FOX_HEADS = 4
DIL_CHAINS = 4
FF_CHUNK = 256


def _params(*semantics):
    return pltpu.CompilerParams(dimension_semantics=semantics,
                                vmem_limit_bytes=VMEM_LIMIT_BYTES)


def _const_spec(shape):
    zeros = (0,) * len(shape)
    return pl.BlockSpec(shape, lambda *_: zeros, pipeline_mode=pl.Buffered(1))


def _rms_norm(x, g):
    ms = jnp.mean(x * x, axis=-1, keepdims=True)
    return x * lax.rsqrt(ms + RMS_EPS) * g


def _dot(a, b):
    return jnp.dot(a, b, preferred_element_type=F32)


def _dot_nt(a, b):
    return lax.dot_general(a, b, (((1,), (1,)), ((), ())), preferred_element_type=F32)


def _first_head_lanes(shape):
    return lax.broadcasted_iota(jnp.int32, shape, len(shape) - 1) % PAIR < HEAD_DIM


def _ffn_body(x_ref, gpre_ref, gpost_ref, wgu_ref, wd_ref, o_ref, xn_ref, act_ref, *, d_ff):
    for sub in range(x_ref.shape[0] // FFN_SUB):
        rows = slice(sub * FFN_SUB, (sub + 1) * FFN_SUB)
        xn_ref[rows, :] = _rms_norm(x_ref[rows, :], gpre_ref[...]).astype(BF16)
        for c in range(d_ff // FF_CHUNK):
            lo = c * FF_CHUNK
            gate = _dot(xn_ref[rows, :], wgu_ref[:, lo:lo + FF_CHUNK])
            up = _dot(xn_ref[rows, :], wgu_ref[:, d_ff + lo:d_ff + lo + FF_CHUNK])
            act_ref[rows, lo:lo + FF_CHUNK] = (gate * jax.nn.sigmoid(gate) * up).astype(BF16)
        y = _dot(act_ref[rows, :], wd_ref[...])
        o_ref[rows, :] = x_ref[rows, :] + 0.5 * _rms_norm(y, gpost_ref[...])


def _stacked_spec(stacked, index):
    k = len(index)
    tail = stacked.shape[k:]
    where = tuple(index) + (0,) * len(tail)
    return pl.BlockSpec((None,) * k + tail, lambda *_: where, pipeline_mode=pl.Buffered(1))


def _ffn(x, g_pre, g_post, w_gate_up, w_down, index):
    m, d = x.shape
    d_ff = w_down.shape[-2]
    assert m % TM_FFN == 0 and d_ff % FF_CHUNK == 0
    row = pl.BlockSpec((TM_FFN, d), lambda i: (i, 0))
    return pl.pallas_call(
        functools.partial(_ffn_body, d_ff=d_ff),
        grid=(m // TM_FFN,),
        in_specs=[row, _const_spec((1, d)), _const_spec((1, d)),
                  _stacked_spec(w_gate_up, index), _stacked_spec(w_down, index)],
        out_specs=row,
        out_shape=jax.ShapeDtypeStruct((m, d), F32),
        scratch_shapes=[pltpu.VMEM((TM_FFN, d), BF16), pltpu.VMEM((TM_FFN, d_ff), BF16)],
        compiler_params=_params("parallel"),
        name="ffn",
    )(x, g_pre, g_post, w_gate_up, w_down)


def _mem_kv_body(mem_ref, g_ref, w_ref, o_ref):
    o_ref[0, 0] = _dot(_rms_norm(mem_ref[0], g_ref[0]).astype(BF16), w_ref[0]).astype(BF16)


def _mem_kv(mem, mem_norm_g, w_mem_kv):
    b, n_mem, d = mem.shape
    depth = w_mem_kv.shape[0]
    return pl.pallas_call(
        _mem_kv_body,
        grid=(depth, b),
        in_specs=[pl.BlockSpec((1, n_mem, d), lambda l, i: (i, 0, 0)),
                  pl.BlockSpec((1, 1, d), lambda l, i: (l, 0, 0)),
                  pl.BlockSpec((1, d, 2 * MEM_WIDTH), lambda l, i: (l, 0, 0))],
        out_specs=pl.BlockSpec((1, 1, n_mem, 2 * MEM_WIDTH), lambda l, i: (l, i, 0, 0)),
        out_shape=jax.ShapeDtypeStruct((depth, b, n_mem, 2 * MEM_WIDTH), BF16),
        compiler_params=_params("parallel", "parallel"),
        name="mem_kv",
    )(mem, mem_norm_g.reshape(depth, 1, d), w_mem_kv)


def _stack_heads(qp, first):
    zero = jnp.zeros_like(qp)
    return jnp.concatenate([jnp.where(first, qp, zero), jnp.where(first, zero, qp)], axis=0)


def _softmax_pv(s, v, with_lse=False):
    mx = jnp.max(s, axis=-1, keepdims=True)
    e = jnp.exp(s - mx)
    denom = jnp.sum(e, axis=-1, keepdims=True)
    out = _dot(e.astype(BF16), v) / denom
    if with_lse:
        return out, mx + jnp.log(denom)
    return out


def _mem_attention(qm, kv_ref):
    tm = qm.shape[0]
    outs = []
    for p in range(MEM_WIDTH // PAIR):
        qp = qm[:, p * PAIR:(p + 1) * PAIR]
        k = kv_ref[0, :, p * PAIR:(p + 1) * PAIR]
        v = kv_ref[0, :, MEM_WIDTH + p * PAIR:MEM_WIDTH + (p + 1) * PAIR]
        first = _first_head_lanes(qp.shape)
        acc = _softmax_pv(_dot_nt(_stack_heads(qp, first), k), v)
        outs.append(jnp.where(first, acc[:tm], acc[tm:]))
    return jnp.concatenate(outs, axis=-1)


def _finish_mixer(x_ref, y, g_ref, o_ref):
    o_ref[...] = x_ref[...] + _rms_norm(y, g_ref[...])


def _mixer_a_body(x_ref, gpre_ref, win_ref, vg_ref, sw_ref, sb_ref, kv_ref, wout_ref, gpost_ref,
                  o_ref, u_ref, v_ref, gated_ref, wc_ref):
    tm = x_ref.shape[0]
    h = _rms_norm(x_ref[...], gpre_ref[...]).astype(BF16)
    u_ref[...] = jax.nn.gelu(_dot(h, win_ref[:, :MIX_WIDTH]))
    v = jax.nn.gelu(_dot(h, win_ref[:, MIX_WIDTH:2 * MIX_WIDTH]))
    mu = jnp.mean(v, axis=-1, keepdims=True)
    vc = v - mu
    var = jnp.mean(vc * vc, axis=-1, keepdims=True)
    v_ref[...] = (vc * lax.rsqrt(var + LN_EPS) * vg_ref[...]).astype(BF16)
    qm = (_dot(h, win_ref[:, 2 * MIX_WIDTH:]) * ATTN_SCALE).astype(BF16)

    t_idx = lax.broadcasted_iota(jnp.int32, (CHUNK, CHUNK), 0)
    s_idx = lax.broadcasted_iota(jnp.int32, (CHUNK, CHUNK), 1)
    for g in range(N_MIX_HEADS):
        wc_ref[g] = jnp.where(s_idx <= t_idx, sw_ref[g], jnp.zeros((CHUNK, CHUNK), BF16))

    first = _first_head_lanes((CHUNK, PAIR))
    for c in range(tm // CHUNK):
        rows = slice(c * CHUNK, (c + 1) * CHUNK)
        for p in range(MIX_WIDTH // PAIR):
            lanes = slice(p * PAIR, (p + 1) * PAIR)
            vp = v_ref[rows, lanes]
            mixed = jnp.where(first, _dot(wc_ref[2 * p], vp), _dot(wc_ref[2 * p + 1], vp))
            gated_ref[rows, lanes] = (u_ref[rows, lanes] * (mixed + sb_ref[:, lanes])).astype(BF16)

    mem_out = _mem_attention(qm, kv_ref).astype(BF16)
    y = _dot(gated_ref[...], wout_ref[:MIX_WIDTH, :]) + _dot(mem_out, wout_ref[MIX_WIDTH:, :])
    _finish_mixer(x_ref, y, gpost_ref, o_ref)


def _mixer_a(x, seq, g_pre, g_post, kv, w_in, spatial_w, spatial_b, v_norm_g, w_out):
    m, d = x.shape
    tm = TM_MIX_A
    assert seq % tm == 0 and tm % CHUNK == 0
    tiles_per_seq = seq // tm
    n_mem = kv.shape[1]
    row = pl.BlockSpec((tm, d), lambda i: (i, 0))
    bias = jnp.repeat(spatial_b.T, HEAD_DIM, axis=1)
    return pl.pallas_call(
        _mixer_a_body,
        grid=(m // tm,),
        in_specs=[row, _const_spec((1, d)), _const_spec(w_in.shape), _const_spec((1, MIX_WIDTH)),
                  _const_spec(spatial_w.shape), _const_spec(bias.shape),
                  pl.BlockSpec((1, n_mem, 2 * MEM_WIDTH), lambda i: (i // tiles_per_seq, 0, 0)),
                  _const_spec(w_out.shape), _const_spec((1, d))],
        out_specs=row,
        out_shape=jax.ShapeDtypeStruct((m, d), F32),
        scratch_shapes=[pltpu.VMEM((tm, MIX_WIDTH), F32), pltpu.VMEM((tm, MIX_WIDTH), BF16),
                        pltpu.VMEM((tm, MIX_WIDTH), BF16),
                        pltpu.VMEM((N_MIX_HEADS, CHUNK, CHUNK), BF16)],
        compiler_params=_params("parallel"),
        name="mixer_a",
    )(x, g_pre, w_in, v_norm_g.reshape(1, MIX_WIDTH), spatial_w, bias, kv, w_out, g_post)


def _rope_body(pos_ref, freq_ref, sign_ref, cos_ref, sin_ref):
    ang = pos_ref[...].astype(F32) * freq_ref[...]
    cos_ref[...] = jnp.cos(ang)
    sin_ref[...] = jnp.sin(ang) * sign_ref[...]


def _rope_tables(positions):
    m = positions.size
    tm = TM_PROJ
    inv_freq = ROPE_THETA ** (-jnp.arange(0, HEAD_DIM, 2, dtype=F32) / HEAD_DIM)
    freq = jnp.tile(inv_freq, LANES // (HEAD_DIM // 2)).reshape(1, LANES)
    half = jnp.concatenate([-jnp.ones((HEAD_DIM // 2,), F32), jnp.ones((HEAD_DIM // 2,), F32)])
    sign = jnp.tile(half, LANES // HEAD_DIM).reshape(1, LANES)
    table = pl.BlockSpec((tm, LANES), lambda i: (i, 0))
    return pl.pallas_call(
        _rope_body,
        grid=(m // tm,),
        in_specs=[pl.BlockSpec((tm, 1), lambda i: (i, 0)), _const_spec((1, LANES)),
                  _const_spec((1, LANES))],
        out_specs=[table, table],
        out_shape=[jax.ShapeDtypeStruct((m, LANES), F32)] * 2,
        compiler_params=_params("parallel"),
        name="rope_tables",
    )(positions.reshape(m, 1), freq, sign)


def _rope(t, cos, sin_signed):
    first_half = lax.broadcasted_iota(jnp.int32, t.shape, 1) % HEAD_DIM < HEAD_DIM // 2
    swapped = jnp.where(first_half,
                        pltpu.roll(t, LANES - HEAD_DIM // 2, axis=1),
                        pltpu.roll(t, HEAD_DIM // 2, axis=1))
    return t * cos + swapped * sin_signed


def _residue_spec(tm, dilation, width, tiles_per_seq):
    return pl.BlockSpec((1, dilation, tm // dilation, width),
                        lambda i: (i // tiles_per_seq, 0, i % tiles_per_seq, 0))


def _proj_b_body(x_ref, g_ref, w_ref, cos_ref, sin_ref, g0_ref, g1_ref, g2_ref, qm_ref,
                 h_ref, t_ref):
    tm = x_ref.shape[0]
    h_ref[...] = _rms_norm(x_ref[...], g_ref[...]).astype(BF16)
    cos = cos_ref[...]
    sin = sin_ref[...]
    outs = (g0_ref, g1_ref, g2_ref)
    n_groups = len(DILATED_GROUPS)
    for c in range(3 * n_groups):
        kind, grp = divmod(c, n_groups)
        lo = c * DIL_OUT_WIDTH
        t = _dot(h_ref[...], w_ref[:, lo:lo + DIL_OUT_WIDTH])
        if kind < 2:
            scale = ATTN_SCALE if kind == 0 else 1.0
            t = jnp.concatenate(
                [_rope(t[:, j * LANES:(j + 1) * LANES], cos, sin)
                 for j in range(DIL_OUT_WIDTH // LANES)], axis=-1) * scale
        dil = DILATED_GROUPS[grp][1]
        cols = slice(kind * DIL_OUT_WIDTH, (kind + 1) * DIL_OUT_WIDTH)
        if dil == 1:
            outs[grp][0, 0, :, cols] = t.astype(BF16)
        else:
            halves = DIL_OUT_WIDTH // LANES
            for j in range(halves):
                t_ref[j] = t[:, j * LANES:(j + 1) * LANES]
            for r in range(dil):
                outs[grp][0, r, :, cols] = jnp.concatenate(
                    [t_ref[j, pl.ds(r, tm // dil, stride=dil), :] for j in range(halves)],
                    axis=-1).astype(BF16)
    qm_ref[...] = (_dot(h_ref[...], w_ref[:, 3 * MIX_WIDTH:]) * ATTN_SCALE).astype(BF16)


def _proj_b(x, batch, seq, g_pre, w_in, cos, sin):
    m, d = x.shape
    tm = TM_PROJ
    tiles_per_seq = seq // tm
    row = lambda w: pl.BlockSpec((tm, w), lambda i: (i, 0))
    width = 3 * DIL_OUT_WIDTH
    group_specs = [_residue_spec(tm, dil, width, tiles_per_seq) for _, dil in DILATED_GROUPS]
    group_shapes = [jax.ShapeDtypeStruct((batch, dil, seq // dil, width), BF16)
                    for _, dil in DILATED_GROUPS]
    return pl.pallas_call(
        _proj_b_body,
        grid=(m // tm,),
        in_specs=[row(d), _const_spec((1, d)), _const_spec(w_in.shape), row(LANES), row(LANES)],
        out_specs=group_specs + [row(MEM_WIDTH)],
        out_shape=group_shapes + [jax.ShapeDtypeStruct((m, MEM_WIDTH), BF16)],
        scratch_shapes=[pltpu.VMEM((tm, d), BF16),
                        pltpu.VMEM((DIL_OUT_WIDTH // LANES, tm, LANES), F32)],
        compiler_params=_params("parallel"),
        name="proj_b",
    )(x, g_pre, w_in, cos, sin)


def _dilated_body(qkv_ref, o_ref, lse_ref, *, n_blocks, span, unroll):
    w = DIL_OUT_WIDTH
    qi = lax.broadcasted_iota(jnp.int32, (2 * span, 2 * span), 0) % span
    ki = lax.broadcasted_iota(jnp.int32, (2 * span, 2 * span), 1)
    band = ((ki < span) & (ki >= qi)) | ((ki >= span) & (ki - span <= qi))
    own_only = (lax.broadcasted_iota(jnp.int32, (2 * span, span), 1)
                <= lax.broadcasted_iota(jnp.int32, (2 * span, span), 0) % span)
    first = _first_head_lanes((span, PAIR))
    own_lanes = jnp.concatenate([first, ~first], axis=0)

    def block(r0, has_prev):
        own = pl.ds(r0, span)
        keys = pl.ds(r0 - span, 2 * span) if has_prev else own
        for r in range(qkv_ref.shape[1]):
            for p in range(w // PAIR):
                ql = slice(p * PAIR, (p + 1) * PAIR)
                q2 = _stack_heads(qkv_ref[0, r, own, ql], first)
                k2 = qkv_ref[0, r, keys, w + p * PAIR:w + (p + 1) * PAIR]
                v2 = qkv_ref[0, r, keys, 2 * w + p * PAIR:2 * w + (p + 1) * PAIR]
                s = jnp.where(band if has_prev else own_only, _dot_nt(q2, k2), NEG_INF)
                acc, lse = _softmax_pv(s, v2, with_lse=True)
                o_ref[0, r, own, ql] = jnp.where(first, acc[:span], acc[span:]).astype(BF16)
                lse = jnp.where(own_lanes, lse, 0.0)
                lse_ref[0, r, own, ql] = lse[:span] + lse[span:]

    block(0, False)

    def step(blk, carry):
        block(pl.multiple_of(blk * span, span), True)
        return carry

    lax.fori_loop(1, n_blocks, step, 0, unroll=unroll)


def _dilated_group(qkv, window, dilation):
    b, _, length, width = qkv.shape
    span = window // dilation
    assert length % span == 0
    per_step = min(dilation, DIL_CHAINS)
    unroll = DIL_CHAINS // per_step
    assert dilation % per_step == 0
    out_spec = pl.BlockSpec((1, per_step, length, DIL_OUT_WIDTH), lambda i, r: (i, r, 0, 0))
    return pl.pallas_call(
        functools.partial(_dilated_body, n_blocks=length // span, span=span, unroll=unroll),
        grid=(b, dilation // per_step),
        in_specs=[pl.BlockSpec((1, per_step, length, width), lambda i, r: (i, r, 0, 0))],
        out_specs=[out_spec, out_spec],
        out_shape=[jax.ShapeDtypeStruct((b, dilation, length, DIL_OUT_WIDTH), BF16),
                   jax.ShapeDtypeStruct((b, dilation, length, DIL_OUT_WIDTH), F32)],
        compiler_params=_params("parallel", "parallel"),
        name=f"dilated_{dilation}",
    )(qkv)


def _out_b_body(x_ref, o0_ref, o1_ref, o2_ref, l0_ref, l1_ref, l2_ref, qm_ref, kv_ref, w_ref,
                g_ref, o_ref, *scratch):
    tm = x_ref.shape[0]

    def token_major(ref, buf):
        dil = ref.shape[1]
        if dil == 1:
            return ref[0, 0].astype(F32)
        halves = DIL_OUT_WIDTH // LANES
        for r in range(dil):
            rows = ref[0, r].astype(F32)
            for j in range(halves):
                buf[j, pl.ds(r, tm // dil, stride=dil), :] = rows[:, j * LANES:(j + 1) * LANES]
        return jnp.concatenate([buf[j] for j in range(halves)], axis=-1)

    outs = [token_major(r, scratch[2 * g]) for g, r in enumerate((o0_ref, o1_ref, o2_ref))]
    lses = [token_major(r, scratch[2 * g + 1]) for g, r in enumerate((l0_ref, l1_ref, l2_ref))]
    mx = jnp.maximum(jnp.maximum(lses[0], lses[1]), lses[2])
    es = [jnp.exp(l - mx) for l in lses]
    denom = es[0] + es[1] + es[2]
    merged = sum((e / denom) * o for e, o in zip(es, outs))
    mem_out = _mem_attention(qm_ref[...], kv_ref).astype(BF16)
    y = (_dot(merged.astype(BF16), w_ref[:DIL_OUT_WIDTH, :])
         + _dot(mem_out, w_ref[DIL_OUT_WIDTH:, :]))
    _finish_mixer(x_ref, y, g_ref, o_ref)


def _out_b(x, seq, outs, lses, qm, kv, w_out, g_post):
    m, d = x.shape
    tm = TM_PROJ
    tiles_per_seq = seq // tm
    n_mem = kv.shape[1]
    row = lambda w: pl.BlockSpec((tm, w), lambda i: (i, 0))
    group_specs = [_residue_spec(tm, dil, DIL_OUT_WIDTH, tiles_per_seq)
                   for _, dil in DILATED_GROUPS]
    return pl.pallas_call(
        _out_b_body,
        grid=(m // tm,),
        in_specs=[row(d)] + group_specs * 2 + [
            row(MEM_WIDTH),
            pl.BlockSpec((1, n_mem, 2 * MEM_WIDTH), lambda i: (i // tiles_per_seq, 0, 0)),
            _const_spec(w_out.shape), _const_spec((1, d))],
        out_specs=row(d),
        out_shape=jax.ShapeDtypeStruct((m, d), F32),
        scratch_shapes=[pltpu.VMEM((DIL_OUT_WIDTH // LANES, tm, LANES), F32)]
        * (2 * len(DILATED_GROUPS)),
        compiler_params=_params("parallel"),
        name="out_b",
    )(x, *outs, *lses, qm, kv, w_out, g_post)


def _split3(x):
    hi = x.astype(BF16)
    r1 = x - hi.astype(F32)
    mid = r1.astype(BF16)
    lo = (r1 - mid.astype(F32)).astype(BF16)
    return hi, mid, lo


N_BIAS_TERMS = 3
AUG_WIDTH = N_MIX_HEADS * LANES


def _pack_terms(hi, mid, lo):
    return (hi.astype(F32) + pltpu.roll(mid.astype(F32), N_MIX_HEADS, axis=1)
            + pltpu.roll(lo.astype(F32), 2 * N_MIX_HEADS, axis=1)).astype(BF16)


def _bias_selectors():
    rows = jnp.arange(LANES)
    term, head = rows // N_MIX_HEADS, rows % N_MIX_HEADS
    valid = rows < N_BIAS_TERMS * N_MIX_HEADS
    cols = jnp.arange(MIX_WIDTH)
    base = (head // 2) * PAIR + jnp.where(head % 2 == 0, HEAD_DIM, 0)

    def select(offset):
        target = base + offset + term
        return ((cols[None, :] == target[:, None]) & valid[:, None]).astype(BF16)

    def ones(offset):
        lane = cols % HEAD_DIM
        return ((lane >= offset) & (lane < offset + N_BIAS_TERMS)).astype(F32)[None, :]

    q_sel, k_sel = select(0), select(N_BIAS_TERMS)
    q_one, k_one = ones(N_BIAS_TERMS), ones(0)
    return q_sel, k_sel, q_one, k_one


def _proj_c_body(x_ref, g_ref, w_ref, wvt_ref, fb_ref, qsel_ref, ksel_ref, qone_ref, kone_ref,
                 q_ref, k_ref, vt_ref, qm_ref, h_ref, carry_ref, qext_ref, kext_ref,
                 *, tiles_per_seq):
    tm = x_ref.shape[0]
    qkv_w = 3 * MIX_WIDTH

    @pl.when(pl.program_id(0) % tiles_per_seq == 0)
    def _():
        carry_ref[...] = jnp.zeros_like(carry_ref)

    h_ref[...] = _rms_norm(x_ref[...], g_ref[...]).astype(BF16)

    z = _dot(h_ref[...], w_ref[:, qkv_w:qkv_w + LANES]) + fb_ref[...]
    log_f = jnp.minimum(z, 0.0) - jnp.log1p(jnp.exp(-jnp.abs(z)))
    lane = lax.broadcasted_iota(jnp.int32, (tm, LANES), 1)
    log_f = jnp.where(lane < N_MIX_HEADS, log_f, 0.0)
    tri = (lax.broadcasted_iota(jnp.int32, (tm, tm), 1)
           <= lax.broadcasted_iota(jnp.int32, (tm, tm), 0)).astype(BF16)
    sums = _dot(tri, _pack_terms(*_split3(log_f)))
    local = (sums + pltpu.roll(sums, LANES - N_MIX_HEADS, axis=1)
             + pltpu.roll(sums, LANES - 2 * N_MIX_HEADS, axis=1))
    c = carry_ref[...] + jnp.where(lane < N_MIX_HEADS, local, 0.0)
    carry_ref[...] = c[tm - 1:tm, :]
    terms = _pack_terms(*_split3(c))
    qext_ref[...] = _dot(terms, qsel_ref[...]) + qone_ref[...]
    kext_ref[...] = kone_ref[...] - _dot(terms, ksel_ref[...])

    first = _first_head_lanes((tm, PAIR))
    for c_idx in range(2 * MIX_WIDTH // MXU_N):
        lo = c_idx * MXU_N
        is_q = lo < MIX_WIDTH
        t = _dot(h_ref[...], w_ref[:, lo:lo + MXU_N])
        if is_q:
            t = t * ATTN_SCALE
        dst, ext_ref = (q_ref, qext_ref) if is_q else (k_ref, kext_ref)
        for j in range(MXU_N // PAIR):
            p = (lo % MIX_WIDTH) // PAIR + j
            pair = t[:, j * PAIR:(j + 1) * PAIR]
            ext = ext_ref[:, p * PAIR:(p + 1) * PAIR]
            dst[:, 2 * p * LANES:(2 * p + 1) * LANES] = jnp.where(first, pair, ext).astype(BF16)
            dst[:, (2 * p + 1) * LANES:(2 * p + 2) * LANES] = pltpu.roll(
                jnp.where(first, ext, pair), HEAD_DIM, axis=1).astype(BF16)
    for c_idx in range(MIX_WIDTH // MXU_N):
        lo = c_idx * MXU_N
        vt_ref[0, lo:lo + MXU_N, :] = _dot_nt(wvt_ref[lo:lo + MXU_N, :], h_ref[...]).astype(BF16)
    qm_ref[...] = (_dot(h_ref[...], w_ref[:, qkv_w + LANES:]) * ATTN_SCALE).astype(BF16)


def _proj_c(x, seq, g_pre, w_in, forget_bias):
    m, d = x.shape
    tm = TK_FOX
    qkv_w = 3 * MIX_WIDTH
    pad = jnp.zeros((d, LANES - N_MIX_HEADS), w_in.dtype)
    w = jnp.concatenate([w_in[:, :qkv_w + N_MIX_HEADS], pad, w_in[:, qkv_w + N_MIX_HEADS:]], axis=1)
    w_v_t = w_in[:, 2 * MIX_WIDTH:qkv_w].T
    fb = jnp.concatenate([forget_bias.astype(F32), jnp.zeros((LANES - N_MIX_HEADS,), F32)])
    consts = _bias_selectors()
    row = lambda w_: pl.BlockSpec((tm, w_), lambda i: (i, 0))
    return pl.pallas_call(
        functools.partial(_proj_c_body, tiles_per_seq=seq // tm),
        grid=(m // tm,),
        in_specs=[row(d), _const_spec((1, d)), _const_spec(w.shape), _const_spec(w_v_t.shape),
                  _const_spec((1, LANES))] + [_const_spec(c.shape) for c in consts],
        out_specs=[row(AUG_WIDTH), row(AUG_WIDTH),
                   pl.BlockSpec((1, MIX_WIDTH, tm), lambda i: (i, 0, 0)), row(MEM_WIDTH)],
        out_shape=[jax.ShapeDtypeStruct((m, AUG_WIDTH), BF16),
                   jax.ShapeDtypeStruct((m, AUG_WIDTH), BF16),
                   jax.ShapeDtypeStruct((m // tm, MIX_WIDTH, tm), BF16),
                   jax.ShapeDtypeStruct((m, MEM_WIDTH), BF16)],
        scratch_shapes=[pltpu.VMEM((tm, d), BF16), pltpu.VMEM((1, LANES), F32),
                        pltpu.VMEM((tm, MIX_WIDTH), F32), pltpu.VMEM((tm, MIX_WIDTH), F32)],
        compiler_params=_params("arbitrary"),
        name="proj_c",
    )(x, g_pre, w, w_v_t, fb.reshape(1, LANES), *consts)


def _fox_body(q_ref, k_ref, vt_ref, o_ref, acc_ref, sa_ref, sb_ref, *, tq):
    i = pl.program_id(2)
    causal = (lax.broadcasted_iota(jnp.int32, (tq, tq), 1)
              >= lax.broadcasted_iota(jnp.int32, (tq, tq), 0))
    acc_ref[...] = jnp.zeros_like(acc_ref)

    def scores(j, s_ref):
        rows = pl.ds(pl.multiple_of(j * tq, tq), tq)
        for h in range(FOX_HEADS):
            lanes = slice(h * LANES, (h + 1) * LANES)
            s_ref[h] = _dot_nt(k_ref[0, rows, lanes], q_ref[0, :, lanes])

    def consume(j, s_ref, carry, diagonal):
        new = []
        for h in range(FOX_HEADS):
            m_run, l_run = carry[h]
            st = s_ref[h]
            if diagonal:
                st = jnp.where(causal, st, NEG_INF)
            m_new = jnp.maximum(m_run, jnp.max(st, axis=0, keepdims=True))
            alpha = jnp.exp(m_run - m_new)
            e = jnp.exp(st - m_new)
            l_new = alpha * l_run + jnp.sum(e, axis=0, keepdims=True)
            vt = vt_ref[j, (h // 2) * PAIR:(h // 2 + 1) * PAIR, :]
            acc_ref[h] = alpha * acc_ref[h] + _dot(vt, e.astype(BF16))
            new.append((m_new, l_new))
        return tuple(new)

    def two_tiles(t, carry):
        j = 2 * t
        scores(j + 1, sb_ref)
        carry = consume(j, sa_ref, carry, False)
        scores(j + 2, sa_ref)
        return consume(j + 1, sb_ref, carry, False)

    def last_two(carry):
        scores(i, sb_ref)
        carry = consume(i - 1, sa_ref, carry, False)
        return consume(i, sb_ref, carry, True)

    def last_one(carry):
        return consume(i, sa_ref, carry, True)

    init = tuple((jnp.full((1, tq), NEG_INF, F32), jnp.zeros((1, tq), F32))
                 for _ in range(FOX_HEADS))
    scores(0, sa_ref)
    carry = lax.fori_loop(0, i // 2, two_tiles, init)
    carry = lax.cond(i % 2 == 1, last_two, last_one, carry)
    even_rows = lax.broadcasted_iota(jnp.int32, (PAIR, tq), 0) < HEAD_DIM
    for p in range(FOX_HEADS // 2):
        denom = jnp.where(even_rows, carry[2 * p][1], carry[2 * p + 1][1])
        out_t = jnp.where(even_rows, acc_ref[2 * p], acc_ref[2 * p + 1]) / denom
        o_ref[0, :, p * PAIR:(p + 1) * PAIR] = out_t.T.astype(BF16)


def _fox_attention(q_aug, k_aug, v_t, batch, seq):
    tq = tk = TK_FOX
    q3 = q_aug.reshape(batch, seq, AUG_WIDTH)
    k3 = k_aug.reshape(batch, seq, AUG_WIDTH)
    aug_w = FOX_HEADS * LANES
    v_w = FOX_HEADS * HEAD_DIM
    score_buf = pltpu.VMEM((FOX_HEADS, tk, tq), F32)
    out = pl.pallas_call(
        functools.partial(_fox_body, tq=tq),
        grid=(batch, N_MIX_HEADS // FOX_HEADS, seq // tq),
        in_specs=[pl.BlockSpec((1, tq, aug_w), lambda b, p, i: (b, i, p)),
                  pl.BlockSpec((1, seq, aug_w), lambda b, p, i: (b, 0, p)),
                  pl.BlockSpec((seq // tk, v_w, tk), lambda b, p, i: (b, p, 0))],
        out_specs=pl.BlockSpec((1, tq, v_w), lambda b, p, i: (b, i, p)),
        out_shape=jax.ShapeDtypeStruct((batch, seq, MIX_WIDTH), BF16),
        scratch_shapes=[pltpu.VMEM((FOX_HEADS, PAIR, tq), F32), score_buf, score_buf],
        compiler_params=_params("parallel", "parallel", "arbitrary"),
        name="fox_attention",
    )(q3, k3, v_t)
    return out.reshape(batch * seq, MIX_WIDTH)


def _out_c_body(x_ref, att_ref, qm_ref, kv_ref, w_ref, g_ref, o_ref):
    mem_out = _mem_attention(qm_ref[...], kv_ref).astype(BF16)
    y = _dot(att_ref[...], w_ref[:MIX_WIDTH, :]) + _dot(mem_out, w_ref[MIX_WIDTH:, :])
    _finish_mixer(x_ref, y, g_ref, o_ref)


def _out_c(x, seq, att, qm, kv, w_out, g_post):
    m, d = x.shape
    tm = TM_PROJ
    tiles_per_seq = seq // tm
    n_mem = kv.shape[1]
    row = lambda w: pl.BlockSpec((tm, w), lambda i: (i, 0))
    return pl.pallas_call(
        _out_c_body,
        grid=(m // tm,),
        in_specs=[row(d), row(MIX_WIDTH), row(MEM_WIDTH),
                  pl.BlockSpec((1, n_mem, 2 * MEM_WIDTH), lambda i: (i // tiles_per_seq, 0, 0)),
                  _const_spec(w_out.shape), _const_spec((1, d))],
        out_specs=row(d),
        out_shape=jax.ShapeDtypeStruct((m, d), F32),
        compiler_params=_params("parallel"),
        name="out_c",
    )(x, att, qm, kv, w_out, g_post)


def kernel(x, mem, positions, norm_g, mem_norm_g, w_mem_kv, ffn_w_gate_up, ffn_w_down,
           a_w_in, a_spatial_w, a_spatial_b, a_v_norm_g, a_w_out,
           b_w_in, b_w_out, c_w_in, c_forget_bias, c_w_out):
    batch, seq, d = x.shape
    depth = norm_g.shape[0]
    bf = lambda w: w.astype(BF16)

    kv_all = _mem_kv(mem, mem_norm_g, bf(w_mem_kv))
    w_gate_up, w_down = bf(ffn_w_gate_up), bf(ffn_w_down)
    cos = sin = None
    xf = x.reshape(batch * seq, d)
    for i in range(depth):
        kind, j = i % 3, i // 3
        g = norm_g[i].reshape(norm_g.shape[1], 1, d)
        xf = _ffn(xf, g[0], g[1], w_gate_up, w_down, (i, 0))
        kv = kv_all[i]
        if kind == 0:
            xf = _mixer_a(xf, seq, g[2], g[3], kv, bf(a_w_in[j]), bf(a_spatial_w[j]),
                          a_spatial_b[j], a_v_norm_g[j], bf(a_w_out[j]))
        elif kind == 1:
            if cos is None:
                cos, sin = _rope_tables(positions)
            *groups, qm = _proj_b(xf, batch, seq, g[2], bf(b_w_in[j]), cos, sin)
            outs, lses = zip(*[_dilated_group(qkv_g, w, dil)
                               for qkv_g, (w, dil) in zip(groups, DILATED_GROUPS)])
            xf = _out_b(xf, seq, outs, lses, qm, kv, bf(b_w_out[j]), g[3])
        else:
            q_aug, k_aug, v_t, qm = _proj_c(xf, seq, g[2], bf(c_w_in[j]), c_forget_bias[j])
            att = _fox_attention(q_aug, k_aug, v_t, batch, seq)
            xf = _out_c(xf, seq, att, qm, kv, bf(c_w_out[j]), g[3])
        xf = _ffn(xf, g[4], g[5], w_gate_up, w_down, (i, 1))
    return xf.reshape(batch, seq, d)
```

```python
import functools

import jax
import jax.numpy as jnp
from jax import lax
from jax.experimental import pallas as pl
from jax.experimental.pallas import tpu as pltpu

F32 = jnp.float32
BF16 = jnp.bfloat16

HEAD_DIM = 64
N_MIX_HEADS = 12
MIX_WIDTH = N_MIX_HEADS * HEAD_DIM
N_MEM_HEADS = 4
MEM_WIDTH = N_MEM_HEADS * HEAD_DIM
CHUNK = 128
ROPE_THETA = 10000.0
DILATED_GROUPS = ((128, 1), (512, 4), (2048, 16))
DIL_OUT_WIDTH = (N_MIX_HEADS // len(DILATED_GROUPS)) * HEAD_DIM
RMS_EPS = 1e-6
LN_EPS = 1e-5
NEG_INF = -1e30
ATTN_SCALE = HEAD_DIM ** -0.5
LOG2_E = 1.4426950408889634

LANES = 128
PAIR = 2 * HEAD_DIM
MXU_N = 256
VMEM_LIMIT_BYTES = 56 * 1024 * 1024

TM_FFN = 1024
FFN_SUB = 512
TM_PROJ = 512
TM_MIX_A = 512
TQ_FOX = 512
TK_FOX = 512
FOX_HEADS = 4
DIL_CHAINS = 4
FF_CHUNK = 256


def _params(*semantics):
    return pltpu.CompilerParams(dimension_semantics=semantics,
                                vmem_limit_bytes=VMEM_LIMIT_BYTES)


def _const_spec(shape):
    zeros = (0,) * len(shape)
    return pl.BlockSpec(shape, lambda *_: zeros, pipeline_mode=pl.Buffered(1))


def _rms_norm(x, g):
    ms = jnp.mean(x * x, axis=-1, keepdims=True)
    return x * lax.rsqrt(ms + RMS_EPS) * g


def _dot(a, b):
    return jnp.dot(a, b, preferred_element_type=F32)


def _dot_nt(a, b):
    return lax.dot_general(a, b, (((1,), (1,)), ((), ())), preferred_element_type=F32)


def _first_head_lanes(shape):
    return lax.broadcasted_iota(jnp.int32, shape, len(shape) - 1) % PAIR < HEAD_DIM


def _ffn_body(x_ref, gpre_ref, gpost_ref, wgu_ref, wd_ref, o_ref, xn_ref, act_ref, *, d_ff):
    for sub in range(x_ref.shape[0] // FFN_SUB):
        rows = slice(sub * FFN_SUB, (sub + 1) * FFN_SUB)
        xn_ref[rows, :] = _rms_norm(x_ref[rows, :], gpre_ref[...]).astype(BF16)
        for c in range(d_ff // FF_CHUNK):
            lo = c * FF_CHUNK
            gate = _dot(xn_ref[rows, :], wgu_ref[:, lo:lo + FF_CHUNK])
            up = _dot(xn_ref[rows, :], wgu_ref[:, d_ff + lo:d_ff + lo + FF_CHUNK])
            act_ref[rows, lo:lo + FF_CHUNK] = (gate * jax.nn.sigmoid(gate) * up).astype(BF16)
        y = _dot(act_ref[rows, :], wd_ref[...])
        o_ref[rows, :] = x_ref[rows, :] + 0.5 * _rms_norm(y, gpost_ref[...])


def _stacked_spec(stacked, index):
    k = len(index)
    tail = stacked.shape[k:]
    where = tuple(index) + (0,) * len(tail)
    return pl.BlockSpec((None,) * k + tail, lambda *_: where, pipeline_mode=pl.Buffered(1))


def _ffn(x, g_pre, g_post, w_gate_up, w_down, index):
    m, d = x.shape
    d_ff = w_down.shape[-2]
    assert m % TM_FFN == 0 and d_ff % FF_CHUNK == 0
    row = pl.BlockSpec((TM_FFN, d), lambda i: (i, 0))
    return pl.pallas_call(
        functools.partial(_ffn_body, d_ff=d_ff),
        grid=(m // TM_FFN,),
        in_specs=[row, _const_spec((1, d)), _const_spec((1, d)),
                  _stacked_spec(w_gate_up, index), _stacked_spec(w_down, index)],
        out_specs=row,
        out_shape=jax.ShapeDtypeStruct((m, d), F32),
        scratch_shapes=[pltpu.VMEM((TM_FFN, d), BF16), pltpu.VMEM((TM_FFN, d_ff), BF16)],
        compiler_params=_params("parallel"),
        name="ffn",
    )(x, g_pre, g_post, w_gate_up, w_down)


def _mem_kv_body(mem_ref, g_ref, w_ref, o_ref):
    o_ref[0, 0] = _dot(_rms_norm(mem_ref[0], g_ref[0]).astype(BF16), w_ref[0]).astype(BF16)


def _mem_kv(mem, mem_norm_g, w_mem_kv):
    b, n_mem, d = mem.shape
    depth = w_mem_kv.shape[0]
    return pl.pallas_call(
        _mem_kv_body,
        grid=(depth, b),
        in_specs=[pl.BlockSpec((1, n_mem, d), lambda l, i: (i, 0, 0)),
                  pl.BlockSpec((1, 1, d), lambda l, i: (l, 0, 0)),
                  pl.BlockSpec((1, d, 2 * MEM_WIDTH), lambda l, i: (l, 0, 0))],
        out_specs=pl.BlockSpec((1, 1, n_mem, 2 * MEM_WIDTH), lambda l, i: (l, i, 0, 0)),
        out_shape=jax.ShapeDtypeStruct((depth, b, n_mem, 2 * MEM_WIDTH), BF16),
        compiler_params=_params("parallel", "parallel"),
        name="mem_kv",
    )(mem, mem_norm_g.reshape(depth, 1, d), w_mem_kv)


def _stack_heads(qp, first):
    zero = jnp.zeros_like(qp)
    return jnp.concatenate([jnp.where(first, qp, zero), jnp.where(first, zero, qp)], axis=0)


def _softmax_pv(s, v, with_lse=False):
    mx = jnp.max(s, axis=-1, keepdims=True)
    e = jnp.exp(s - mx)
    denom = jnp.sum(e, axis=-1, keepdims=True)
    out = _dot(e.astype(BF16), v) / denom
    if with_lse:
        return out, mx + jnp.log(denom)
    return out


def _mem_attention(qm, kv_ref):
    tm = qm.shape[0]
    outs = []
    for p in range(MEM_WIDTH // PAIR):
        qp = qm[:, p * PAIR:(p + 1) * PAIR]
        k = kv_ref[0, :, p * PAIR:(p + 1) * PAIR]
        v = kv_ref[0, :, MEM_WIDTH + p * PAIR:MEM_WIDTH + (p + 1) * PAIR]
        first = _first_head_lanes(qp.shape)
        acc = _softmax_pv(_dot_nt(_stack_heads(qp, first), k), v)
        outs.append(jnp.where(first, acc[:tm], acc[tm:]))
    return jnp.concatenate(outs, axis=-1)


def _finish_mixer(x_ref, y, g_ref, o_ref):
    o_ref[...] = x_ref[...] + _rms_norm(y, g_ref[...])


def _mixer_a_body(x_ref, gpre_ref, win_ref, vg_ref, sw_ref, sb_ref, kv_ref, wout_ref, gpost_ref,
                  o_ref, u_ref, v_ref, gated_ref, wc_ref):
    tm = x_ref.shape[0]
    h = _rms_norm(x_ref[...], gpre_ref[...]).astype(BF16)
    u_ref[...] = jax.nn.gelu(_dot(h, win_ref[:, :MIX_WIDTH]))
    v = jax.nn.gelu(_dot(h, win_ref[:, MIX_WIDTH:2 * MIX_WIDTH]))
    mu = jnp.mean(v, axis=-1, keepdims=True)
    vc = v - mu
    var = jnp.mean(vc * vc, axis=-1, keepdims=True)
    v_ref[...] = (vc * lax.rsqrt(var + LN_EPS) * vg_ref[...]).astype(BF16)
    qm = (_dot(h, win_ref[:, 2 * MIX_WIDTH:]) * ATTN_SCALE).astype(BF16)

    t_idx = lax.broadcasted_iota(jnp.int32, (CHUNK, CHUNK), 0)
    s_idx = lax.broadcasted_iota(jnp.int32, (CHUNK, CHUNK), 1)
    for g in range(N_MIX_HEADS):
        wc_ref[g] = jnp.where(s_idx <= t_idx, sw_ref[g], jnp.zeros((CHUNK, CHUNK), BF16))

    first = _first_head_lanes((CHUNK, PAIR))
    for c in range(tm // CHUNK):
        rows = slice(c * CHUNK, (c + 1) * CHUNK)
        for p in range(MIX_WIDTH // PAIR):
            lanes = slice(p * PAIR, (p + 1) * PAIR)
            vp = v_ref[rows, lanes]
            mixed = jnp.where(first, _dot(wc_ref[2 * p], vp), _dot(wc_ref[2 * p + 1], vp))
            gated_ref[rows, lanes] = (u_ref[rows, lanes] * (mixed + sb_ref[:, lanes])).astype(BF16)

    mem_out = _mem_attention(qm, kv_ref).astype(BF16)
    y = _dot(gated_ref[...], wout_ref[:MIX_WIDTH, :]) + _dot(mem_out, wout_ref[MIX_WIDTH:, :])
    _finish_mixer(x_ref, y, gpost_ref, o_ref)


def _mixer_a(x, seq, g_pre, g_post, kv, w_in, spatial_w, spatial_b, v_norm_g, w_out):
    m, d = x.shape
    tm = TM_MIX_A
    assert seq % tm == 0 and tm % CHUNK == 0
    tiles_per_seq = seq // tm
    n_mem = kv.shape[1]
    row = pl.BlockSpec((tm, d), lambda i: (i, 0))
    bias = jnp.repeat(spatial_b.T, HEAD_DIM, axis=1)
    return pl.pallas_call(
        _mixer_a_body,
        grid=(m // tm,),
        in_specs=[row, _const_spec((1, d)), _const_spec(w_in.shape), _const_spec((1, MIX_WIDTH)),
                  _const_spec(spatial_w.shape), _const_spec(bias.shape),
                  pl.BlockSpec((1, n_mem, 2 * MEM_WIDTH), lambda i: (i // tiles_per_seq, 0, 0)),
                  _const_spec(w_out.shape), _const_spec((1, d))],
        out_specs=row,
        out_shape=jax.ShapeDtypeStruct((m, d), F32),
        scratch_shapes=[pltpu.VMEM((tm, MIX_WIDTH), F32), pltpu.VMEM((tm, MIX_WIDTH), BF16),
                        pltpu.VMEM((tm, MIX_WIDTH), BF16),
                        pltpu.VMEM((N_MIX_HEADS, CHUNK, CHUNK), BF16)],
        compiler_params=_params("parallel"),
        name="mixer_a",
    )(x, g_pre, w_in, v_norm_g.reshape(1, MIX_WIDTH), spatial_w, bias, kv, w_out, g_post)


def _rope_body(pos_ref, freq_ref, sign_ref, cos_ref, sin_ref):
    ang = pos_ref[...].astype(F32) * freq_ref[...]
    cos_ref[...] = jnp.cos(ang)
    sin_ref[...] = jnp.sin(ang) * sign_ref[...]


def _rope_tables(positions):
    m = positions.size
    tm = TM_PROJ
    inv_freq = ROPE_THETA ** (-jnp.arange(0, HEAD_DIM, 2, dtype=F32) / HEAD_DIM)
    freq = jnp.tile(inv_freq, LANES // (HEAD_DIM // 2)).reshape(1, LANES)
    half = jnp.concatenate([-jnp.ones((HEAD_DIM // 2,), F32), jnp.ones((HEAD_DIM // 2,), F32)])
    sign = jnp.tile(half, LANES // HEAD_DIM).reshape(1, LANES)
    table = pl.BlockSpec((tm, LANES), lambda i: (i, 0))
    return pl.pallas_call(
        _rope_body,
        grid=(m // tm,),
        in_specs=[pl.BlockSpec((tm, 1), lambda i: (i, 0)), _const_spec((1, LANES)),
                  _const_spec((1, LANES))],
        out_specs=[table, table],
        out_shape=[jax.ShapeDtypeStruct((m, LANES), F32)] * 2,
        compiler_params=_params("parallel"),
        name="rope_tables",
    )(positions.reshape(m, 1), freq, sign)


def _rope(t, cos, sin_signed):
    first_half = lax.broadcasted_iota(jnp.int32, t.shape, 1) % HEAD_DIM < HEAD_DIM // 2
    swapped = jnp.where(first_half,
                        pltpu.roll(t, LANES - HEAD_DIM // 2, axis=1),
                        pltpu.roll(t, HEAD_DIM // 2, axis=1))
    return t * cos + swapped * sin_signed


def _residue_spec(tm, dilation, width, tiles_per_seq):
    return pl.BlockSpec((1, dilation, tm // dilation, width),
                        lambda i: (i // tiles_per_seq, 0, i % tiles_per_seq, 0))


def _proj_b_body(x_ref, g_ref, w_ref, cos_ref, sin_ref, g0_ref, g1_ref, g2_ref, qm_ref,
                 h_ref, t_ref):
    tm = x_ref.shape[0]
    h_ref[...] = _rms_norm(x_ref[...], g_ref[...]).astype(BF16)
    cos = cos_ref[...]
    sin = sin_ref[...]
    outs = (g0_ref, g1_ref, g2_ref)
    n_groups = len(DILATED_GROUPS)
    for c in range(3 * n_groups):
        kind, grp = divmod(c, n_groups)
        lo = c * DIL_OUT_WIDTH
        t = _dot(h_ref[...], w_ref[:, lo:lo + DIL_OUT_WIDTH])
        if kind < 2:
            scale = ATTN_SCALE if kind == 0 else 1.0
            t = jnp.concatenate(
                [_rope(t[:, j * LANES:(j + 1) * LANES], cos, sin)
                 for j in range(DIL_OUT_WIDTH // LANES)], axis=-1) * scale
        dil = DILATED_GROUPS[grp][1]
        cols = slice(kind * DIL_OUT_WIDTH, (kind + 1) * DIL_OUT_WIDTH)
        if dil == 1:
            outs[grp][0, 0, :, cols] = t.astype(BF16)
        else:
            halves = DIL_OUT_WIDTH // LANES
            for j in range(halves):
                t_ref[j] = t[:, j * LANES:(j + 1) * LANES]
            for r in range(dil):
                outs[grp][0, r, :, cols] = jnp.concatenate(
                    [t_ref[j, pl.ds(r, tm // dil, stride=dil), :] for j in range(halves)],
                    axis=-1).astype(BF16)
    qm_ref[...] = (_dot(h_ref[...], w_ref[:, 3 * MIX_WIDTH:]) * ATTN_SCALE).astype(BF16)


def _proj_b(x, batch, seq, g_pre, w_in, cos, sin):
    m, d = x.shape
    tm = TM_PROJ
    tiles_per_seq = seq // tm
    row = lambda w: pl.BlockSpec((tm, w), lambda i: (i, 0))
    width = 3 * DIL_OUT_WIDTH
    group_specs = [_residue_spec(tm, dil, width, tiles_per_seq) for _, dil in DILATED_GROUPS]
    group_shapes = [jax.ShapeDtypeStruct((batch, dil, seq // dil, width), BF16)
                    for _, dil in DILATED_GROUPS]
    return pl.pallas_call(
        _proj_b_body,
        grid=(m // tm,),
        in_specs=[row(d), _const_spec((1, d)), _const_spec(w_in.shape), row(LANES), row(LANES)],
        out_specs=group_specs + [row(MEM_WIDTH)],
        out_shape=group_shapes + [jax.ShapeDtypeStruct((m, MEM_WIDTH), BF16)],
        scratch_shapes=[pltpu.VMEM((tm, d), BF16),
                        pltpu.VMEM((DIL_OUT_WIDTH // LANES, tm, LANES), F32)],
        compiler_params=_params("parallel"),
        name="proj_b",
    )(x, g_pre, w_in, cos, sin)


def _dilated_body(qkv_ref, o_ref, lse_ref, *, n_blocks, span, unroll):
    w = DIL_OUT_WIDTH
    qi = lax.broadcasted_iota(jnp.int32, (2 * span, 2 * span), 0) % span
    ki = lax.broadcasted_iota(jnp.int32, (2 * span, 2 * span), 1)
    band = ((ki < span) & (ki >= qi)) | ((ki >= span) & (ki - span <= qi))
    own_only = (lax.broadcasted_iota(jnp.int32, (2 * span, span), 1)
                <= lax.broadcasted_iota(jnp.int32, (2 * span, span), 0) % span)
    first = _first_head_lanes((span, PAIR))
    own_lanes = jnp.concatenate([first, ~first], axis=0)

    def block(r0, has_prev):
        own = pl.ds(r0, span)
        keys = pl.ds(r0 - span, 2 * span) if has_prev else own
        for r in range(qkv_ref.shape[1]):
            for p in range(w // PAIR):
                ql = slice(p * PAIR, (p + 1) * PAIR)
                q2 = _stack_heads(qkv_ref[0, r, own, ql], first)
                k2 = qkv_ref[0, r, keys, w + p * PAIR:w + (p + 1) * PAIR]
                v2 = qkv_ref[0, r, keys, 2 * w + p * PAIR:2 * w + (p + 1) * PAIR]
                s = jnp.where(band if has_prev else own_only, _dot_nt(q2, k2), NEG_INF)
                acc, lse = _softmax_pv(s, v2, with_lse=True)
                o_ref[0, r, own, ql] = jnp.where(first, acc[:span], acc[span:]).astype(BF16)
                lse = jnp.where(own_lanes, lse, 0.0)
                lse_ref[0, r, own, ql] = lse[:span] + lse[span:]

    block(0, False)

    def step(blk, carry):
        block(pl.multiple_of(blk * span, span), True)
        return carry

    lax.fori_loop(1, n_blocks, step, 0, unroll=unroll)


def _dilated_group(qkv, window, dilation):
    b, _, length, width = qkv.shape
    span = window // dilation
    assert length % span == 0
    per_step = min(dilation, DIL_CHAINS)
    unroll = DIL_CHAINS // per_step
    assert dilation % per_step == 0
    out_spec = pl.BlockSpec((1, per_step, length, DIL_OUT_WIDTH), lambda i, r: (i, r, 0, 0))
    return pl.pallas_call(
        functools.partial(_dilated_body, n_blocks=length // span, span=span, unroll=unroll),
        grid=(b, dilation // per_step),
        in_specs=[pl.BlockSpec((1, per_step, length, width), lambda i, r: (i, r, 0, 0))],
        out_specs=[out_spec, out_spec],
        out_shape=[jax.ShapeDtypeStruct((b, dilation, length, DIL_OUT_WIDTH), BF16),
                   jax.ShapeDtypeStruct((b, dilation, length, DIL_OUT_WIDTH), F32)],
        compiler_params=_params("parallel", "parallel"),
        name=f"dilated_{dilation}",
    )(qkv)


def _out_b_body(x_ref, o0_ref, o1_ref, o2_ref, l0_ref, l1_ref, l2_ref, qm_ref, kv_ref, w_ref,
                g_ref, o_ref, *scratch):
    tm = x_ref.shape[0]

    def token_major(ref, buf):
        dil = ref.shape[1]
        if dil == 1:
            return ref[0, 0].astype(F32)
        halves = DIL_OUT_WIDTH // LANES
        for r in range(dil):
            rows = ref[0, r].astype(F32)
            for j in range(halves):
                buf[j, pl.ds(r, tm // dil, stride=dil), :] = rows[:, j * LANES:(j + 1) * LANES]
        return jnp.concatenate([buf[j] for j in range(halves)], axis=-1)

    outs = [token_major(r, scratch[2 * g]) for g, r in enumerate((o0_ref, o1_ref, o2_ref))]
    lses = [token_major(r, scratch[2 * g + 1]) for g, r in enumerate((l0_ref, l1_ref, l2_ref))]
    mx = jnp.maximum(jnp.maximum(lses[0], lses[1]), lses[2])
    es = [jnp.exp(l - mx) for l in lses]
    denom = es[0] + es[1] + es[2]
    merged = sum((e / denom) * o for e, o in zip(es, outs))
    mem_out = _mem_attention(qm_ref[...], kv_ref).astype(BF16)
    y = (_dot(merged.astype(BF16), w_ref[:DIL_OUT_WIDTH, :])
         + _dot(mem_out, w_ref[DIL_OUT_WIDTH:, :]))
    _finish_mixer(x_ref, y, g_ref, o_ref)


def _out_b(x, seq, outs, lses, qm, kv, w_out, g_post):
    m, d = x.shape
    tm = TM_PROJ
    tiles_per_seq = seq // tm
    n_mem = kv.shape[1]
    row = lambda w: pl.BlockSpec((tm, w), lambda i: (i, 0))
    group_specs = [_residue_spec(tm, dil, DIL_OUT_WIDTH, tiles_per_seq)
                   for _, dil in DILATED_GROUPS]
    return pl.pallas_call(
        _out_b_body,
        grid=(m // tm,),
        in_specs=[row(d)] + group_specs * 2 + [
            row(MEM_WIDTH),
            pl.BlockSpec((1, n_mem, 2 * MEM_WIDTH), lambda i: (i // tiles_per_seq, 0, 0)),
            _const_spec(w_out.shape), _const_spec((1, d))],
        out_specs=row(d),
        out_shape=jax.ShapeDtypeStruct((m, d), F32),
        scratch_shapes=[pltpu.VMEM((DIL_OUT_WIDTH // LANES, tm, LANES), F32)]
        * (2 * len(DILATED_GROUPS)),
        compiler_params=_params("parallel"),
        name="out_b",
    )(x, *outs, *lses, qm, kv, w_out, g_post)


def _split3(x):
    hi = x.astype(BF16)
    r1 = x - hi.astype(F32)
    mid = r1.astype(BF16)
    lo = (r1 - mid.astype(F32)).astype(BF16)
    return hi, mid, lo


N_BIAS_TERMS = 3
AUG_WIDTH = N_MIX_HEADS * LANES
V_ROWS = HEAD_DIM + 16


def _pack_terms(hi, mid, lo):
    return (hi.astype(F32) + pltpu.roll(mid.astype(F32), N_MIX_HEADS, axis=1)
            + pltpu.roll(lo.astype(F32), 2 * N_MIX_HEADS, axis=1)).astype(BF16)


def _bias_selectors():
    rows = jnp.arange(LANES)
    term, head = rows // N_MIX_HEADS, rows % N_MIX_HEADS
    valid = rows < N_BIAS_TERMS * N_MIX_HEADS
    cols = jnp.arange(MIX_WIDTH)
    base = (head // 2) * PAIR + jnp.where(head % 2 == 0, HEAD_DIM, 0)

    def select(offset):
        target = base + offset + term
        return ((cols[None, :] == target[:, None]) & valid[:, None]).astype(BF16)

    def ones(offset):
        lane = cols % HEAD_DIM
        return ((lane >= offset) & (lane < offset + N_BIAS_TERMS)).astype(F32)[None, :]

    q_sel, k_sel = select(0), select(N_BIAS_TERMS)
    q_one, k_one = ones(N_BIAS_TERMS), ones(0)
    return q_sel, k_sel, q_one, k_one


def _proj_c_body(x_ref, g_ref, w_ref, wvt_ref, fb_ref, qsel_ref, ksel_ref, qone_ref, kone_ref,
                 q_ref, k_ref, vt_ref, qm_ref, h_ref, carry_ref, qext_ref, kext_ref,
                 *, tiles_per_seq):
    tm = x_ref.shape[0]
    qkv_w = 3 * MIX_WIDTH

    @pl.when(pl.program_id(0) % tiles_per_seq == 0)
    def _():
        carry_ref[...] = jnp.zeros_like(carry_ref)

    h_ref[...] = _rms_norm(x_ref[...], g_ref[...]).astype(BF16)

    z = _dot(h_ref[...], w_ref[:, qkv_w:qkv_w + LANES]) + fb_ref[...]
    log_f = jnp.minimum(z, 0.0) - jnp.log1p(jnp.exp(-jnp.abs(z)))
    lane = lax.broadcasted_iota(jnp.int32, (tm, LANES), 1)
    log_f = jnp.where(lane < N_MIX_HEADS, log_f, 0.0)
    tri = (lax.broadcasted_iota(jnp.int32, (tm, tm), 1)
           <= lax.broadcasted_iota(jnp.int32, (tm, tm), 0)).astype(BF16)
    sums = _dot(tri, _pack_terms(*_split3(log_f)))
    local = (sums + pltpu.roll(sums, LANES - N_MIX_HEADS, axis=1)
             + pltpu.roll(sums, LANES - 2 * N_MIX_HEADS, axis=1))
    c = carry_ref[...] + jnp.where(lane < N_MIX_HEADS, local, 0.0)
    carry_ref[...] = c[tm - 1:tm, :]
    terms = _pack_terms(*_split3(c * LOG2_E))
    qext_ref[...] = _dot(terms, qsel_ref[...]) + qone_ref[...]
    kext_ref[...] = kone_ref[...] - _dot(terms, ksel_ref[...])

    first = _first_head_lanes((tm, PAIR))
    for c_idx in range(2 * MIX_WIDTH // MXU_N):
        lo = c_idx * MXU_N
        is_q = lo < MIX_WIDTH
        t = _dot(h_ref[...], w_ref[:, lo:lo + MXU_N])
        if is_q:
            t = t * (ATTN_SCALE * LOG2_E)
        dst, ext_ref = (q_ref, qext_ref) if is_q else (k_ref, kext_ref)
        for j in range(MXU_N // PAIR):
            p = (lo % MIX_WIDTH) // PAIR + j
            pair = t[:, j * PAIR:(j + 1) * PAIR]
            ext = ext_ref[:, p * PAIR:(p + 1) * PAIR]
            dst[:, 2 * p * LANES:(2 * p + 1) * LANES] = jnp.where(first, pair, ext).astype(BF16)
            dst[:, (2 * p + 1) * LANES:(2 * p + 2) * LANES] = pltpu.roll(
                jnp.where(first, ext, pair), HEAD_DIM, axis=1).astype(BF16)
    ones_row = (lax.broadcasted_iota(jnp.int32, (V_ROWS - HEAD_DIM, tm), 0) == 0).astype(BF16)
    for c_idx in range(MIX_WIDTH // MXU_N):
        lo = c_idx * MXU_N
        vt = _dot_nt(wvt_ref[lo:lo + MXU_N, :], h_ref[...]).astype(BF16)
        for j in range(MXU_N // HEAD_DIM):
            head = lo // HEAD_DIM + j
            vt_ref[0, head * V_ROWS:head * V_ROWS + HEAD_DIM, :] = vt[j * HEAD_DIM:(j + 1) * HEAD_DIM]
            vt_ref[0, head * V_ROWS + HEAD_DIM:(head + 1) * V_ROWS, :] = ones_row
    qm_ref[...] = (_dot(h_ref[...], w_ref[:, qkv_w + LANES:]) * ATTN_SCALE).astype(BF16)


def _proj_c(x, seq, g_pre, w_in, forget_bias):
    m, d = x.shape
    tm = TK_FOX
    qkv_w = 3 * MIX_WIDTH
    pad = jnp.zeros((d, LANES - N_MIX_HEADS), w_in.dtype)
    w = jnp.concatenate([w_in[:, :qkv_w + N_MIX_HEADS], pad, w_in[:, qkv_w + N_MIX_HEADS:]], axis=1)
    w_v_t = w_in[:, 2 * MIX_WIDTH:qkv_w].T
    fb = jnp.concatenate([forget_bias.astype(F32), jnp.zeros((LANES - N_MIX_HEADS,), F32)])
    consts = _bias_selectors()
    row = lambda w_: pl.BlockSpec((tm, w_), lambda i: (i, 0))
    return pl.pallas_call(
        functools.partial(_proj_c_body, tiles_per_seq=seq // tm),
        grid=(m // tm,),
        in_specs=[row(d), _const_spec((1, d)), _const_spec(w.shape), _const_spec(w_v_t.shape),
                  _const_spec((1, LANES))] + [_const_spec(c.shape) for c in consts],
        out_specs=[row(AUG_WIDTH), row(AUG_WIDTH),
                   pl.BlockSpec((1, N_MIX_HEADS * V_ROWS, tm), lambda i: (i, 0, 0)),
                   row(MEM_WIDTH)],
        out_shape=[jax.ShapeDtypeStruct((m, AUG_WIDTH), BF16),
                   jax.ShapeDtypeStruct((m, AUG_WIDTH), BF16),
                   jax.ShapeDtypeStruct((m // tm, N_MIX_HEADS * V_ROWS, tm), BF16),
                   jax.ShapeDtypeStruct((m, MEM_WIDTH), BF16)],
        scratch_shapes=[pltpu.VMEM((tm, d), BF16), pltpu.VMEM((1, LANES), F32),
                        pltpu.VMEM((tm, MIX_WIDTH), F32), pltpu.VMEM((tm, MIX_WIDTH), F32)],
        compiler_params=_params("arbitrary"),
        name="proj_c",
    )(x, g_pre, w, w_v_t, fb.reshape(1, LANES), *consts)


def _fox_body(q_ref, k_ref, vt_ref, o_ref, acc_ref, sa_ref, sb_ref, mxa_ref, mxb_ref, *, tq):
    i = pl.program_id(2)
    causal = (lax.broadcasted_iota(jnp.int32, (tq, tq), 1)
              >= lax.broadcasted_iota(jnp.int32, (tq, tq), 0))
    acc_ref[...] = jnp.zeros_like(acc_ref)

    def scores(j, s_ref, mx_ref, diagonal):
        rows = pl.ds(pl.multiple_of(j * tq, tq), tq)
        for h in range(FOX_HEADS):
            lanes = slice(h * LANES, (h + 1) * LANES)
            st = _dot_nt(k_ref[0, rows, lanes], q_ref[0, :, lanes])
            if diagonal:
                st = jnp.where(causal, st, NEG_INF)
            s_ref[h] = st
            mx_ref[h] = jnp.max(st, axis=0, keepdims=True)

    def consume(j, s_ref, mx_ref, m_run):
        m_out = []
        for h in range(FOX_HEADS):
            m_new = jnp.maximum(m_run[h], mx_ref[h])
            alpha = jnp.exp2(m_run[h] - m_new)
            e = jnp.exp2(s_ref[h] - m_new).astype(BF16)
            acc_ref[h] = alpha * acc_ref[h] + _dot(vt_ref[j, h * V_ROWS:(h + 1) * V_ROWS, :], e)
            m_out.append(m_new)
        return tuple(m_out)

    def two_tiles(t, m_run):
        j = i - 2 * t
        scores(j - 1, sb_ref, mxb_ref, False)
        m_run = consume(j, sa_ref, mxa_ref, m_run)
        scores(j - 2, sa_ref, mxa_ref, False)
        return consume(j - 1, sb_ref, mxb_ref, m_run)

    def last_two(m_run):
        scores(0, sb_ref, mxb_ref, False)
        m_run = consume(1, sa_ref, mxa_ref, m_run)
        return consume(0, sb_ref, mxb_ref, m_run)

    def last_one(m_run):
        return consume(0, sa_ref, mxa_ref, m_run)

    scores(i, sa_ref, mxa_ref, True)
    m_run = tuple(jnp.full((1, tq), NEG_INF, F32) for _ in range(FOX_HEADS))
    m_run = lax.fori_loop(0, i // 2, two_tiles, m_run)
    lax.cond(i % 2 == 1, last_two, last_one, m_run)
    for p in range(FOX_HEADS // 2):
        halves = []
        for h in (2 * p, 2 * p + 1):
            halves.append(acc_ref[h, :HEAD_DIM, :] / acc_ref[h, HEAD_DIM:HEAD_DIM + 1, :])
        o_ref[0, :, p * PAIR:(p + 1) * PAIR] = jnp.concatenate(halves, axis=0).T.astype(BF16)


def _fox_attention(q_aug, k_aug, v_t, batch, seq):
    tq = tk = TK_FOX
    q3 = q_aug.reshape(batch, seq, AUG_WIDTH)
    k3 = k_aug.reshape(batch, seq, AUG_WIDTH)
    aug_w = FOX_HEADS * LANES
    v_w = FOX_HEADS * HEAD_DIM
    score_buf = pltpu.VMEM((FOX_HEADS, tk, tq), F32)
    max_buf = pltpu.VMEM((FOX_HEADS, 1, tq), F32)
    out = pl.pallas_call(
        functools.partial(_fox_body, tq=tq),
        grid=(batch, N_MIX_HEADS // FOX_HEADS, seq // tq),
        in_specs=[pl.BlockSpec((1, tq, aug_w), lambda b, p, i: (b, i, p)),
                  pl.BlockSpec((1, seq, aug_w), lambda b, p, i: (b, 0, p)),
                  pl.BlockSpec((seq // tk, FOX_HEADS * V_ROWS, tk), lambda b, p, i: (b, p, 0))],
        out_specs=pl.BlockSpec((1, tq, v_w), lambda b, p, i: (b, i, p)),
        out_shape=jax.ShapeDtypeStruct((batch, seq, MIX_WIDTH), BF16),
        scratch_shapes=[pltpu.VMEM((FOX_HEADS, V_ROWS, tq), F32), score_buf, score_buf,
                        max_buf, max_buf],
        compiler_params=_params("parallel", "parallel", "arbitrary"),
        name="fox_attention",
    )(q3, k3, v_t)
    return out.reshape(batch * seq, MIX_WIDTH)


def _out_c_body(x_ref, att_ref, qm_ref, kv_ref, w_ref, g_ref, o_ref):
    mem_out = _mem_attention(qm_ref[...], kv_ref).astype(BF16)
    y = _dot(att_ref[...], w_ref[:MIX_WIDTH, :]) + _dot(mem_out, w_ref[MIX_WIDTH:, :])
    _finish_mixer(x_ref, y, g_ref, o_ref)


def _out_c(x, seq, att, qm, kv, w_out, g_post):
    m, d = x.shape
    tm = TM_PROJ
    tiles_per_seq = seq // tm
    n_mem = kv.shape[1]
    row = lambda w: pl.BlockSpec((tm, w), lambda i: (i, 0))
    return pl.pallas_call(
        _out_c_body,
        grid=(m // tm,),
        in_specs=[row(d), row(MIX_WIDTH), row(MEM_WIDTH),
                  pl.BlockSpec((1, n_mem, 2 * MEM_WIDTH), lambda i: (i // tiles_per_seq, 0, 0)),
                  _const_spec(w_out.shape), _const_spec((1, d))],
        out_specs=row(d),
        out_shape=jax.ShapeDtypeStruct((m, d), F32),
        compiler_params=_params("parallel"),
        name="out_c",
    )(x, att, qm, kv, w_out, g_post)


def kernel(x, mem, positions, norm_g, mem_norm_g, w_mem_kv, ffn_w_gate_up, ffn_w_down,
           a_w_in, a_spatial_w, a_spatial_b, a_v_norm_g, a_w_out,
           b_w_in, b_w_out, c_w_in, c_forget_bias, c_w_out):
    batch, seq, d = x.shape
    depth = norm_g.shape[0]
    bf = lambda w: w.astype(BF16)

    kv_all = _mem_kv(mem, mem_norm_g, bf(w_mem_kv))
    w_gate_up, w_down = bf(ffn_w_gate_up), bf(ffn_w_down)
    cos = sin = None
    xf = x.reshape(batch * seq, d)
    for i in range(depth):
        kind, j = i % 3, i // 3
        g = norm_g[i].reshape(norm_g.shape[1], 1, d)
        xf = _ffn(xf, g[0], g[1], w_gate_up, w_down, (i, 0))
        kv = kv_all[i]
        if kind == 0:
            xf = _mixer_a(xf, seq, g[2], g[3], kv, bf(a_w_in[j]), bf(a_spatial_w[j]),
                          a_spatial_b[j], a_v_norm_g[j], bf(a_w_out[j]))
        elif kind == 1:
            if cos is None:
                cos, sin = _rope_tables(positions)
            *groups, qm = _proj_b(xf, batch, seq, g[2], bf(b_w_in[j]), cos, sin)
            outs, lses = zip(*[_dilated_group(qkv_g, w, dil)
                               for qkv_g, (w, dil) in zip(groups, DILATED_GROUPS)])
            xf = _out_b(xf, seq, outs, lses, qm, kv, bf(b_w_out[j]), g[3])
        else:
            q_aug, k_aug, v_t, qm = _proj_c(xf, seq, g[2], bf(c_w_in[j]), c_forget_bias[j])
            att = _fox_attention(q_aug, k_aug, v_t, batch, seq)
            xf = _out_c(xf, seq, att, qm, kv, bf(c_w_out[j]), g[3])
        xf = _ffn(xf, g[4], g[5], w_gate_up, w_down, (i, 1))
    return xf.reshape(batch, seq, d)
```

```python
import functools

import jax
import jax.numpy as jnp
from jax import lax
from jax.experimental import pallas as pl
from jax.experimental.pallas import tpu as pltpu

F32 = jnp.float32
BF16 = jnp.bfloat16

HEAD_DIM = 64
N_MIX_HEADS = 12
MIX_WIDTH = N_MIX_HEADS * HEAD_DIM
N_MEM_HEADS = 4
MEM_WIDTH = N_MEM_HEADS * HEAD_DIM
CHUNK = 128
ROPE_THETA = 10000.0
DILATED_GROUPS = ((128, 1), (512, 4), (2048, 16))
DIL_OUT_WIDTH = (N_MIX_HEADS // len(DILATED_GROUPS)) * HEAD_DIM
RMS_EPS = 1e-6
LN_EPS = 1e-5
NEG_INF = -1e30
ATTN_SCALE = HEAD_DIM ** -0.5
LOG2_E = 1.4426950408889634

LANES = 128
PAIR = 2 * HEAD_DIM
MXU_N = 256
VMEM_LIMIT_BYTES = 56 * 1024 * 1024

TM_FFN = 1024
FFN_SUB = 512
TM_PROJ = 512
TM_MIX_A = 512
TQ_FOX = 512
TK_FOX = 512
FOX_HEADS = 4
DIL_CHAINS = 4
FF_CHUNK = 256


def _params(*semantics):
    return pltpu.CompilerParams(dimension_semantics=semantics,
                                vmem_limit_bytes=VMEM_LIMIT_BYTES)


def _const_spec(shape):
    zeros = (0,) * len(shape)
    return pl.BlockSpec(shape, lambda *_: zeros, pipeline_mode=pl.Buffered(1))


def _rms_norm(x, g):
    ms = jnp.mean(x * x, axis=-1, keepdims=True)
    return x * lax.rsqrt(ms + RMS_EPS) * g


def _dot(a, b):
    return jnp.dot(a, b, preferred_element_type=F32)


def _dot_nt(a, b):
    return lax.dot_general(a, b, (((1,), (1,)), ((), ())), preferred_element_type=F32)


def _first_head_lanes(shape):
    return lax.broadcasted_iota(jnp.int32, shape, len(shape) - 1) % PAIR < HEAD_DIM


def _ffn_body(x_ref, gpre_ref, gpost_ref, wgu_ref, wd_ref, o_ref, xn_ref, act_ref, *, d_ff):
    for sub in range(x_ref.shape[0] // FFN_SUB):
        rows = slice(sub * FFN_SUB, (sub + 1) * FFN_SUB)
        xn_ref[rows, :] = _rms_norm(x_ref[rows, :], gpre_ref[...]).astype(BF16)
        for c in range(d_ff // FF_CHUNK):
            lo = c * FF_CHUNK
            gate = _dot(xn_ref[rows, :], wgu_ref[:, lo:lo + FF_CHUNK])
            up = _dot(xn_ref[rows, :], wgu_ref[:, d_ff + lo:d_ff + lo + FF_CHUNK])
            act_ref[rows, lo:lo + FF_CHUNK] = (gate * jax.nn.sigmoid(gate) * up).astype(BF16)
        y = _dot(act_ref[rows, :], wd_ref[...])
        o_ref[rows, :] = x_ref[rows, :] + 0.5 * _rms_norm(y, gpost_ref[...])


def _stacked_spec(stacked, index):
    k = len(index)
    tail = stacked.shape[k:]
    where = tuple(index) + (0,) * len(tail)
    return pl.BlockSpec((None,) * k + tail, lambda *_: where, pipeline_mode=pl.Buffered(1))


def _ffn(x, g_pre, g_post, w_gate_up, w_down, index):
    m, d = x.shape
    d_ff = w_down.shape[-2]
    assert m % TM_FFN == 0 and d_ff % FF_CHUNK == 0
    row = pl.BlockSpec((TM_FFN, d), lambda i: (i, 0))
    return pl.pallas_call(
        functools.partial(_ffn_body, d_ff=d_ff),
        grid=(m // TM_FFN,),
        in_specs=[row, _const_spec((1, d)), _const_spec((1, d)),
                  _stacked_spec(w_gate_up, index), _stacked_spec(w_down, index)],
        out_specs=row,
        out_shape=jax.ShapeDtypeStruct((m, d), F32),
        scratch_shapes=[pltpu.VMEM((TM_FFN, d), BF16), pltpu.VMEM((TM_FFN, d_ff), BF16)],
        compiler_params=_params("parallel"),
        name="ffn",
    )(x, g_pre, g_post, w_gate_up, w_down)


def _mem_kv_body(mem_ref, g_ref, w_ref, o_ref):
    o_ref[0, 0] = _dot(_rms_norm(mem_ref[0], g_ref[0]).astype(BF16), w_ref[0]).astype(BF16)


def _mem_kv(mem, mem_norm_g, w_mem_kv):
    b, n_mem, d = mem.shape
    depth = w_mem_kv.shape[0]
    return pl.pallas_call(
        _mem_kv_body,
        grid=(depth, b),
        in_specs=[pl.BlockSpec((1, n_mem, d), lambda l, i: (i, 0, 0)),
                  pl.BlockSpec((1, 1, d), lambda l, i: (l, 0, 0)),
                  pl.BlockSpec((1, d, 2 * MEM_WIDTH), lambda l, i: (l, 0, 0))],
        out_specs=pl.BlockSpec((1, 1, n_mem, 2 * MEM_WIDTH), lambda l, i: (l, i, 0, 0)),
        out_shape=jax.ShapeDtypeStruct((depth, b, n_mem, 2 * MEM_WIDTH), BF16),
        compiler_params=_params("parallel", "parallel"),
        name="mem_kv",
    )(mem, mem_norm_g.reshape(depth, 1, d), w_mem_kv)


def _stack_heads(qp, first):
    zero = jnp.zeros_like(qp)
    return jnp.concatenate([jnp.where(first, qp, zero), jnp.where(first, zero, qp)], axis=0)


def _softmax_pv(s, v, with_lse=False):
    mx = jnp.max(s, axis=-1, keepdims=True)
    e = jnp.exp(s - mx)
    denom = jnp.sum(e, axis=-1, keepdims=True)
    out = _dot(e.astype(BF16), v) / denom
    if with_lse:
        return out, mx + jnp.log(denom)
    return out


def _mem_attention(qm, kv_ref):
    tm = qm.shape[0]
    outs = []
    for p in range(MEM_WIDTH // PAIR):
        qp = qm[:, p * PAIR:(p + 1) * PAIR]
        k = kv_ref[0, :, p * PAIR:(p + 1) * PAIR]
        v = kv_ref[0, :, MEM_WIDTH + p * PAIR:MEM_WIDTH + (p + 1) * PAIR]
        first = _first_head_lanes(qp.shape)
        acc = _softmax_pv(_dot_nt(_stack_heads(qp, first), k), v)
        outs.append(jnp.where(first, acc[:tm], acc[tm:]))
    return jnp.concatenate(outs, axis=-1)


def _finish_mixer(x_ref, y, g_ref, o_ref):
    o_ref[...] = x_ref[...] + _rms_norm(y, g_ref[...])


def _mixer_a_body(x_ref, gpre_ref, win_ref, vg_ref, sw_ref, sb_ref, kv_ref, wout_ref, gpost_ref,
                  o_ref, u_ref, v_ref, gated_ref, wc_ref):
    tm = x_ref.shape[0]
    h = _rms_norm(x_ref[...], gpre_ref[...]).astype(BF16)
    u_ref[...] = jax.nn.gelu(_dot(h, win_ref[:, :MIX_WIDTH]))
    v = jax.nn.gelu(_dot(h, win_ref[:, MIX_WIDTH:2 * MIX_WIDTH]))
    mu = jnp.mean(v, axis=-1, keepdims=True)
    vc = v - mu
    var = jnp.mean(vc * vc, axis=-1, keepdims=True)
    v_ref[...] = (vc * lax.rsqrt(var + LN_EPS) * vg_ref[...]).astype(BF16)
    qm = (_dot(h, win_ref[:, 2 * MIX_WIDTH:]) * ATTN_SCALE).astype(BF16)

    t_idx = lax.broadcasted_iota(jnp.int32, (CHUNK, CHUNK), 0)
    s_idx = lax.broadcasted_iota(jnp.int32, (CHUNK, CHUNK), 1)
    for g in range(N_MIX_HEADS):
        wc_ref[g] = jnp.where(s_idx <= t_idx, sw_ref[g], jnp.zeros((CHUNK, CHUNK), BF16))

    first = _first_head_lanes((CHUNK, PAIR))
    for c in range(tm // CHUNK):
        rows = slice(c * CHUNK, (c + 1) * CHUNK)
        for p in range(MIX_WIDTH // PAIR):
            lanes = slice(p * PAIR, (p + 1) * PAIR)
            vp = v_ref[rows, lanes]
            mixed = jnp.where(first, _dot(wc_ref[2 * p], vp), _dot(wc_ref[2 * p + 1], vp))
            gated_ref[rows, lanes] = (u_ref[rows, lanes] * (mixed + sb_ref[:, lanes])).astype(BF16)

    mem_out = _mem_attention(qm, kv_ref).astype(BF16)
    y = _dot(gated_ref[...], wout_ref[:MIX_WIDTH, :]) + _dot(mem_out, wout_ref[MIX_WIDTH:, :])
    _finish_mixer(x_ref, y, gpost_ref, o_ref)


def _mixer_a(x, seq, g_pre, g_post, kv, w_in, spatial_w, spatial_b, v_norm_g, w_out):
    m, d = x.shape
    tm = TM_MIX_A
    assert seq % tm == 0 and tm % CHUNK == 0
    tiles_per_seq = seq // tm
    n_mem = kv.shape[1]
    row = pl.BlockSpec((tm, d), lambda i: (i, 0))
    bias = jnp.repeat(spatial_b.T, HEAD_DIM, axis=1)
    return pl.pallas_call(
        _mixer_a_body,
        grid=(m // tm,),
        in_specs=[row, _const_spec((1, d)), _const_spec(w_in.shape), _const_spec((1, MIX_WIDTH)),
                  _const_spec(spatial_w.shape), _const_spec(bias.shape),
                  pl.BlockSpec((1, n_mem, 2 * MEM_WIDTH), lambda i: (i // tiles_per_seq, 0, 0)),
                  _const_spec(w_out.shape), _const_spec((1, d))],
        out_specs=row,
        out_shape=jax.ShapeDtypeStruct((m, d), F32),
        scratch_shapes=[pltpu.VMEM((tm, MIX_WIDTH), F32), pltpu.VMEM((tm, MIX_WIDTH), BF16),
                        pltpu.VMEM((tm, MIX_WIDTH), BF16),
                        pltpu.VMEM((N_MIX_HEADS, CHUNK, CHUNK), BF16)],
        compiler_params=_params("parallel"),
        name="mixer_a",
    )(x, g_pre, w_in, v_norm_g.reshape(1, MIX_WIDTH), spatial_w, bias, kv, w_out, g_post)


def _rope_constants():
    inv_freq = ROPE_THETA ** (-jnp.arange(0, HEAD_DIM, 2, dtype=F32) / HEAD_DIM)
    freq = jnp.tile(inv_freq, LANES // (HEAD_DIM // 2)).reshape(1, LANES)
    half = jnp.concatenate([-jnp.ones((HEAD_DIM // 2,), F32), jnp.ones((HEAD_DIM // 2,), F32)])
    sign = jnp.tile(half, LANES // HEAD_DIM).reshape(1, LANES)
    return freq, sign


def _rope(t, cos, sin_signed):
    first_half = lax.broadcasted_iota(jnp.int32, t.shape, 1) % HEAD_DIM < HEAD_DIM // 2
    swapped = jnp.where(first_half,
                        pltpu.roll(t, LANES - HEAD_DIM // 2, axis=1),
                        pltpu.roll(t, HEAD_DIM // 2, axis=1))
    return t * cos + swapped * sin_signed


def _residue_spec(tm, dilation, width, tiles_per_seq):
    return pl.BlockSpec((1, dilation, tm // dilation, width),
                        lambda i: (i // tiles_per_seq, 0, i % tiles_per_seq, 0))


def _proj_b_body(x_ref, g_ref, w_ref, pos_ref, freq_ref, sign_ref, g0_ref, g1_ref, g2_ref, qm_ref,
                 h_ref, t_ref):
    tm = x_ref.shape[0]
    h_ref[...] = _rms_norm(x_ref[...], g_ref[...]).astype(BF16)
    angle = pos_ref[...].astype(F32) * freq_ref[...]
    cos = jnp.cos(angle)
    sin = jnp.sin(angle) * sign_ref[...]
    outs = (g0_ref, g1_ref, g2_ref)
    n_groups = len(DILATED_GROUPS)
    for c in range(3 * n_groups):
        kind, grp = divmod(c, n_groups)
        lo = c * DIL_OUT_WIDTH
        t = _dot(h_ref[...], w_ref[:, lo:lo + DIL_OUT_WIDTH])
        if kind < 2:
            scale = ATTN_SCALE if kind == 0 else 1.0
            t = jnp.concatenate(
                [_rope(t[:, j * LANES:(j + 1) * LANES], cos, sin)
                 for j in range(DIL_OUT_WIDTH // LANES)], axis=-1) * scale
        dil = DILATED_GROUPS[grp][1]
        cols = slice(kind * DIL_OUT_WIDTH, (kind + 1) * DIL_OUT_WIDTH)
        if dil == 1:
            outs[grp][0, 0, :, cols] = t.astype(BF16)
        else:
            halves = DIL_OUT_WIDTH // LANES
            for j in range(halves):
                t_ref[j] = t[:, j * LANES:(j + 1) * LANES]
            for r in range(dil):
                outs[grp][0, r, :, cols] = jnp.concatenate(
                    [t_ref[j, pl.ds(r, tm // dil, stride=dil), :] for j in range(halves)],
                    axis=-1).astype(BF16)
    qm_ref[...] = (_dot(h_ref[...], w_ref[:, 3 * MIX_WIDTH:]) * ATTN_SCALE).astype(BF16)


def _proj_b(x, batch, seq, g_pre, w_in, positions):
    m, d = x.shape
    tm = TM_PROJ
    tiles_per_seq = seq // tm
    freq, sign = _rope_constants()
    row = lambda w: pl.BlockSpec((tm, w), lambda i: (i, 0))
    width = 3 * DIL_OUT_WIDTH
    group_specs = [_residue_spec(tm, dil, width, tiles_per_seq) for _, dil in DILATED_GROUPS]
    group_shapes = [jax.ShapeDtypeStruct((batch, dil, seq // dil, width), BF16)
                    for _, dil in DILATED_GROUPS]
    return pl.pallas_call(
        _proj_b_body,
        grid=(m // tm,),
        in_specs=[row(d), _const_spec((1, d)), _const_spec(w_in.shape), row(1),
                  _const_spec((1, LANES)), _const_spec((1, LANES))],
        out_specs=group_specs + [row(MEM_WIDTH)],
        out_shape=group_shapes + [jax.ShapeDtypeStruct((m, MEM_WIDTH), BF16)],
        scratch_shapes=[pltpu.VMEM((tm, d), BF16),
                        pltpu.VMEM((DIL_OUT_WIDTH // LANES, tm, LANES), F32)],
        compiler_params=_params("parallel"),
        name="proj_b",
    )(x, g_pre, w_in, positions.reshape(m, 1), freq, sign)


def _dilated_body(qkv_ref, o_ref, lse_ref, *, n_blocks, span, unroll):
    w = DIL_OUT_WIDTH
    qi = lax.broadcasted_iota(jnp.int32, (2 * span, 2 * span), 0) % span
    ki = lax.broadcasted_iota(jnp.int32, (2 * span, 2 * span), 1)
    band = ((ki < span) & (ki >= qi)) | ((ki >= span) & (ki - span <= qi))
    own_only = (lax.broadcasted_iota(jnp.int32, (2 * span, span), 1)
                <= lax.broadcasted_iota(jnp.int32, (2 * span, span), 0) % span)
    first = _first_head_lanes((span, PAIR))
    own_lanes = jnp.concatenate([first, ~first], axis=0)

    def block(r0, has_prev):
        own = pl.ds(r0, span)
        keys = pl.ds(r0 - span, 2 * span) if has_prev else own
        for r in range(qkv_ref.shape[1]):
            for p in range(w // PAIR):
                ql = slice(p * PAIR, (p + 1) * PAIR)
                q2 = _stack_heads(qkv_ref[0, r, own, ql], first)
                k2 = qkv_ref[0, r, keys, w + p * PAIR:w + (p + 1) * PAIR]
                v2 = qkv_ref[0, r, keys, 2 * w + p * PAIR:2 * w + (p + 1) * PAIR]
                s = jnp.where(band if has_prev else own_only, _dot_nt(q2, k2), NEG_INF)
                acc, lse = _softmax_pv(s, v2, with_lse=True)
                o_ref[0, r, own, ql] = jnp.where(first, acc[:span], acc[span:]).astype(BF16)
                lse = jnp.where(own_lanes, lse, 0.0)
                lse_ref[0, r, own, ql] = lse[:span] + lse[span:]

    block(0, False)

    def step(blk, carry):
        block(pl.multiple_of(blk * span, span), True)
        return carry

    lax.fori_loop(1, n_blocks, step, 0, unroll=unroll)


def _dilated_group(qkv, window, dilation):
    b, _, length, width = qkv.shape
    span = window // dilation
    assert length % span == 0
    per_step = min(dilation, DIL_CHAINS)
    unroll = DIL_CHAINS // per_step
    assert dilation % per_step == 0
    out_spec = pl.BlockSpec((1, per_step, length, DIL_OUT_WIDTH), lambda i, r: (i, r, 0, 0))
    return pl.pallas_call(
        functools.partial(_dilated_body, n_blocks=length // span, span=span, unroll=unroll),
        grid=(b, dilation // per_step),
        in_specs=[pl.BlockSpec((1, per_step, length, width), lambda i, r: (i, r, 0, 0))],
        out_specs=[out_spec, out_spec],
        out_shape=[jax.ShapeDtypeStruct((b, dilation, length, DIL_OUT_WIDTH), BF16),
                   jax.ShapeDtypeStruct((b, dilation, length, DIL_OUT_WIDTH), F32)],
        compiler_params=_params("parallel", "parallel"),
        name=f"dilated_{dilation}",
    )(qkv)


def _out_b_body(x_ref, o0_ref, o1_ref, o2_ref, l0_ref, l1_ref, l2_ref, qm_ref, kv_ref, w_ref,
                g_ref, o_ref, *scratch):
    tm = x_ref.shape[0]

    def token_major(ref, buf):
        dil = ref.shape[1]
        if dil == 1:
            return ref[0, 0].astype(F32)
        halves = DIL_OUT_WIDTH // LANES
        for r in range(dil):
            rows = ref[0, r].astype(F32)
            for j in range(halves):
                buf[j, pl.ds(r, tm // dil, stride=dil), :] = rows[:, j * LANES:(j + 1) * LANES]
        return jnp.concatenate([buf[j] for j in range(halves)], axis=-1)

    outs = [token_major(r, scratch[2 * g]) for g, r in enumerate((o0_ref, o1_ref, o2_ref))]
    lses = [token_major(r, scratch[2 * g + 1]) for g, r in enumerate((l0_ref, l1_ref, l2_ref))]
    mx = jnp.maximum(jnp.maximum(lses[0], lses[1]), lses[2])
    es = [jnp.exp(l - mx) for l in lses]
    denom = es[0] + es[1] + es[2]
    merged = sum((e / denom) * o for e, o in zip(es, outs))
    mem_out = _mem_attention(qm_ref[...], kv_ref).astype(BF16)
    y = (_dot(merged.astype(BF16), w_ref[:DIL_OUT_WIDTH, :])
         + _dot(mem_out, w_ref[DIL_OUT_WIDTH:, :]))
    _finish_mixer(x_ref, y, g_ref, o_ref)


def _out_b(x, seq, outs, lses, qm, kv, w_out, g_post):
    m, d = x.shape
    tm = TM_PROJ
    tiles_per_seq = seq // tm
    n_mem = kv.shape[1]
    row = lambda w: pl.BlockSpec((tm, w), lambda i: (i, 0))
    group_specs = [_residue_spec(tm, dil, DIL_OUT_WIDTH, tiles_per_seq)
                   for _, dil in DILATED_GROUPS]
    return pl.pallas_call(
        _out_b_body,
        grid=(m // tm,),
        in_specs=[row(d)] + group_specs * 2 + [
            row(MEM_WIDTH),
            pl.BlockSpec((1, n_mem, 2 * MEM_WIDTH), lambda i: (i // tiles_per_seq, 0, 0)),
            _const_spec(w_out.shape), _const_spec((1, d))],
        out_specs=row(d),
        out_shape=jax.ShapeDtypeStruct((m, d), F32),
        scratch_shapes=[pltpu.VMEM((DIL_OUT_WIDTH // LANES, tm, LANES), F32)]
        * (2 * len(DILATED_GROUPS)),
        compiler_params=_params("parallel"),
        name="out_b",
    )(x, *outs, *lses, qm, kv, w_out, g_post)


def _split3(x):
    hi = x.astype(BF16)
    r1 = x - hi.astype(F32)
    mid = r1.astype(BF16)
    lo = (r1 - mid.astype(F32)).astype(BF16)
    return hi, mid, lo


N_BIAS_TERMS = 3
AUG_WIDTH = N_MIX_HEADS * LANES
V_ROWS = HEAD_DIM + 16


def _pack_terms(hi, mid, lo):
    return (hi.astype(F32) + pltpu.roll(mid.astype(F32), N_MIX_HEADS, axis=1)
            + pltpu.roll(lo.astype(F32), 2 * N_MIX_HEADS, axis=1)).astype(BF16)


def _bias_selectors():
    rows = jnp.arange(LANES)
    term, head = rows // N_MIX_HEADS, rows % N_MIX_HEADS
    valid = rows < N_BIAS_TERMS * N_MIX_HEADS
    cols = jnp.arange(MIX_WIDTH)
    base = (head // 2) * PAIR + jnp.where(head % 2 == 0, HEAD_DIM, 0)

    def select(offset):
        target = base + offset + term
        return ((cols[None, :] == target[:, None]) & valid[:, None]).astype(BF16)

    def ones(offset):
        lane = cols % HEAD_DIM
        return ((lane >= offset) & (lane < offset + N_BIAS_TERMS)).astype(F32)[None, :]

    q_sel, k_sel = select(0), select(N_BIAS_TERMS)
    q_one, k_one = ones(N_BIAS_TERMS), ones(0)
    return q_sel, k_sel, q_one, k_one


def _proj_c_body(x_ref, g_ref, w_ref, wvt_ref, wf_ref, wqm_ref, fb_ref, qsel_ref, ksel_ref,
                 qone_ref, kone_ref, q_ref, k_ref, vt_ref, qm_ref, h_ref, carry_ref, qext_ref,
                 kext_ref, *, tiles_per_seq):
    tm = x_ref.shape[0]

    @pl.when(pl.program_id(0) % tiles_per_seq == 0)
    def _():
        carry_ref[...] = jnp.zeros_like(carry_ref)

    h_ref[...] = _rms_norm(x_ref[...], g_ref[...]).astype(BF16)

    z = _dot(h_ref[...], wf_ref[...]) + fb_ref[...]
    log_f = jnp.minimum(z, 0.0) - jnp.log1p(jnp.exp(-jnp.abs(z)))
    lane = lax.broadcasted_iota(jnp.int32, (tm, LANES), 1)
    log_f = jnp.where(lane < N_MIX_HEADS, log_f, 0.0)
    tri = (lax.broadcasted_iota(jnp.int32, (tm, tm), 1)
           <= lax.broadcasted_iota(jnp.int32, (tm, tm), 0)).astype(BF16)
    sums = _dot(tri, _pack_terms(*_split3(log_f)))
    local = (sums + pltpu.roll(sums, LANES - N_MIX_HEADS, axis=1)
             + pltpu.roll(sums, LANES - 2 * N_MIX_HEADS, axis=1))
    c = carry_ref[...] + jnp.where(lane < N_MIX_HEADS, local, 0.0)
    carry_ref[...] = c[tm - 1:tm, :]
    terms = _pack_terms(*_split3(c * LOG2_E))
    qext_ref[...] = _dot(terms, qsel_ref[...]) + qone_ref[...]
    kext_ref[...] = kone_ref[...] - _dot(terms, ksel_ref[...])

    first = _first_head_lanes((tm, PAIR))
    for c_idx in range(2 * MIX_WIDTH // MXU_N):
        lo = c_idx * MXU_N
        is_q = lo < MIX_WIDTH
        t = _dot(h_ref[...], w_ref[:, lo:lo + MXU_N])
        if is_q:
            t = t * (ATTN_SCALE * LOG2_E)
        dst, ext_ref = (q_ref, qext_ref) if is_q else (k_ref, kext_ref)
        for j in range(MXU_N // PAIR):
            p = (lo % MIX_WIDTH) // PAIR + j
            pair = t[:, j * PAIR:(j + 1) * PAIR]
            ext = ext_ref[:, p * PAIR:(p + 1) * PAIR]
            dst[:, 2 * p * LANES:(2 * p + 1) * LANES] = jnp.where(first, pair, ext).astype(BF16)
            dst[:, (2 * p + 1) * LANES:(2 * p + 2) * LANES] = pltpu.roll(
                jnp.where(first, ext, pair), HEAD_DIM, axis=1).astype(BF16)
    ones_row = (lax.broadcasted_iota(jnp.int32, (V_ROWS - HEAD_DIM, tm), 0) == 0).astype(BF16)
    for c_idx in range(MIX_WIDTH // MXU_N):
        lo = c_idx * MXU_N
        vt = _dot_nt(wvt_ref[lo:lo + MXU_N, :], h_ref[...]).astype(BF16)
        for j in range(MXU_N // HEAD_DIM):
            head = lo // HEAD_DIM + j
            vt_ref[0, head * V_ROWS:head * V_ROWS + HEAD_DIM, :] = vt[j * HEAD_DIM:(j + 1) * HEAD_DIM]
            vt_ref[0, head * V_ROWS + HEAD_DIM:(head + 1) * V_ROWS, :] = ones_row
    qm_ref[...] = (_dot(h_ref[...], wqm_ref[...]) * ATTN_SCALE).astype(BF16)


def _proj_c(x, seq, g_pre, w_in, forget_bias):
    m, d = x.shape
    tm = TK_FOX
    qkv_w = 3 * MIX_WIDTH
    w_v_t = w_in[:, 2 * MIX_WIDTH:qkv_w].T
    w_f = jnp.pad(w_in[:, qkv_w:qkv_w + N_MIX_HEADS], ((0, 0), (0, LANES - N_MIX_HEADS)))
    w_qm = w_in[:, qkv_w + N_MIX_HEADS:]
    fb = jnp.concatenate([forget_bias.astype(F32), jnp.zeros((LANES - N_MIX_HEADS,), F32)])
    consts = _bias_selectors()
    row = lambda w_: pl.BlockSpec((tm, w_), lambda i: (i, 0))
    return pl.pallas_call(
        functools.partial(_proj_c_body, tiles_per_seq=seq // tm),
        grid=(m // tm,),
        in_specs=[row(d), _const_spec((1, d)), _const_spec(w_in.shape), _const_spec(w_v_t.shape),
                  _const_spec(w_f.shape), _const_spec(w_qm.shape), _const_spec((1, LANES))]
        + [_const_spec(c.shape) for c in consts],
        out_specs=[row(AUG_WIDTH), row(AUG_WIDTH),
                   pl.BlockSpec((1, N_MIX_HEADS * V_ROWS, tm), lambda i: (i, 0, 0)),
                   row(MEM_WIDTH)],
        out_shape=[jax.ShapeDtypeStruct((m, AUG_WIDTH), BF16),
                   jax.ShapeDtypeStruct((m, AUG_WIDTH), BF16),
                   jax.ShapeDtypeStruct((m // tm, N_MIX_HEADS * V_ROWS, tm), BF16),
                   jax.ShapeDtypeStruct((m, MEM_WIDTH), BF16)],
        scratch_shapes=[pltpu.VMEM((tm, d), BF16), pltpu.VMEM((1, LANES), F32),
                        pltpu.VMEM((tm, MIX_WIDTH), F32), pltpu.VMEM((tm, MIX_WIDTH), F32)],
        compiler_params=_params("arbitrary"),
        name="proj_c",
    )(x, g_pre, w_in, w_v_t, w_f, w_qm, fb.reshape(1, LANES), *consts)


def _fox_body(q_ref, qnext_ref, k_ref, vt_ref, o_ref, acc_ref, s0_ref, s1_ref, s2_ref,
              mx0_ref, mx1_ref, mx2_ref, *, tq, n_q):
    i = pl.program_id(2)
    causal = (lax.broadcasted_iota(jnp.int32, (tq, tq), 1)
              >= lax.broadcasted_iota(jnp.int32, (tq, tq), 0))
    acc_ref[...] = jnp.zeros_like(acc_ref)

    def scores(queries_ref, j, buf, diagonal):
        s_ref, mx_ref = buf
        rows = pl.ds(pl.multiple_of(j * tq, tq), tq)
        for h in range(FOX_HEADS):
            lanes = slice(h * LANES, (h + 1) * LANES)
            st = _dot_nt(k_ref[0, rows, lanes], queries_ref[0, :, lanes])
            if diagonal:
                st = jnp.where(causal, st, NEG_INF)
            s_ref[h] = st
            mx_ref[h] = jnp.max(st, axis=0, keepdims=True)

    def consume(j, buf, m_run):
        s_ref, mx_ref = buf
        m_out = []
        for h in range(FOX_HEADS):
            m_new = jnp.maximum(m_run[h], mx_ref[h])
            alpha = jnp.exp2(m_run[h] - m_new)
            e = jnp.exp2(s_ref[h] - m_new).astype(BF16)
            acc_ref[h] = alpha * acc_ref[h] + _dot(vt_ref[j, h * V_ROWS:(h + 1) * V_ROWS, :], e)
            m_out.append(m_new)
        return tuple(m_out)

    def run(diag, other, handover):
        @pl.when(i == 0)
        def _():
            scores(q_ref, i, diag, True)

        def prefetch_next_tile():
            scores(qnext_ref, jnp.minimum(i + 1, n_q - 1), handover, True)

        def two_tiles(t, m_run):
            j = i - 2 * t
            scores(q_ref, j - 1, other, False)
            m_run = consume(j, diag, m_run)
            scores(q_ref, j - 2, diag, False)
            return consume(j - 1, other, m_run)

        def last_two(m_run):
            scores(q_ref, 0, other, False)
            m_run = consume(1, diag, m_run)
            prefetch_next_tile()
            return consume(0, other, m_run)

        def last_one(m_run):
            prefetch_next_tile()
            return consume(0, diag, m_run)

        m_run = tuple(jnp.full((1, tq), NEG_INF, F32) for _ in range(FOX_HEADS))
        m_run = lax.fori_loop(0, i // 2, two_tiles, m_run)
        lax.cond(i % 2 == 1, last_two, last_one, m_run)

    bufs = ((s0_ref, mx0_ref), (s1_ref, mx1_ref), (s2_ref, mx2_ref))
    lax.cond(i % 2 == 0,
             lambda: run(bufs[0], bufs[1], bufs[2]),
             lambda: run(bufs[2], bufs[1], bufs[0]))
    for p in range(FOX_HEADS // 2):
        halves = []
        for h in (2 * p, 2 * p + 1):
            halves.append(acc_ref[h, :HEAD_DIM, :] / acc_ref[h, HEAD_DIM:HEAD_DIM + 1, :])
        o_ref[0, :, p * PAIR:(p + 1) * PAIR] = jnp.concatenate(halves, axis=0).T.astype(BF16)


def _fox_attention(q_aug, k_aug, v_t, batch, seq):
    tq = tk = TK_FOX
    q3 = q_aug.reshape(batch, seq, AUG_WIDTH)
    k3 = k_aug.reshape(batch, seq, AUG_WIDTH)
    aug_w = FOX_HEADS * LANES
    v_w = FOX_HEADS * HEAD_DIM
    score_buf = pltpu.VMEM((FOX_HEADS, tk, tq), F32)
    max_buf = pltpu.VMEM((FOX_HEADS, 1, tq), F32)
    n_q = seq // tq
    out = pl.pallas_call(
        functools.partial(_fox_body, tq=tq, n_q=n_q),
        grid=(batch, N_MIX_HEADS // FOX_HEADS, n_q),
        in_specs=[pl.BlockSpec((1, tq, aug_w), lambda b, p, i: (b, i, p)),
                  pl.BlockSpec((1, tq, aug_w), lambda b, p, i: (b, jnp.minimum(i + 1, n_q - 1), p)),
                  pl.BlockSpec((1, seq, aug_w), lambda b, p, i: (b, 0, p)),
                  pl.BlockSpec((seq // tk, FOX_HEADS * V_ROWS, tk), lambda b, p, i: (b, p, 0))],
        out_specs=pl.BlockSpec((1, tq, v_w), lambda b, p, i: (b, i, p)),
        out_shape=jax.ShapeDtypeStruct((batch, seq, MIX_WIDTH), BF16),
        scratch_shapes=[pltpu.VMEM((FOX_HEADS, V_ROWS, tq), F32)] + [score_buf] * 3
        + [max_buf] * 3,
        compiler_params=_params("parallel", "arbitrary", "arbitrary"),
        name="fox_attention",
    )(q3, q3, k3, v_t)
    return out.reshape(batch * seq, MIX_WIDTH)


def _out_c_body(x_ref, att_ref, qm_ref, kv_ref, w_ref, g_ref, o_ref):
    mem_out = _mem_attention(qm_ref[...], kv_ref).astype(BF16)
    y = _dot(att_ref[...], w_ref[:MIX_WIDTH, :]) + _dot(mem_out, w_ref[MIX_WIDTH:, :])
    _finish_mixer(x_ref, y, g_ref, o_ref)


def _out_c(x, seq, att, qm, kv, w_out, g_post):
    m, d = x.shape
    tm = TM_PROJ
    tiles_per_seq = seq // tm
    n_mem = kv.shape[1]
    row = lambda w: pl.BlockSpec((tm, w), lambda i: (i, 0))
    return pl.pallas_call(
        _out_c_body,
        grid=(m // tm,),
        in_specs=[row(d), row(MIX_WIDTH), row(MEM_WIDTH),
                  pl.BlockSpec((1, n_mem, 2 * MEM_WIDTH), lambda i: (i // tiles_per_seq, 0, 0)),
                  _const_spec(w_out.shape), _const_spec((1, d))],
        out_specs=row(d),
        out_shape=jax.ShapeDtypeStruct((m, d), F32),
        compiler_params=_params("parallel"),
        name="out_c",
    )(x, att, qm, kv, w_out, g_post)


def kernel(x, mem, positions, norm_g, mem_norm_g, w_mem_kv, ffn_w_gate_up, ffn_w_down,
           a_w_in, a_spatial_w, a_spatial_b, a_v_norm_g, a_w_out,
           b_w_in, b_w_out, c_w_in, c_forget_bias, c_w_out):
    batch, seq, d = x.shape
    depth = norm_g.shape[0]
    bf = lambda w: w.astype(BF16)

    kv_all = _mem_kv(mem, mem_norm_g, bf(w_mem_kv))
    w_gate_up, w_down = bf(ffn_w_gate_up), bf(ffn_w_down)
    xf = x.reshape(batch * seq, d)
    for i in range(depth):
        kind, j = i % 3, i // 3
        g = norm_g[i].reshape(norm_g.shape[1], 1, d)
        xf = _ffn(xf, g[0], g[1], w_gate_up, w_down, (i, 0))
        kv = kv_all[i]
        if kind == 0:
            xf = _mixer_a(xf, seq, g[2], g[3], kv, bf(a_w_in[j]), bf(a_spatial_w[j]),
                          a_spatial_b[j], a_v_norm_g[j], bf(a_w_out[j]))
        elif kind == 1:
            *groups, qm = _proj_b(xf, batch, seq, g[2], bf(b_w_in[j]), positions)
            outs, lses = zip(*[_dilated_group(qkv_g, w, dil)
                               for qkv_g, (w, dil) in zip(groups, DILATED_GROUPS)])
            xf = _out_b(xf, seq, outs, lses, qm, kv, bf(b_w_out[j]), g[3])
        else:
            q_aug, k_aug, v_t, qm = _proj_c(xf, seq, g[2], bf(c_w_in[j]), c_forget_bias[j])
            att = _fox_attention(q_aug, k_aug, v_t, batch, seq)
            xf = _out_c(xf, seq, att, qm, kv, bf(c_w_out[j]), g[3])
        xf = _ffn(xf, g[4], g[5], w_gate_up, w_down, (i, 1))
    return xf.reshape(batch, seq, d)
```

```python
import functools

import jax
import jax.numpy as jnp
from jax import lax
from jax.experimental import pallas as pl
from jax.experimental.pallas import tpu as pltpu

F32 = jnp.float32
BF16 = jnp.bfloat16

HEAD_DIM = 64
N_MIX_HEADS = 12
MIX_WIDTH = N_MIX_HEADS * HEAD_DIM
N_MEM_HEADS = 4
MEM_WIDTH = N_MEM_HEADS * HEAD_DIM
CHUNK = 128
ROPE_THETA = 10000.0
DILATED_GROUPS = ((128, 1), (512, 4), (2048, 16))
DIL_OUT_WIDTH = (N_MIX_HEADS // len(DILATED_GROUPS)) * HEAD_DIM
RMS_EPS = 1e-6
LN_EPS = 1e-5
NEG_INF = -1e30
ATTN_SCALE = HEAD_DIM ** -0.5
LOG2_E = 1.4426950408889634

LANES = 128
PAIR = 2 * HEAD_DIM
MXU_N = 256
VMEM_LIMIT_BYTES = 56 * 1024 * 1024

TM_FFN = 1024
FFN_SUB = 512
TM_PROJ = 512
TM_MIX_A = 1024
MIX_A_SUB = 512
TM_OUT = 1024
OUT_SUB = 512
TQ_FOX = 512
TK_FOX = 512
FOX_HEADS = 4
DIL_CHAINS = 4
FF_CHUNK = 256


def _params(*semantics):
    return pltpu.CompilerParams(dimension_semantics=semantics,
                                vmem_limit_bytes=VMEM_LIMIT_BYTES)


def _const_spec(shape):
    zeros = (0,) * len(shape)
    return pl.BlockSpec(shape, lambda *_: zeros, pipeline_mode=pl.Buffered(1))


def _rms_norm(x, g):
    ms = jnp.mean(x * x, axis=-1, keepdims=True)
    return x * lax.rsqrt(ms + RMS_EPS) * g


def _dot(a, b):
    return jnp.dot(a, b, preferred_element_type=F32)


def _dot_nt(a, b):
    return lax.dot_general(a, b, (((1,), (1,)), ((), ())), preferred_element_type=F32)


def _first_head_lanes(shape):
    return lax.broadcasted_iota(jnp.int32, shape, len(shape) - 1) % PAIR < HEAD_DIM


def _ffn_body(x_ref, gpre_ref, gpost_ref, wgu_ref, wd_ref, o_ref, xn_ref, act_ref, *, d_ff):
    for sub in range(x_ref.shape[0] // FFN_SUB):
        rows = slice(sub * FFN_SUB, (sub + 1) * FFN_SUB)
        xn_ref[rows, :] = _rms_norm(x_ref[rows, :], gpre_ref[...]).astype(BF16)
        for c in range(d_ff // FF_CHUNK):
            lo = c * FF_CHUNK
            gate = _dot(xn_ref[rows, :], wgu_ref[:, lo:lo + FF_CHUNK])
            up = _dot(xn_ref[rows, :], wgu_ref[:, d_ff + lo:d_ff + lo + FF_CHUNK])
            act_ref[rows, lo:lo + FF_CHUNK] = (gate * jax.nn.sigmoid(gate) * up).astype(BF16)
        y = _dot(act_ref[rows, :], wd_ref[...])
        o_ref[rows, :] = x_ref[rows, :] + 0.5 * _rms_norm(y, gpost_ref[...])


def _stacked_spec(stacked, index):
    k = len(index)
    tail = stacked.shape[k:]
    where = tuple(index) + (0,) * len(tail)
    return pl.BlockSpec((None,) * k + tail, lambda *_: where, pipeline_mode=pl.Buffered(1))


def _ffn(x, g_pre, g_post, w_gate_up, w_down, index):
    m, d = x.shape
    d_ff = w_down.shape[-2]
    assert m % TM_FFN == 0 and d_ff % FF_CHUNK == 0
    row = pl.BlockSpec((TM_FFN, d), lambda i: (i, 0))
    return pl.pallas_call(
        functools.partial(_ffn_body, d_ff=d_ff),
        grid=(m // TM_FFN,),
        in_specs=[row, _const_spec((1, d)), _const_spec((1, d)),
                  _stacked_spec(w_gate_up, index), _stacked_spec(w_down, index)],
        out_specs=row,
        out_shape=jax.ShapeDtypeStruct((m, d), F32),
        scratch_shapes=[pltpu.VMEM((TM_FFN, d), BF16), pltpu.VMEM((TM_FFN, d_ff), BF16)],
        compiler_params=_params("parallel"),
        name="ffn",
    )(x, g_pre, g_post, w_gate_up, w_down)


def _mem_kv_body(mem_ref, g_ref, w_ref, o_ref):
    o_ref[0, 0] = _dot(_rms_norm(mem_ref[0], g_ref[0]).astype(BF16), w_ref[0]).astype(BF16)


def _mem_kv(mem, mem_norm_g, w_mem_kv):
    b, n_mem, d = mem.shape
    depth = w_mem_kv.shape[0]
    return pl.pallas_call(
        _mem_kv_body,
        grid=(depth, b),
        in_specs=[pl.BlockSpec((1, n_mem, d), lambda l, i: (i, 0, 0)),
                  pl.BlockSpec((1, 1, d), lambda l, i: (l, 0, 0)),
                  pl.BlockSpec((1, d, 2 * MEM_WIDTH), lambda l, i: (l, 0, 0))],
        out_specs=pl.BlockSpec((1, 1, n_mem, 2 * MEM_WIDTH), lambda l, i: (l, i, 0, 0)),
        out_shape=jax.ShapeDtypeStruct((depth, b, n_mem, 2 * MEM_WIDTH), BF16),
        compiler_params=_params("parallel", "parallel"),
        name="mem_kv",
    )(mem, mem_norm_g.reshape(depth, 1, d), w_mem_kv)


def _stack_heads(qp, first):
    zero = jnp.zeros_like(qp)
    return jnp.concatenate([jnp.where(first, qp, zero), jnp.where(first, zero, qp)], axis=0)


def _softmax_pv(s, v, with_lse=False):
    mx = jnp.max(s, axis=-1, keepdims=True)
    e = jnp.exp(s - mx)
    denom = jnp.sum(e, axis=-1, keepdims=True)
    out = _dot(e.astype(BF16), v) / denom
    if with_lse:
        return out, mx + jnp.log(denom)
    return out


def _mem_attention(qm, kv_ref):
    tm = qm.shape[0]
    outs = []
    for p in range(MEM_WIDTH // PAIR):
        qp = qm[:, p * PAIR:(p + 1) * PAIR]
        k = kv_ref[0, :, p * PAIR:(p + 1) * PAIR]
        v = kv_ref[0, :, MEM_WIDTH + p * PAIR:MEM_WIDTH + (p + 1) * PAIR]
        first = _first_head_lanes(qp.shape)
        acc = _softmax_pv(_dot_nt(_stack_heads(qp, first), k), v)
        outs.append(jnp.where(first, acc[:tm], acc[tm:]))
    return jnp.concatenate(outs, axis=-1)


def _mixer_a_body(x_ref, gpre_ref, win_ref, vg_ref, sw_ref, sb_ref, kv_ref, wout_ref, gpost_ref,
                  o_ref, u_ref, v_ref, gated_ref, wc_ref):
    t_idx = lax.broadcasted_iota(jnp.int32, (CHUNK, CHUNK), 0)
    s_idx = lax.broadcasted_iota(jnp.int32, (CHUNK, CHUNK), 1)
    for g in range(N_MIX_HEADS):
        wc_ref[g] = jnp.where(s_idx <= t_idx, sw_ref[g], jnp.zeros((CHUNK, CHUNK), BF16))
    first = _first_head_lanes((CHUNK, PAIR))

    for sub in range(x_ref.shape[0] // MIX_A_SUB):
        tile = slice(sub * MIX_A_SUB, (sub + 1) * MIX_A_SUB)
        h = _rms_norm(x_ref[tile, :], gpre_ref[...]).astype(BF16)
        u_ref[tile, :] = jax.nn.gelu(_dot(h, win_ref[:, :MIX_WIDTH]))
        v = jax.nn.gelu(_dot(h, win_ref[:, MIX_WIDTH:2 * MIX_WIDTH]))
        mu = jnp.mean(v, axis=-1, keepdims=True)
        vc = v - mu
        var = jnp.mean(vc * vc, axis=-1, keepdims=True)
        v_ref[tile, :] = (vc * lax.rsqrt(var + LN_EPS) * vg_ref[...]).astype(BF16)
        qm = (_dot(h, win_ref[:, 2 * MIX_WIDTH:]) * ATTN_SCALE).astype(BF16)

        for c in range(sub * MIX_A_SUB // CHUNK, (sub + 1) * MIX_A_SUB // CHUNK):
            rows = slice(c * CHUNK, (c + 1) * CHUNK)
            for p in range(MIX_WIDTH // PAIR):
                lanes = slice(p * PAIR, (p + 1) * PAIR)
                vp = v_ref[rows, lanes]
                mixed = jnp.where(first, _dot(wc_ref[2 * p], vp), _dot(wc_ref[2 * p + 1], vp))
                gated_ref[rows, lanes] = (u_ref[rows, lanes]
                                          * (mixed + sb_ref[:, lanes])).astype(BF16)

        mem_out = _mem_attention(qm, kv_ref).astype(BF16)
        y = (_dot(gated_ref[tile, :], wout_ref[:MIX_WIDTH, :])
             + _dot(mem_out, wout_ref[MIX_WIDTH:, :]))
        o_ref[tile, :] = x_ref[tile, :] + _rms_norm(y, gpost_ref[...])


def _mixer_a(x, seq, g_pre, g_post, kv, w_in, spatial_w, spatial_b, v_norm_g, w_out):
    m, d = x.shape
    tm = TM_MIX_A
    assert seq % tm == 0 and tm % CHUNK == 0
    tiles_per_seq = seq // tm
    n_mem = kv.shape[1]
    row = pl.BlockSpec((tm, d), lambda i: (i, 0))
    bias = jnp.repeat(spatial_b.T, HEAD_DIM, axis=1)
    return pl.pallas_call(
        _mixer_a_body,
        grid=(m // tm,),
        in_specs=[row, _const_spec((1, d)), _const_spec(w_in.shape), _const_spec((1, MIX_WIDTH)),
                  _const_spec(spatial_w.shape), _const_spec(bias.shape),
                  pl.BlockSpec((1, n_mem, 2 * MEM_WIDTH), lambda i: (i // tiles_per_seq, 0, 0)),
                  _const_spec(w_out.shape), _const_spec((1, d))],
        out_specs=row,
        out_shape=jax.ShapeDtypeStruct((m, d), F32),
        scratch_shapes=[pltpu.VMEM((tm, MIX_WIDTH), F32), pltpu.VMEM((tm, MIX_WIDTH), BF16),
                        pltpu.VMEM((tm, MIX_WIDTH), BF16),
                        pltpu.VMEM((N_MIX_HEADS, CHUNK, CHUNK), BF16)],
        compiler_params=_params("parallel"),
        name="mixer_a",
    )(x, g_pre, w_in, v_norm_g.reshape(1, MIX_WIDTH), spatial_w, bias, kv, w_out, g_post)


def _rope_constants():
    inv_freq = ROPE_THETA ** (-jnp.arange(0, HEAD_DIM, 2, dtype=F32) / HEAD_DIM)
    freq = jnp.tile(inv_freq, LANES // (HEAD_DIM // 2)).reshape(1, LANES)
    half = jnp.concatenate([-jnp.ones((HEAD_DIM // 2,), F32), jnp.ones((HEAD_DIM // 2,), F32)])
    sign = jnp.tile(half, LANES // HEAD_DIM).reshape(1, LANES)
    return freq, sign


def _rope(t, cos, sin_signed):
    first_half = lax.broadcasted_iota(jnp.int32, t.shape, 1) % HEAD_DIM < HEAD_DIM // 2
    swapped = jnp.where(first_half,
                        pltpu.roll(t, LANES - HEAD_DIM // 2, axis=1),
                        pltpu.roll(t, HEAD_DIM // 2, axis=1))
    return t * cos + swapped * sin_signed


def _residue_spec(tm, dilation, width, tiles_per_seq):
    return pl.BlockSpec((1, dilation, tm // dilation, width),
                        lambda i: (i // tiles_per_seq, 0, i % tiles_per_seq, 0))


def _proj_b_body(x_ref, g_ref, w_ref, pos_ref, freq_ref, sign_ref, g0_ref, g1_ref, g2_ref, qm_ref,
                 h_ref, t_ref):
    tm = x_ref.shape[0]
    h_ref[...] = _rms_norm(x_ref[...], g_ref[...]).astype(BF16)
    angle = pos_ref[...].astype(F32) * freq_ref[...]
    cos = jnp.cos(angle)
    sin = jnp.sin(angle) * sign_ref[...]
    outs = (g0_ref, g1_ref, g2_ref)
    n_groups = len(DILATED_GROUPS)
    for c in range(3 * n_groups):
        kind, grp = divmod(c, n_groups)
        lo = c * DIL_OUT_WIDTH
        t = _dot(h_ref[...], w_ref[:, lo:lo + DIL_OUT_WIDTH])
        if kind < 2:
            scale = ATTN_SCALE if kind == 0 else 1.0
            t = jnp.concatenate(
                [_rope(t[:, j * LANES:(j + 1) * LANES], cos, sin)
                 for j in range(DIL_OUT_WIDTH // LANES)], axis=-1) * scale
        dil = DILATED_GROUPS[grp][1]
        cols = slice(kind * DIL_OUT_WIDTH, (kind + 1) * DIL_OUT_WIDTH)
        if dil == 1:
            outs[grp][0, 0, :, cols] = t.astype(BF16)
        else:
            halves = DIL_OUT_WIDTH // LANES
            for j in range(halves):
                t_ref[j] = t[:, j * LANES:(j + 1) * LANES]
            for r in range(dil):
                outs[grp][0, r, :, cols] = jnp.concatenate(
                    [t_ref[j, pl.ds(r, tm // dil, stride=dil), :] for j in range(halves)],
                    axis=-1).astype(BF16)
    qm_ref[...] = (_dot(h_ref[...], w_ref[:, 3 * MIX_WIDTH:]) * ATTN_SCALE).astype(BF16)


def _proj_b(x, batch, seq, g_pre, w_in, positions):
    m, d = x.shape
    tm = TM_PROJ
    tiles_per_seq = seq // tm
    freq, sign = _rope_constants()
    row = lambda w: pl.BlockSpec((tm, w), lambda i: (i, 0))
    width = 3 * DIL_OUT_WIDTH
    group_specs = [_residue_spec(tm, dil, width, tiles_per_seq) for _, dil in DILATED_GROUPS]
    group_shapes = [jax.ShapeDtypeStruct((batch, dil, seq // dil, width), BF16)
                    for _, dil in DILATED_GROUPS]
    return pl.pallas_call(
        _proj_b_body,
        grid=(m // tm,),
        in_specs=[row(d), _const_spec((1, d)), _const_spec(w_in.shape), row(1),
                  _const_spec((1, LANES)), _const_spec((1, LANES))],
        out_specs=group_specs + [row(MEM_WIDTH)],
        out_shape=group_shapes + [jax.ShapeDtypeStruct((m, MEM_WIDTH), BF16)],
        scratch_shapes=[pltpu.VMEM((tm, d), BF16),
                        pltpu.VMEM((DIL_OUT_WIDTH // LANES, tm, LANES), F32)],
        compiler_params=_params("parallel"),
        name="proj_b",
    )(x, g_pre, w_in, positions.reshape(m, 1), freq, sign)


def _dilated_body(qkv_ref, o_ref, lse_ref, *, n_blocks, span, unroll):
    w = DIL_OUT_WIDTH
    qi = lax.broadcasted_iota(jnp.int32, (2 * span, 2 * span), 0) % span
    ki = lax.broadcasted_iota(jnp.int32, (2 * span, 2 * span), 1)
    band = ((ki < span) & (ki >= qi)) | ((ki >= span) & (ki - span <= qi))
    own_only = (lax.broadcasted_iota(jnp.int32, (2 * span, span), 1)
                <= lax.broadcasted_iota(jnp.int32, (2 * span, span), 0) % span)
    first = _first_head_lanes((span, PAIR))
    own_lanes = jnp.concatenate([first, ~first], axis=0)

    def block(r0, has_prev):
        own = pl.ds(r0, span)
        keys = pl.ds(r0 - span, 2 * span) if has_prev else own
        for r in range(qkv_ref.shape[1]):
            for p in range(w // PAIR):
                ql = slice(p * PAIR, (p + 1) * PAIR)
                q2 = _stack_heads(qkv_ref[0, r, own, ql], first)
                k2 = qkv_ref[0, r, keys, w + p * PAIR:w + (p + 1) * PAIR]
                v2 = qkv_ref[0, r, keys, 2 * w + p * PAIR:2 * w + (p + 1) * PAIR]
                s = jnp.where(band if has_prev else own_only, _dot_nt(q2, k2), NEG_INF)
                acc, lse = _softmax_pv(s, v2, with_lse=True)
                o_ref[0, r, own, ql] = jnp.where(first, acc[:span], acc[span:]).astype(BF16)
                lse = jnp.where(own_lanes, lse, 0.0)
                lse_ref[0, r, own, ql] = lse[:span] + lse[span:]

    block(0, False)

    def step(blk, carry):
        block(pl.multiple_of(blk * span, span), True)
        return carry

    lax.fori_loop(1, n_blocks, step, 0, unroll=unroll)


def _dilated_group(qkv, window, dilation):
    b, _, length, width = qkv.shape
    span = window // dilation
    assert length % span == 0
    per_step = min(dilation, DIL_CHAINS)
    unroll = DIL_CHAINS // per_step
    assert dilation % per_step == 0
    out_spec = pl.BlockSpec((1, per_step, length, DIL_OUT_WIDTH), lambda i, r: (i, r, 0, 0))
    return pl.pallas_call(
        functools.partial(_dilated_body, n_blocks=length // span, span=span, unroll=unroll),
        grid=(b, dilation // per_step),
        in_specs=[pl.BlockSpec((1, per_step, length, width), lambda i, r: (i, r, 0, 0))],
        out_specs=[out_spec, out_spec],
        out_shape=[jax.ShapeDtypeStruct((b, dilation, length, DIL_OUT_WIDTH), BF16),
                   jax.ShapeDtypeStruct((b, dilation, length, DIL_OUT_WIDTH), F32)],
        compiler_params=_params("parallel", "parallel"),
        name=f"dilated_{dilation}",
    )(qkv)


def _out_b_body(x_ref, o0_ref, o1_ref, o2_ref, l0_ref, l1_ref, l2_ref, qm_ref, kv_ref, w_ref,
                g_ref, o_ref, *scratch):
    tm = x_ref.shape[0]
    halves = DIL_OUT_WIDTH // LANES

    def to_token_major(ref, buf):
        dil = ref.shape[1]
        for r in range(dil):
            rows = ref[0, r].astype(F32)
            for j in range(halves):
                buf[j, pl.ds(r, tm // dil, stride=dil), :] = rows[:, j * LANES:(j + 1) * LANES]

    for g, (o_g, l_g) in enumerate(((o0_ref, l0_ref), (o1_ref, l1_ref), (o2_ref, l2_ref))):
        to_token_major(o_g, scratch[2 * g])
        to_token_major(l_g, scratch[2 * g + 1])

    for sub in range(tm // OUT_SUB):
        tile = slice(sub * OUT_SUB, (sub + 1) * OUT_SUB)
        read = lambda buf: jnp.concatenate([buf[j, tile, :] for j in range(halves)], axis=-1)
        outs = [read(scratch[2 * g]) for g in range(len(DILATED_GROUPS))]
        lses = [read(scratch[2 * g + 1]) for g in range(len(DILATED_GROUPS))]
        mx = jnp.maximum(jnp.maximum(lses[0], lses[1]), lses[2])
        es = [jnp.exp(l - mx) for l in lses]
        denom = es[0] + es[1] + es[2]
        merged = sum((e / denom) * o for e, o in zip(es, outs))
        mem_out = _mem_attention(qm_ref[tile, :], kv_ref).astype(BF16)
        y = (_dot(merged.astype(BF16), w_ref[:DIL_OUT_WIDTH, :])
             + _dot(mem_out, w_ref[DIL_OUT_WIDTH:, :]))
        o_ref[tile, :] = x_ref[tile, :] + _rms_norm(y, g_ref[...])


def _out_b(x, seq, outs, lses, qm, kv, w_out, g_post):
    m, d = x.shape
    tm = TM_OUT
    tiles_per_seq = seq // tm
    n_mem = kv.shape[1]
    row = lambda w: pl.BlockSpec((tm, w), lambda i: (i, 0))
    group_specs = [_residue_spec(tm, dil, DIL_OUT_WIDTH, tiles_per_seq)
                   for _, dil in DILATED_GROUPS]
    return pl.pallas_call(
        _out_b_body,
        grid=(m // tm,),
        in_specs=[row(d)] + group_specs * 2 + [
            row(MEM_WIDTH),
            pl.BlockSpec((1, n_mem, 2 * MEM_WIDTH), lambda i: (i // tiles_per_seq, 0, 0)),
            _const_spec(w_out.shape), _const_spec((1, d))],
        out_specs=row(d),
        out_shape=jax.ShapeDtypeStruct((m, d), F32),
        scratch_shapes=[pltpu.VMEM((DIL_OUT_WIDTH // LANES, tm, LANES), F32)]
        * (2 * len(DILATED_GROUPS)),
        compiler_params=_params("parallel"),
        name="out_b",
    )(x, *outs, *lses, qm, kv, w_out, g_post)


def _split3(x):
    hi = x.astype(BF16)
    r1 = x - hi.astype(F32)
    mid = r1.astype(BF16)
    lo = (r1 - mid.astype(F32)).astype(BF16)
    return hi, mid, lo


N_BIAS_TERMS = 3
AUG_WIDTH = N_MIX_HEADS * LANES
V_ROWS = HEAD_DIM + 16


def _pack_terms(hi, mid, lo):
    return (hi.astype(F32) + pltpu.roll(mid.astype(F32), N_MIX_HEADS, axis=1)
            + pltpu.roll(lo.astype(F32), 2 * N_MIX_HEADS, axis=1)).astype(BF16)


def _bias_selectors():
    rows = jnp.arange(LANES)
    term, head = rows // N_MIX_HEADS, rows % N_MIX_HEADS
    valid = rows < N_BIAS_TERMS * N_MIX_HEADS
    cols = jnp.arange(MIX_WIDTH)
    base = (head // 2) * PAIR + jnp.where(head % 2 == 0, HEAD_DIM, 0)

    def select(offset):
        target = base + offset + term
        return ((cols[None, :] == target[:, None]) & valid[:, None]).astype(BF16)

    def ones(offset):
        lane = cols % HEAD_DIM
        return ((lane >= offset) & (lane < offset + N_BIAS_TERMS)).astype(F32)[None, :]

    q_sel, k_sel = select(0), select(N_BIAS_TERMS)
    q_one, k_one = ones(N_BIAS_TERMS), ones(0)
    return q_sel, k_sel, q_one, k_one


def _proj_c_body(x_ref, g_ref, w_ref, wvt_ref, wf_ref, wqm_ref, fb_ref, qsel_ref, ksel_ref,
                 qone_ref, kone_ref, q_ref, k_ref, vt_ref, qm_ref, h_ref, carry_ref, qext_ref,
                 kext_ref, *, tiles_per_seq):
    tm = x_ref.shape[0]

    @pl.when(pl.program_id(0) % tiles_per_seq == 0)
    def _():
        carry_ref[...] = jnp.zeros_like(carry_ref)

    h_ref[...] = _rms_norm(x_ref[...], g_ref[...]).astype(BF16)

    z = _dot(h_ref[...], wf_ref[...]) + fb_ref[...]
    log_f = jnp.minimum(z, 0.0) - jnp.log1p(jnp.exp(-jnp.abs(z)))
    lane = lax.broadcasted_iota(jnp.int32, (tm, LANES), 1)
    log_f = jnp.where(lane < N_MIX_HEADS, log_f, 0.0)
    tri = (lax.broadcasted_iota(jnp.int32, (tm, tm), 1)
           <= lax.broadcasted_iota(jnp.int32, (tm, tm), 0)).astype(BF16)
    sums = _dot(tri, _pack_terms(*_split3(log_f)))
    local = (sums + pltpu.roll(sums, LANES - N_MIX_HEADS, axis=1)
             + pltpu.roll(sums, LANES - 2 * N_MIX_HEADS, axis=1))
    c = carry_ref[...] + jnp.where(lane < N_MIX_HEADS, local, 0.0)
    carry_ref[...] = c[tm - 1:tm, :]
    terms = _pack_terms(*_split3(c * LOG2_E))
    qext_ref[...] = _dot(terms, qsel_ref[...]) + qone_ref[...]
    kext_ref[...] = kone_ref[...] - _dot(terms, ksel_ref[...])

    first = _first_head_lanes((tm, PAIR))
    for c_idx in range(2 * MIX_WIDTH // MXU_N):
        lo = c_idx * MXU_N
        is_q = lo < MIX_WIDTH
        t = _dot(h_ref[...], w_ref[:, lo:lo + MXU_N])
        if is_q:
            t = t * (ATTN_SCALE * LOG2_E)
        dst, ext_ref = (q_ref, qext_ref) if is_q else (k_ref, kext_ref)
        for j in range(MXU_N // PAIR):
            p = (lo % MIX_WIDTH) // PAIR + j
            pair = t[:, j * PAIR:(j + 1) * PAIR]
            ext = ext_ref[:, p * PAIR:(p + 1) * PAIR]
            dst[:, 2 * p * LANES:(2 * p + 1) * LANES] = jnp.where(first, pair, ext).astype(BF16)
            dst[:, (2 * p + 1) * LANES:(2 * p + 2) * LANES] = pltpu.roll(
                jnp.where(first, ext, pair), HEAD_DIM, axis=1).astype(BF16)
    ones_row = (lax.broadcasted_iota(jnp.int32, (V_ROWS - HEAD_DIM, tm), 0) == 0).astype(BF16)
    for c_idx in range(MIX_WIDTH // MXU_N):
        lo = c_idx * MXU_N
        vt = _dot_nt(wvt_ref[lo:lo + MXU_N, :], h_ref[...]).astype(BF16)
        for j in range(MXU_N // HEAD_DIM):
            head = lo // HEAD_DIM + j
            vt_ref[0, head * V_ROWS:head * V_ROWS + HEAD_DIM, :] = vt[j * HEAD_DIM:(j + 1) * HEAD_DIM]
            vt_ref[0, head * V_ROWS + HEAD_DIM:(head + 1) * V_ROWS, :] = ones_row
    qm_ref[...] = (_dot(h_ref[...], wqm_ref[...]) * ATTN_SCALE).astype(BF16)


def _proj_c(x, seq, g_pre, w_in, forget_bias):
    m, d = x.shape
    tm = TK_FOX
    qkv_w = 3 * MIX_WIDTH
    w_v_t = w_in[:, 2 * MIX_WIDTH:qkv_w].T
    w_f = jnp.pad(w_in[:, qkv_w:qkv_w + N_MIX_HEADS], ((0, 0), (0, LANES - N_MIX_HEADS)))
    w_qm = w_in[:, qkv_w + N_MIX_HEADS:]
    fb = jnp.concatenate([forget_bias.astype(F32), jnp.zeros((LANES - N_MIX_HEADS,), F32)])
    consts = _bias_selectors()
    row = lambda w_: pl.BlockSpec((tm, w_), lambda i: (i, 0))
    return pl.pallas_call(
        functools.partial(_proj_c_body, tiles_per_seq=seq // tm),
        grid=(m // tm,),
        in_specs=[row(d), _const_spec((1, d)), _const_spec(w_in.shape), _const_spec(w_v_t.shape),
                  _const_spec(w_f.shape), _const_spec(w_qm.shape), _const_spec((1, LANES))]
        + [_const_spec(c.shape) for c in consts],
        out_specs=[row(AUG_WIDTH), row(AUG_WIDTH),
                   pl.BlockSpec((1, N_MIX_HEADS * V_ROWS, tm), lambda i: (i, 0, 0)),
                   row(MEM_WIDTH)],
        out_shape=[jax.ShapeDtypeStruct((m, AUG_WIDTH), BF16),
                   jax.ShapeDtypeStruct((m, AUG_WIDTH), BF16),
                   jax.ShapeDtypeStruct((m // tm, N_MIX_HEADS * V_ROWS, tm), BF16),
                   jax.ShapeDtypeStruct((m, MEM_WIDTH), BF16)],
        scratch_shapes=[pltpu.VMEM((tm, d), BF16), pltpu.VMEM((1, LANES), F32),
                        pltpu.VMEM((tm, MIX_WIDTH), F32), pltpu.VMEM((tm, MIX_WIDTH), F32)],
        compiler_params=_params("arbitrary"),
        name="proj_c",
    )(x, g_pre, w_in, w_v_t, w_f, w_qm, fb.reshape(1, LANES), *consts)


def _fox_body(q_ref, qnext_ref, k_ref, vt_ref, o_ref, acc_ref, s0_ref, s1_ref, s2_ref,
              mx0_ref, mx1_ref, mx2_ref, *, tq, n_q):
    i = pl.program_id(2)
    causal = (lax.broadcasted_iota(jnp.int32, (tq, tq), 1)
              >= lax.broadcasted_iota(jnp.int32, (tq, tq), 0))
    acc_ref[...] = jnp.zeros_like(acc_ref)

    def scores(queries_ref, j, buf, diagonal):
        s_ref, mx_ref = buf
        rows = pl.ds(pl.multiple_of(j * tq, tq), tq)
        for h in range(FOX_HEADS):
            lanes = slice(h * LANES, (h + 1) * LANES)
            st = _dot_nt(k_ref[0, rows, lanes], queries_ref[0, :, lanes])
            if diagonal:
                st = jnp.where(causal, st, NEG_INF)
            s_ref[h] = st
            mx_ref[h] = jnp.max(st, axis=0, keepdims=True)

    def consume(j, buf, m_run):
        s_ref, mx_ref = buf
        m_out = []
        for h in range(FOX_HEADS):
            m_new = jnp.maximum(m_run[h], mx_ref[h])
            alpha = jnp.exp2(m_run[h] - m_new)
            e = jnp.exp2(s_ref[h] - m_new).astype(BF16)
            acc_ref[h] = alpha * acc_ref[h] + _dot(vt_ref[j, h * V_ROWS:(h + 1) * V_ROWS, :], e)
            m_out.append(m_new)
        return tuple(m_out)

    def run(diag, other, handover):
        @pl.when(i == 0)
        def _():
            scores(q_ref, i, diag, True)

        def prefetch_next_tile():
            scores(qnext_ref, jnp.minimum(i + 1, n_q - 1), handover, True)

        def two_tiles(t, m_run):
            j = i - 2 * t
            scores(q_ref, j - 1, other, False)
            m_run = consume(j, diag, m_run)
            scores(q_ref, j - 2, diag, False)
            return consume(j - 1, other, m_run)

        def last_two(m_run):
            scores(q_ref, 0, other, False)
            m_run = consume(1, diag, m_run)
            prefetch_next_tile()
            return consume(0, other, m_run)

        def last_one(m_run):
            prefetch_next_tile()
            return consume(0, diag, m_run)

        m_run = tuple(jnp.full((1, tq), NEG_INF, F32) for _ in range(FOX_HEADS))
        m_run = lax.fori_loop(0, i // 2, two_tiles, m_run)
        lax.cond(i % 2 == 1, last_two, last_one, m_run)

    bufs = ((s0_ref, mx0_ref), (s1_ref, mx1_ref), (s2_ref, mx2_ref))
    lax.cond(i % 2 == 0,
             lambda: run(bufs[0], bufs[1], bufs[2]),
             lambda: run(bufs[2], bufs[1], bufs[0]))
    for p in range(FOX_HEADS // 2):
        halves = []
        for h in (2 * p, 2 * p + 1):
            halves.append(acc_ref[h, :HEAD_DIM, :] / acc_ref[h, HEAD_DIM:HEAD_DIM + 1, :])
        o_ref[0, :, p * PAIR:(p + 1) * PAIR] = jnp.concatenate(halves, axis=0).T.astype(BF16)


def _fox_attention(q_aug, k_aug, v_t, batch, seq):
    tq = tk = TK_FOX
    q3 = q_aug.reshape(batch, seq, AUG_WIDTH)
    k3 = k_aug.reshape(batch, seq, AUG_WIDTH)
    aug_w = FOX_HEADS * LANES
    v_w = FOX_HEADS * HEAD_DIM
    score_buf = pltpu.VMEM((FOX_HEADS, tk, tq), F32)
    max_buf = pltpu.VMEM((FOX_HEADS, 1, tq), F32)
    n_q = seq // tq
    out = pl.pallas_call(
        functools.partial(_fox_body, tq=tq, n_q=n_q),
        grid=(batch, N_MIX_HEADS // FOX_HEADS, n_q),
        in_specs=[pl.BlockSpec((1, tq, aug_w), lambda b, p, i: (b, i, p)),
                  pl.BlockSpec((1, tq, aug_w), lambda b, p, i: (b, jnp.minimum(i + 1, n_q - 1), p)),
                  pl.BlockSpec((1, seq, aug_w), lambda b, p, i: (b, 0, p)),
                  pl.BlockSpec((seq // tk, FOX_HEADS * V_ROWS, tk), lambda b, p, i: (b, p, 0))],
        out_specs=pl.BlockSpec((1, tq, v_w), lambda b, p, i: (b, i, p)),
        out_shape=jax.ShapeDtypeStruct((batch, seq, MIX_WIDTH), BF16),
        scratch_shapes=[pltpu.VMEM((FOX_HEADS, V_ROWS, tq), F32)] + [score_buf] * 3
        + [max_buf] * 3,
        compiler_params=_params("parallel", "arbitrary", "arbitrary"),
        name="fox_attention",
    )(q3, q3, k3, v_t)
    return out.reshape(batch * seq, MIX_WIDTH)


def _out_c_body(x_ref, att_ref, qm_ref, kv_ref, w_ref, g_ref, o_ref):
    for sub in range(x_ref.shape[0] // OUT_SUB):
        tile = slice(sub * OUT_SUB, (sub + 1) * OUT_SUB)
        mem_out = _mem_attention(qm_ref[tile, :], kv_ref).astype(BF16)
        y = _dot(att_ref[tile, :], w_ref[:MIX_WIDTH, :]) + _dot(mem_out, w_ref[MIX_WIDTH:, :])
        o_ref[tile, :] = x_ref[tile, :] + _rms_norm(y, g_ref[...])


def _out_c(x, seq, att, qm, kv, w_out, g_post):
    m, d = x.shape
    tm = TM_OUT
    tiles_per_seq = seq // tm
    n_mem = kv.shape[1]
    row = lambda w: pl.BlockSpec((tm, w), lambda i: (i, 0))
    return pl.pallas_call(
        _out_c_body,
        grid=(m // tm,),
        in_specs=[row(d), row(MIX_WIDTH), row(MEM_WIDTH),
                  pl.BlockSpec((1, n_mem, 2 * MEM_WIDTH), lambda i: (i // tiles_per_seq, 0, 0)),
                  _const_spec(w_out.shape), _const_spec((1, d))],
        out_specs=row(d),
        out_shape=jax.ShapeDtypeStruct((m, d), F32),
        compiler_params=_params("parallel"),
        name="out_c",
    )(x, att, qm, kv, w_out, g_post)


def kernel(x, mem, positions, norm_g, mem_norm_g, w_mem_kv, ffn_w_gate_up, ffn_w_down,
           a_w_in, a_spatial_w, a_spatial_b, a_v_norm_g, a_w_out,
           b_w_in, b_w_out, c_w_in, c_forget_bias, c_w_out):
    batch, seq, d = x.shape
    depth = norm_g.shape[0]
    bf = lambda w: w.astype(BF16)

    kv_all = _mem_kv(mem, mem_norm_g, bf(w_mem_kv))
    w_gate_up, w_down = bf(ffn_w_gate_up), bf(ffn_w_down)
    xf = x.reshape(batch * seq, d)
    for i in range(depth):
        kind, j = i % 3, i // 3
        g = norm_g[i].reshape(norm_g.shape[1], 1, d)
        xf = _ffn(xf, g[0], g[1], w_gate_up, w_down, (i, 0))
        kv = kv_all[i]
        if kind == 0:
            xf = _mixer_a(xf, seq, g[2], g[3], kv, bf(a_w_in[j]), bf(a_spatial_w[j]),
                          a_spatial_b[j], a_v_norm_g[j], bf(a_w_out[j]))
        elif kind == 1:
            *groups, qm = _proj_b(xf, batch, seq, g[2], bf(b_w_in[j]), positions)
            outs, lses = zip(*[_dilated_group(qkv_g, w, dil)
                               for qkv_g, (w, dil) in zip(groups, DILATED_GROUPS)])
            xf = _out_b(xf, seq, outs, lses, qm, kv, bf(b_w_out[j]), g[3])
        else:
            q_aug, k_aug, v_t, qm = _proj_c(xf, seq, g[2], bf(c_w_in[j]), c_forget_bias[j])
            att = _fox_attention(q_aug, k_aug, v_t, batch, seq)
            xf = _out_c(xf, seq, att, qm, kv, bf(c_w_out[j]), g[3])
        xf = _ffn(xf, g[4], g[5], w_gate_up, w_down, (i, 1))
    return xf.reshape(batch, seq, d)
```

```python
import functools

import jax
import jax.numpy as jnp
from jax import lax
from jax.experimental import pallas as pl
from jax.experimental.pallas import tpu as pltpu

F32 = jnp.float32
BF16 = jnp.bfloat16

HEAD_DIM = 64
N_MIX_HEADS = 12
MIX_WIDTH = N_MIX_HEADS * HEAD_DIM
N_MEM_HEADS = 4
MEM_WIDTH = N_MEM_HEADS * HEAD_DIM
CHUNK = 128
ROPE_THETA = 10000.0
DILATED_GROUPS = ((128, 1), (512, 4), (2048, 16))
DIL_OUT_WIDTH = (N_MIX_HEADS // len(DILATED_GROUPS)) * HEAD_DIM
RMS_EPS = 1e-6
LN_EPS = 1e-5
NEG_INF = -1e30
ATTN_SCALE = HEAD_DIM ** -0.5
LOG2_E = 1.4426950408889634
Q_SCALE = ATTN_SCALE * LOG2_E

LANES = 128
PAIR = 2 * HEAD_DIM
MXU_N = 256
VMEM_LIMIT_BYTES = 56 * 1024 * 1024

TM_FFN = 1024
FFN_SUB = 512
TM_PROJ = 512
TM_MIX_A = 1024
MIX_A_SUB = 512
TM_OUT = 1024
OUT_SUB = 512
TK_FOX = 512
FOX_HEADS = 4
DIL_CHAINS = 4
FF_CHUNK = 256


def _params(*semantics):
    return pltpu.CompilerParams(dimension_semantics=semantics,
                                vmem_limit_bytes=VMEM_LIMIT_BYTES)


def _const_spec(shape):
    zeros = (0,) * len(shape)
    return pl.BlockSpec(shape, lambda *_: zeros, pipeline_mode=pl.Buffered(1))


def _rms_norm(x, g):
    ms = jnp.mean(x * x, axis=-1, keepdims=True)
    return x * lax.rsqrt(ms + RMS_EPS) * g


def _dot(a, b):
    return jnp.dot(a, b, preferred_element_type=F32)


def _dot_nt(a, b):
    return lax.dot_general(a, b, (((1,), (1,)), ((), ())), preferred_element_type=F32)


def _first_head_lanes(shape):
    return lax.broadcasted_iota(jnp.int32, shape, len(shape) - 1) % PAIR < HEAD_DIM


def _ffn_body(x_ref, gpre_ref, gpost_ref, wgu_ref, wd_ref, o_ref, xn_ref, act_ref, *, d_ff):
    for sub in range(x_ref.shape[0] // FFN_SUB):
        rows = slice(sub * FFN_SUB, (sub + 1) * FFN_SUB)
        xn_ref[rows, :] = _rms_norm(x_ref[rows, :], gpre_ref[...]).astype(BF16)
        for c in range(d_ff // FF_CHUNK):
            lo = c * FF_CHUNK
            gate = _dot(xn_ref[rows, :], wgu_ref[:, lo:lo + FF_CHUNK])
            up = _dot(xn_ref[rows, :], wgu_ref[:, d_ff + lo:d_ff + lo + FF_CHUNK])
            act_ref[rows, lo:lo + FF_CHUNK] = (gate * jax.nn.sigmoid(gate) * up).astype(BF16)
        y = _dot(act_ref[rows, :], wd_ref[...])
        o_ref[rows, :] = x_ref[rows, :] + 0.5 * _rms_norm(y, gpost_ref[...])


def _stacked_spec(stacked, index):
    k = len(index)
    tail = stacked.shape[k:]
    where = tuple(index) + (0,) * len(tail)
    return pl.BlockSpec((None,) * k + tail, lambda *_: where, pipeline_mode=pl.Buffered(1))


def _ffn(x, g_pre, g_post, w_gate_up, w_down, index):
    m, d = x.shape
    d_ff = w_down.shape[-2]
    assert m % TM_FFN == 0 and d_ff % FF_CHUNK == 0
    row = pl.BlockSpec((TM_FFN, d), lambda i: (i, 0))
    return pl.pallas_call(
        functools.partial(_ffn_body, d_ff=d_ff),
        grid=(m // TM_FFN,),
        in_specs=[row, _const_spec((1, d)), _const_spec((1, d)),
                  _stacked_spec(w_gate_up, index), _stacked_spec(w_down, index)],
        out_specs=row,
        out_shape=jax.ShapeDtypeStruct((m, d), F32),
        scratch_shapes=[pltpu.VMEM((TM_FFN, d), BF16), pltpu.VMEM((TM_FFN, d_ff), BF16)],
        compiler_params=_params("parallel"),
        name="ffn",
    )(x, g_pre, g_post, w_gate_up, w_down)


def _mem_kv_body(mem_ref, g_ref, w_ref, o_ref):
    for i in range(mem_ref.shape[0]):
        o_ref[0, i] = _dot(_rms_norm(mem_ref[i], g_ref[0]).astype(BF16), w_ref[0]).astype(BF16)


def _mem_kv(mem, mem_norm_g, w_mem_kv):
    b, n_mem, d = mem.shape
    depth = w_mem_kv.shape[0]
    return pl.pallas_call(
        _mem_kv_body,
        grid=(depth,),
        in_specs=[_const_spec(mem.shape),
                  pl.BlockSpec((1, 1, d), lambda l: (l, 0, 0)),
                  pl.BlockSpec((1, d, 2 * MEM_WIDTH), lambda l: (l, 0, 0))],
        out_specs=pl.BlockSpec((1, b, n_mem, 2 * MEM_WIDTH), lambda l: (l, 0, 0, 0)),
        out_shape=jax.ShapeDtypeStruct((depth, b, n_mem, 2 * MEM_WIDTH), BF16),
        compiler_params=_params("parallel"),
        name="mem_kv",
    )(mem, mem_norm_g.reshape(depth, 1, d), w_mem_kv)


def _stack_heads(qp, first):
    zero = jnp.zeros_like(qp)
    return jnp.concatenate([jnp.where(first, qp, zero), jnp.where(first, zero, qp)], axis=0)


def _softmax_pv(s, v, with_lse=False):
    mx = jnp.max(s, axis=-1, keepdims=True)
    e = jnp.exp2(s - mx)
    denom = jnp.sum(e, axis=-1, keepdims=True)
    out = _dot(e.astype(BF16), v) / denom
    if with_lse:
        return out, mx + jnp.log2(denom)
    return out


def _mem_attention(qm, kv_ref):
    tm = qm.shape[0]
    outs = []
    for p in range(MEM_WIDTH // PAIR):
        qp = qm[:, p * PAIR:(p + 1) * PAIR]
        k = kv_ref[0, :, p * PAIR:(p + 1) * PAIR]
        v = kv_ref[0, :, MEM_WIDTH + p * PAIR:MEM_WIDTH + (p + 1) * PAIR]
        first = _first_head_lanes(qp.shape)
        acc = _softmax_pv(_dot_nt(_stack_heads(qp, first), k), v)
        outs.append(jnp.where(first, acc[:tm], acc[tm:]))
    return jnp.concatenate(outs, axis=-1)


def _mixer_a_body(x_ref, gpre_ref, win_ref, vg_ref, sw_ref, sb_ref, kv_ref, wout_ref, gpost_ref,
                  o_ref, u_ref, v_ref, gated_ref, wc_ref):
    t_idx = lax.broadcasted_iota(jnp.int32, (CHUNK, CHUNK), 0)
    s_idx = lax.broadcasted_iota(jnp.int32, (CHUNK, CHUNK), 1)
    for g in range(N_MIX_HEADS):
        wc_ref[g] = jnp.where(s_idx <= t_idx, sw_ref[g], jnp.zeros((CHUNK, CHUNK), BF16))
    first = _first_head_lanes((CHUNK, PAIR))

    for sub in range(x_ref.shape[0] // MIX_A_SUB):
        tile = slice(sub * MIX_A_SUB, (sub + 1) * MIX_A_SUB)
        h = _rms_norm(x_ref[tile, :], gpre_ref[...]).astype(BF16)
        u_ref[tile, :] = jax.nn.gelu(_dot(h, win_ref[:, :MIX_WIDTH]))
        v = jax.nn.gelu(_dot(h, win_ref[:, MIX_WIDTH:2 * MIX_WIDTH]))
        mu = jnp.mean(v, axis=-1, keepdims=True)
        vc = v - mu
        var = jnp.mean(vc * vc, axis=-1, keepdims=True)
        v_ref[tile, :] = (vc * lax.rsqrt(var + LN_EPS) * vg_ref[...]).astype(BF16)
        qm = (_dot(h, win_ref[:, 2 * MIX_WIDTH:]) * Q_SCALE).astype(BF16)

        for c in range(sub * MIX_A_SUB // CHUNK, (sub + 1) * MIX_A_SUB // CHUNK):
            rows = slice(c * CHUNK, (c + 1) * CHUNK)
            for p in range(MIX_WIDTH // PAIR):
                lanes = slice(p * PAIR, (p + 1) * PAIR)
                vp = v_ref[rows, lanes]
                mixed = jnp.where(first, _dot(wc_ref[2 * p], vp), _dot(wc_ref[2 * p + 1], vp))
                gated_ref[rows, lanes] = (u_ref[rows, lanes]
                                          * (mixed + sb_ref[:, lanes])).astype(BF16)

        mem_out = _mem_attention(qm, kv_ref).astype(BF16)
        y = (_dot(gated_ref[tile, :], wout_ref[:MIX_WIDTH, :])
             + _dot(mem_out, wout_ref[MIX_WIDTH:, :]))
        o_ref[tile, :] = x_ref[tile, :] + _rms_norm(y, gpost_ref[...])


def _mixer_a(x, seq, g_pre, g_post, kv, w_in, spatial_w, spatial_b, v_norm_g, w_out):
    m, d = x.shape
    tm = TM_MIX_A
    assert seq % tm == 0 and tm % CHUNK == 0
    tiles_per_seq = seq // tm
    n_mem = kv.shape[1]
    row = pl.BlockSpec((tm, d), lambda i: (i, 0))
    bias = jnp.repeat(spatial_b.T, HEAD_DIM, axis=1)
    return pl.pallas_call(
        _mixer_a_body,
        grid=(m // tm,),
        in_specs=[row, _const_spec((1, d)), _const_spec(w_in.shape), _const_spec((1, MIX_WIDTH)),
                  _const_spec(spatial_w.shape), _const_spec(bias.shape),
                  pl.BlockSpec((1, n_mem, 2 * MEM_WIDTH), lambda i: (i // tiles_per_seq, 0, 0)),
                  _const_spec(w_out.shape), _const_spec((1, d))],
        out_specs=row,
        out_shape=jax.ShapeDtypeStruct((m, d), F32),
        scratch_shapes=[pltpu.VMEM((tm, MIX_WIDTH), F32), pltpu.VMEM((tm, MIX_WIDTH), BF16),
                        pltpu.VMEM((tm, MIX_WIDTH), BF16),
                        pltpu.VMEM((N_MIX_HEADS, CHUNK, CHUNK), BF16)],
        compiler_params=_params("parallel"),
        name="mixer_a",
    )(x, g_pre, w_in, v_norm_g.reshape(1, MIX_WIDTH), spatial_w, bias, kv, w_out, g_post)


def _rope_constants():
    inv_freq = ROPE_THETA ** (-jnp.arange(0, HEAD_DIM, 2, dtype=F32) / HEAD_DIM)
    freq = jnp.tile(inv_freq, LANES // (HEAD_DIM // 2)).reshape(1, LANES)
    half = jnp.concatenate([-jnp.ones((HEAD_DIM // 2,), F32), jnp.ones((HEAD_DIM // 2,), F32)])
    sign = jnp.tile(half, LANES // HEAD_DIM).reshape(1, LANES)
    return freq, sign


def _rope(t, cos, sin_signed):
    first_half = lax.broadcasted_iota(jnp.int32, t.shape, 1) % HEAD_DIM < HEAD_DIM // 2
    swapped = jnp.where(first_half,
                        pltpu.roll(t, LANES - HEAD_DIM // 2, axis=1),
                        pltpu.roll(t, HEAD_DIM // 2, axis=1))
    return t * cos + swapped * sin_signed


def _residue_spec(tm, dilation, width, tiles_per_seq):
    return pl.BlockSpec((1, dilation, tm // dilation, width),
                        lambda i: (i // tiles_per_seq, 0, i % tiles_per_seq, 0))


def _proj_b_body(x_ref, g_ref, w_ref, pos_ref, freq_ref, sign_ref, g0_ref, g1_ref, g2_ref, qm_ref,
                 h_ref, t_ref):
    tm = x_ref.shape[0]
    h_ref[...] = _rms_norm(x_ref[...], g_ref[...]).astype(BF16)
    angle = pos_ref[...].astype(F32) * freq_ref[...]
    cos = jnp.cos(angle)
    sin = jnp.sin(angle) * sign_ref[...]
    outs = (g0_ref, g1_ref, g2_ref)
    n_groups = len(DILATED_GROUPS)
    for c in range(3 * n_groups):
        kind, grp = divmod(c, n_groups)
        lo = c * DIL_OUT_WIDTH
        t = _dot(h_ref[...], w_ref[:, lo:lo + DIL_OUT_WIDTH])
        if kind < 2:
            scale = Q_SCALE if kind == 0 else 1.0
            t = jnp.concatenate(
                [_rope(t[:, j * LANES:(j + 1) * LANES], cos, sin)
                 for j in range(DIL_OUT_WIDTH // LANES)], axis=-1) * scale
        dil = DILATED_GROUPS[grp][1]
        cols = slice(kind * DIL_OUT_WIDTH, (kind + 1) * DIL_OUT_WIDTH)
        if dil == 1:
            outs[grp][0, 0, :, cols] = t.astype(BF16)
        else:
            halves = DIL_OUT_WIDTH // LANES
            for j in range(halves):
                t_ref[j] = t[:, j * LANES:(j + 1) * LANES]
            for r in range(dil):
                outs[grp][0, r, :, cols] = jnp.concatenate(
                    [t_ref[j, pl.ds(r, tm // dil, stride=dil), :] for j in range(halves)],
                    axis=-1).astype(BF16)
    qm_ref[...] = (_dot(h_ref[...], w_ref[:, 3 * MIX_WIDTH:]) * Q_SCALE).astype(BF16)


def _proj_b(x, batch, seq, g_pre, w_in, positions):
    m, d = x.shape
    tm = TM_PROJ
    tiles_per_seq = seq // tm
    freq, sign = _rope_constants()
    row = lambda w: pl.BlockSpec((tm, w), lambda i: (i, 0))
    width = 3 * DIL_OUT_WIDTH
    group_specs = [_residue_spec(tm, dil, width, tiles_per_seq) for _, dil in DILATED_GROUPS]
    group_shapes = [jax.ShapeDtypeStruct((batch, dil, seq // dil, width), BF16)
                    for _, dil in DILATED_GROUPS]
    return pl.pallas_call(
        _proj_b_body,
        grid=(m // tm,),
        in_specs=[row(d), _const_spec((1, d)), _const_spec(w_in.shape), row(1),
                  _const_spec((1, LANES)), _const_spec((1, LANES))],
        out_specs=group_specs + [row(MEM_WIDTH)],
        out_shape=group_shapes + [jax.ShapeDtypeStruct((m, MEM_WIDTH), BF16)],
        scratch_shapes=[pltpu.VMEM((tm, d), BF16),
                        pltpu.VMEM((DIL_OUT_WIDTH // LANES, tm, LANES), F32)],
        compiler_params=_params("parallel"),
        name="proj_b",
    )(x, g_pre, w_in, positions.reshape(m, 1), freq, sign)


def _dilated_body(qkv_ref, o_ref, lse_ref, *, n_blocks, span, unroll):
    w = DIL_OUT_WIDTH
    qi = lax.broadcasted_iota(jnp.int32, (2 * span, 2 * span), 0) % span
    ki = lax.broadcasted_iota(jnp.int32, (2 * span, 2 * span), 1)
    band = ((ki < span) & (ki >= qi)) | ((ki >= span) & (ki - span <= qi))
    own_only = (lax.broadcasted_iota(jnp.int32, (2 * span, span), 1)
                <= lax.broadcasted_iota(jnp.int32, (2 * span, span), 0) % span)
    first = _first_head_lanes((span, PAIR))
    own_lanes = jnp.concatenate([first, ~first], axis=0)

    def block(r0, has_prev):
        own = pl.ds(r0, span)
        keys = pl.ds(r0 - span, 2 * span) if has_prev else own
        for r in range(qkv_ref.shape[1]):
            for p in range(w // PAIR):
                ql = slice(p * PAIR, (p + 1) * PAIR)
                q2 = _stack_heads(qkv_ref[0, r, own, ql], first)
                k2 = qkv_ref[0, r, keys, w + p * PAIR:w + (p + 1) * PAIR]
                v2 = qkv_ref[0, r, keys, 2 * w + p * PAIR:2 * w + (p + 1) * PAIR]
                s = jnp.where(band if has_prev else own_only, _dot_nt(q2, k2), NEG_INF)
                acc, lse = _softmax_pv(s, v2, with_lse=True)
                o_ref[0, r, own, ql] = jnp.where(first, acc[:span], acc[span:]).astype(BF16)
                lse = jnp.where(own_lanes, lse, 0.0)
                lse_ref[0, r, own, ql] = lse[:span] + lse[span:]

    block(0, False)

    def step(blk, carry):
        block(pl.multiple_of(blk * span, span), True)
        return carry

    lax.fori_loop(1, n_blocks, step, 0, unroll=unroll)


def _dilated_group(qkv, window, dilation):
    b, _, length, width = qkv.shape
    span = window // dilation
    assert length % span == 0
    per_step = min(dilation, DIL_CHAINS)
    unroll = DIL_CHAINS // per_step
    assert dilation % per_step == 0
    out_spec = pl.BlockSpec((1, per_step, length, DIL_OUT_WIDTH), lambda i, r: (i, r, 0, 0))
    return pl.pallas_call(
        functools.partial(_dilated_body, n_blocks=length // span, span=span, unroll=unroll),
        grid=(b, dilation // per_step),
        in_specs=[pl.BlockSpec((1, per_step, length, width), lambda i, r: (i, r, 0, 0))],
        out_specs=[out_spec, out_spec],
        out_shape=[jax.ShapeDtypeStruct((b, dilation, length, DIL_OUT_WIDTH), BF16),
                   jax.ShapeDtypeStruct((b, dilation, length, DIL_OUT_WIDTH), F32)],
        compiler_params=_params("parallel", "parallel"),
        name=f"dilated_{dilation}",
    )(qkv)


def _out_b_body(x_ref, o0_ref, o1_ref, o2_ref, l0_ref, l1_ref, l2_ref, qm_ref, kv_ref, w_ref,
                g_ref, o_ref, *scratch):
    tm = x_ref.shape[0]
    halves = DIL_OUT_WIDTH // LANES

    def to_token_major(ref, buf):
        dil = ref.shape[1]
        for r in range(dil):
            rows = ref[0, r].astype(F32)
            for j in range(halves):
                buf[j, pl.ds(r, tm // dil, stride=dil), :] = rows[:, j * LANES:(j + 1) * LANES]

    for g, (o_g, l_g) in enumerate(((o0_ref, l0_ref), (o1_ref, l1_ref), (o2_ref, l2_ref))):
        to_token_major(o_g, scratch[2 * g])
        to_token_major(l_g, scratch[2 * g + 1])

    for sub in range(tm // OUT_SUB):
        tile = slice(sub * OUT_SUB, (sub + 1) * OUT_SUB)
        read = lambda buf: jnp.concatenate([buf[j, tile, :] for j in range(halves)], axis=-1)
        outs = [read(scratch[2 * g]) for g in range(len(DILATED_GROUPS))]
        lses = [read(scratch[2 * g + 1]) for g in range(len(DILATED_GROUPS))]
        mx = jnp.maximum(jnp.maximum(lses[0], lses[1]), lses[2])
        es = [jnp.exp2(l - mx) for l in lses]
        denom = es[0] + es[1] + es[2]
        merged = sum((e / denom) * o for e, o in zip(es, outs))
        mem_out = _mem_attention(qm_ref[tile, :], kv_ref).astype(BF16)
        y = (_dot(merged.astype(BF16), w_ref[:DIL_OUT_WIDTH, :])
             + _dot(mem_out, w_ref[DIL_OUT_WIDTH:, :]))
        o_ref[tile, :] = x_ref[tile, :] + _rms_norm(y, g_ref[...])


def _out_b(x, seq, outs, lses, qm, kv, w_out, g_post):
    m, d = x.shape
    tm = TM_OUT
    tiles_per_seq = seq // tm
    n_mem = kv.shape[1]
    row = lambda w: pl.BlockSpec((tm, w), lambda i: (i, 0))
    group_specs = [_residue_spec(tm, dil, DIL_OUT_WIDTH, tiles_per_seq)
                   for _, dil in DILATED_GROUPS]
    return pl.pallas_call(
        _out_b_body,
        grid=(m // tm,),
        in_specs=[row(d)] + group_specs * 2 + [
            row(MEM_WIDTH),
            pl.BlockSpec((1, n_mem, 2 * MEM_WIDTH), lambda i: (i // tiles_per_seq, 0, 0)),
            _const_spec(w_out.shape), _const_spec((1, d))],
        out_specs=row(d),
        out_shape=jax.ShapeDtypeStruct((m, d), F32),
        scratch_shapes=[pltpu.VMEM((DIL_OUT_WIDTH // LANES, tm, LANES), F32)]
        * (2 * len(DILATED_GROUPS)),
        compiler_params=_params("parallel"),
        name="out_b",
    )(x, *outs, *lses, qm, kv, w_out, g_post)


def _split3(x):
    hi = x.astype(BF16)
    r1 = x - hi.astype(F32)
    mid = r1.astype(BF16)
    lo = (r1 - mid.astype(F32)).astype(BF16)
    return hi, mid, lo


N_BIAS_TERMS = 3
AUG_WIDTH = N_MIX_HEADS * LANES
V_ROWS = HEAD_DIM + 16


def _pack_terms(hi, mid, lo):
    return (hi.astype(F32) + pltpu.roll(mid.astype(F32), N_MIX_HEADS, axis=1)
            + pltpu.roll(lo.astype(F32), 2 * N_MIX_HEADS, axis=1)).astype(BF16)


def _bias_selectors():
    rows = jnp.arange(LANES)
    term, head = rows // N_MIX_HEADS, rows % N_MIX_HEADS
    valid = rows < N_BIAS_TERMS * N_MIX_HEADS
    cols = jnp.arange(MIX_WIDTH)
    base = (head // 2) * PAIR + jnp.where(head % 2 == 0, HEAD_DIM, 0)

    def select(offset):
        target = base + offset + term
        return ((cols[None, :] == target[:, None]) & valid[:, None]).astype(BF16)

    def ones(offset):
        lane = cols % HEAD_DIM
        return ((lane >= offset) & (lane < offset + N_BIAS_TERMS)).astype(F32)[None, :]

    q_sel, k_sel = select(0), select(N_BIAS_TERMS)
    q_one, k_one = ones(N_BIAS_TERMS), ones(0)
    return q_sel, k_sel, q_one, k_one


def _proj_c_body(x_ref, g_ref, w_ref, wvt_ref, wf_ref, wqm_ref, fb_ref, qsel_ref, ksel_ref,
                 qone_ref, kone_ref, q_ref, k_ref, vt_ref, qm_ref, h_ref, carry_ref, qext_ref,
                 kext_ref, *, tiles_per_seq):
    tm = x_ref.shape[0]

    @pl.when(pl.program_id(0) % tiles_per_seq == 0)
    def _():
        carry_ref[...] = jnp.zeros_like(carry_ref)

    h_ref[...] = _rms_norm(x_ref[...], g_ref[...]).astype(BF16)

    z = _dot(h_ref[...], wf_ref[...]) + fb_ref[...]
    log_f = jnp.minimum(z, 0.0) - jnp.log1p(jnp.exp(-jnp.abs(z)))
    lane = lax.broadcasted_iota(jnp.int32, (tm, LANES), 1)
    log_f = jnp.where(lane < N_MIX_HEADS, log_f, 0.0)
    tri = (lax.broadcasted_iota(jnp.int32, (tm, tm), 1)
           <= lax.broadcasted_iota(jnp.int32, (tm, tm), 0)).astype(BF16)
    sums = _dot(tri, _pack_terms(*_split3(log_f)))
    local = (sums + pltpu.roll(sums, LANES - N_MIX_HEADS, axis=1)
             + pltpu.roll(sums, LANES - 2 * N_MIX_HEADS, axis=1))
    c = carry_ref[...] + jnp.where(lane < N_MIX_HEADS, local, 0.0)
    carry_ref[...] = c[tm - 1:tm, :]
    terms = _pack_terms(*_split3(c * LOG2_E))
    qext_ref[...] = _dot(terms, qsel_ref[...]) + qone_ref[...]
    kext_ref[...] = kone_ref[...] - _dot(terms, ksel_ref[...])

    first = _first_head_lanes((tm, PAIR))
    for c_idx in range(2 * MIX_WIDTH // MXU_N):
        lo = c_idx * MXU_N
        is_q = lo < MIX_WIDTH
        t = _dot(h_ref[...], w_ref[:, lo:lo + MXU_N])
        if is_q:
            t = t * Q_SCALE
        dst, ext_ref = (q_ref, qext_ref) if is_q else (k_ref, kext_ref)
        for j in range(MXU_N // PAIR):
            p = (lo % MIX_WIDTH) // PAIR + j
            pair = t[:, j * PAIR:(j + 1) * PAIR]
            ext = ext_ref[:, p * PAIR:(p + 1) * PAIR]
            dst[:, 2 * p * LANES:(2 * p + 1) * LANES] = jnp.where(first, pair, ext).astype(BF16)
            dst[:, (2 * p + 1) * LANES:(2 * p + 2) * LANES] = pltpu.roll(
                jnp.where(first, ext, pair), HEAD_DIM, axis=1).astype(BF16)
    ones_row = (lax.broadcasted_iota(jnp.int32, (V_ROWS - HEAD_DIM, tm), 0) == 0).astype(BF16)
    for c_idx in range(MIX_WIDTH // MXU_N):
        lo = c_idx * MXU_N
        vt = _dot_nt(wvt_ref[lo:lo + MXU_N, :], h_ref[...]).astype(BF16)
        for j in range(MXU_N // HEAD_DIM):
            head = lo // HEAD_DIM + j
            vt_ref[0, head * V_ROWS:head * V_ROWS + HEAD_DIM, :] = vt[j * HEAD_DIM:(j + 1) * HEAD_DIM]
            vt_ref[0, head * V_ROWS + HEAD_DIM:(head + 1) * V_ROWS, :] = ones_row
    qm_ref[...] = (_dot(h_ref[...], wqm_ref[...]) * Q_SCALE).astype(BF16)


def _proj_c(x, seq, g_pre, w_in, forget_bias):
    m, d = x.shape
    tm = TK_FOX
    qkv_w = 3 * MIX_WIDTH
    w_v_t = w_in[:, 2 * MIX_WIDTH:qkv_w].T
    w_f = jnp.pad(w_in[:, qkv_w:qkv_w + N_MIX_HEADS], ((0, 0), (0, LANES - N_MIX_HEADS)))
    w_qm = w_in[:, qkv_w + N_MIX_HEADS:]
    fb = jnp.concatenate([forget_bias.astype(F32), jnp.zeros((LANES - N_MIX_HEADS,), F32)])
    consts = _bias_selectors()
    row = lambda w_: pl.BlockSpec((tm, w_), lambda i: (i, 0))
    return pl.pallas_call(
        functools.partial(_proj_c_body, tiles_per_seq=seq // tm),
        grid=(m // tm,),
        in_specs=[row(d), _const_spec((1, d)), _const_spec(w_in.shape), _const_spec(w_v_t.shape),
                  _const_spec(w_f.shape), _const_spec(w_qm.shape), _const_spec((1, LANES))]
        + [_const_spec(c.shape) for c in consts],
        out_specs=[row(AUG_WIDTH), row(AUG_WIDTH),
                   pl.BlockSpec((1, N_MIX_HEADS * V_ROWS, tm), lambda i: (i, 0, 0)),
                   row(MEM_WIDTH)],
        out_shape=[jax.ShapeDtypeStruct((m, AUG_WIDTH), BF16),
                   jax.ShapeDtypeStruct((m, AUG_WIDTH), BF16),
                   jax.ShapeDtypeStruct((m // tm, N_MIX_HEADS * V_ROWS, tm), BF16),
                   jax.ShapeDtypeStruct((m, MEM_WIDTH), BF16)],
        scratch_shapes=[pltpu.VMEM((tm, d), BF16), pltpu.VMEM((1, LANES), F32),
                        pltpu.VMEM((tm, MIX_WIDTH), F32), pltpu.VMEM((tm, MIX_WIDTH), F32)],
        compiler_params=_params("arbitrary"),
        name="proj_c",
    )(x, g_pre, w_in, w_v_t, w_f, w_qm, fb.reshape(1, LANES), *consts)


def _fox_body(q_ref, qnext_ref, k_ref, vt_ref, o_ref, acc_ref, s0_ref, s1_ref, s2_ref,
              mx0_ref, mx1_ref, mx2_ref, *, tq, n_q):
    i = pl.program_id(2)
    causal = (lax.broadcasted_iota(jnp.int32, (tq, tq), 1)
              >= lax.broadcasted_iota(jnp.int32, (tq, tq), 0))
    acc_ref[...] = jnp.zeros_like(acc_ref)

    def scores(queries_ref, j, buf, diagonal):
        s_ref, mx_ref = buf
        rows = pl.ds(pl.multiple_of(j * tq, tq), tq)
        for h in range(FOX_HEADS):
            lanes = slice(h * LANES, (h + 1) * LANES)
            st = _dot_nt(k_ref[0, rows, lanes], queries_ref[0, :, lanes])
            if diagonal:
                st = jnp.where(causal, st, NEG_INF)
            s_ref[h] = st
            mx_ref[h] = jnp.max(st, axis=0, keepdims=True)

    def consume(j, buf, m_run):
        s_ref, mx_ref = buf
        m_out = []
        for h in range(FOX_HEADS):
            m_new = jnp.maximum(m_run[h], mx_ref[h])
            alpha = jnp.exp2(m_run[h] - m_new)
            e = jnp.exp2(s_ref[h] - m_new).astype(BF16)
            acc_ref[h] = alpha * acc_ref[h] + _dot(vt_ref[j, h * V_ROWS:(h + 1) * V_ROWS, :], e)
            m_out.append(m_new)
        return tuple(m_out)

    def run(diag, other, handover):
        @pl.when(i == 0)
        def _():
            scores(q_ref, i, diag, True)

        def prefetch_next_tile():
            scores(qnext_ref, jnp.minimum(i + 1, n_q - 1), handover, True)

        def two_tiles(t, m_run):
            j = i - 2 * t
            scores(q_ref, j - 1, other, False)
            m_run = consume(j, diag, m_run)
            scores(q_ref, j - 2, diag, False)
            return consume(j - 1, other, m_run)

        def last_two(m_run):
            scores(q_ref, 0, other, False)
            m_run = consume(1, diag, m_run)
            prefetch_next_tile()
            return consume(0, other, m_run)

        def last_one(m_run):
            prefetch_next_tile()
            return consume(0, diag, m_run)

        m_run = tuple(jnp.full((1, tq), NEG_INF, F32) for _ in range(FOX_HEADS))
        m_run = lax.fori_loop(0, i // 2, two_tiles, m_run)
        lax.cond(i % 2 == 1, last_two, last_one, m_run)

    bufs = ((s0_ref, mx0_ref), (s1_ref, mx1_ref), (s2_ref, mx2_ref))
    lax.cond(i % 2 == 0,
             lambda: run(bufs[0], bufs[1], bufs[2]),
             lambda: run(bufs[2], bufs[1], bufs[0]))
    for p in range(FOX_HEADS // 2):
        halves = []
        for h in (2 * p, 2 * p + 1):
            halves.append(acc_ref[h, :HEAD_DIM, :] / acc_ref[h, HEAD_DIM:HEAD_DIM + 1, :])
        o_ref[0, :, p * PAIR:(p + 1) * PAIR] = jnp.concatenate(halves, axis=0).T.astype(BF16)


def _fox_attention(q_aug, k_aug, v_t, batch, seq):
    tq = tk = TK_FOX
    q3 = q_aug.reshape(batch, seq, AUG_WIDTH)
    k3 = k_aug.reshape(batch, seq, AUG_WIDTH)
    aug_w = FOX_HEADS * LANES
    v_w = FOX_HEADS * HEAD_DIM
    score_buf = pltpu.VMEM((FOX_HEADS, tk, tq), F32)
    max_buf = pltpu.VMEM((FOX_HEADS, 1, tq), F32)
    n_q = seq // tq
    out = pl.pallas_call(
        functools.partial(_fox_body, tq=tq, n_q=n_q),
        grid=(batch, N_MIX_HEADS // FOX_HEADS, n_q),
        in_specs=[pl.BlockSpec((1, tq, aug_w), lambda b, p, i: (b, i, p)),
                  pl.BlockSpec((1, tq, aug_w), lambda b, p, i: (b, jnp.minimum(i + 1, n_q - 1), p)),
                  pl.BlockSpec((1, seq, aug_w), lambda b, p, i: (b, 0, p)),
                  pl.BlockSpec((seq // tk, FOX_HEADS * V_ROWS, tk), lambda b, p, i: (b, p, 0))],
        out_specs=pl.BlockSpec((1, tq, v_w), lambda b, p, i: (b, i, p)),
        out_shape=jax.ShapeDtypeStruct((batch, seq, MIX_WIDTH), BF16),
        scratch_shapes=[pltpu.VMEM((FOX_HEADS, V_ROWS, tq), F32)] + [score_buf] * 3
        + [max_buf] * 3,
        compiler_params=_params("parallel", "arbitrary", "arbitrary"),
        name="fox_attention",
    )(q3, q3, k3, v_t)
    return out.reshape(batch * seq, MIX_WIDTH)


def _out_c_body(x_ref, att_ref, qm_ref, kv_ref, w_ref, g_ref, o_ref):
    for sub in range(x_ref.shape[0] // OUT_SUB):
        tile = slice(sub * OUT_SUB, (sub + 1) * OUT_SUB)
        mem_out = _mem_attention(qm_ref[tile, :], kv_ref).astype(BF16)
        y = _dot(att_ref[tile, :], w_ref[:MIX_WIDTH, :]) + _dot(mem_out, w_ref[MIX_WIDTH:, :])
        o_ref[tile, :] = x_ref[tile, :] + _rms_norm(y, g_ref[...])


def _out_c(x, seq, att, qm, kv, w_out, g_post):
    m, d = x.shape
    tm = TM_OUT
    tiles_per_seq = seq // tm
    n_mem = kv.shape[1]
    row = lambda w: pl.BlockSpec((tm, w), lambda i: (i, 0))
    return pl.pallas_call(
        _out_c_body,
        grid=(m // tm,),
        in_specs=[row(d), row(MIX_WIDTH), row(MEM_WIDTH),
                  pl.BlockSpec((1, n_mem, 2 * MEM_WIDTH), lambda i: (i // tiles_per_seq, 0, 0)),
                  _const_spec(w_out.shape), _const_spec((1, d))],
        out_specs=row(d),
        out_shape=jax.ShapeDtypeStruct((m, d), F32),
        compiler_params=_params("parallel"),
        name="out_c",
    )(x, att, qm, kv, w_out, g_post)


def kernel(x, mem, positions, norm_g, mem_norm_g, w_mem_kv, ffn_w_gate_up, ffn_w_down,
           a_w_in, a_spatial_w, a_spatial_b, a_v_norm_g, a_w_out,
           b_w_in, b_w_out, c_w_in, c_forget_bias, c_w_out):
    batch, seq, d = x.shape
    depth = norm_g.shape[0]
    bf = lambda w: w.astype(BF16)

    kv_all = _mem_kv(mem, mem_norm_g, bf(w_mem_kv))
    w_gate_up, w_down = bf(ffn_w_gate_up), bf(ffn_w_down)
    xf = x.reshape(batch * seq, d)
    for i in range(depth):
        kind, j = i % 3, i // 3
        g = norm_g[i].reshape(norm_g.shape[1], 1, d)
        xf = _ffn(xf, g[0], g[1], w_gate_up, w_down, (i, 0))
        kv = kv_all[i]
        if kind == 0:
            xf = _mixer_a(xf, seq, g[2], g[3], kv, bf(a_w_in[j]), bf(a_spatial_w[j]),
                          a_spatial_b[j], a_v_norm_g[j], bf(a_w_out[j]))
        elif kind == 1:
            *groups, qm = _proj_b(xf, batch, seq, g[2], bf(b_w_in[j]), positions)
            outs, lses = zip(*[_dilated_group(qkv_g, w, dil)
                               for qkv_g, (w, dil) in zip(groups, DILATED_GROUPS)])
            xf = _out_b(xf, seq, outs, lses, qm, kv, bf(b_w_out[j]), g[3])
        else:
            q_aug, k_aug, v_t, qm = _proj_c(xf, seq, g[2], bf(c_w_in[j]), c_forget_bias[j])
            att = _fox_attention(q_aug, k_aug, v_t, batch, seq)
            xf = _out_c(xf, seq, att, qm, kv, bf(c_w_out[j]), g[3])
        xf = _ffn(xf, g[4], g[5], w_gate_up, w_down, (i, 1))
    return xf.reshape(batch, seq, d)
```

```python
import functools

import jax
import jax.numpy as jnp
from jax import lax
from jax.experimental import pallas as pl
from jax.experimental.pallas import tpu as pltpu

F32 = jnp.float32
BF16 = jnp.bfloat16

HEAD_DIM = 64
N_MIX_HEADS = 12
MIX_WIDTH = N_MIX_HEADS * HEAD_DIM
N_MEM_HEADS = 4
MEM_WIDTH = N_MEM_HEADS * HEAD_DIM
CHUNK = 128
ROPE_THETA = 10000.0
DILATED_GROUPS = ((128, 1), (512, 4), (2048, 16))
DIL_OUT_WIDTH = (N_MIX_HEADS // len(DILATED_GROUPS)) * HEAD_DIM
RMS_EPS = 1e-6
LN_EPS = 1e-5
NEG_INF = -1e30
ATTN_SCALE = HEAD_DIM ** -0.5
LOG2_E = 1.4426950408889634
Q_SCALE = ATTN_SCALE * LOG2_E

LANES = 128
PAIR = 2 * HEAD_DIM
MXU_N = 256
VMEM_LIMIT_BYTES = 56 * 1024 * 1024

TM_FFN = 1024
FFN_SUB = 512
TM_PROJ = 512
TM_MIX_A = 1024
MIX_A_SUB = 512
TM_OUT = 1024
OUT_SUB = 512
TK_FOX = 512
TM_PROJ_C = 1024
FOX_HEADS = 4
DIL_CHAINS = 4
FF_CHUNK = 256


def _params(*semantics):
    return pltpu.CompilerParams(dimension_semantics=semantics,
                                vmem_limit_bytes=VMEM_LIMIT_BYTES)


def _const_spec(shape):
    zeros = (0,) * len(shape)
    return pl.BlockSpec(shape, lambda *_: zeros, pipeline_mode=pl.Buffered(1))


def _rms_norm(x, g):
    ms = jnp.mean(x * x, axis=-1, keepdims=True)
    return x * lax.rsqrt(ms + RMS_EPS) * g


def _dot(a, b):
    return jnp.dot(a, b, preferred_element_type=F32)


def _dot_nt(a, b):
    return lax.dot_general(a, b, (((1,), (1,)), ((), ())), preferred_element_type=F32)


def _first_head_lanes(shape):
    return lax.broadcasted_iota(jnp.int32, shape, len(shape) - 1) % PAIR < HEAD_DIM


def _ffn_body(x_ref, gpre_ref, gpost_ref, wgu_ref, wd_ref, o_ref, xn_ref, act_ref, *, d_ff):
    for sub in range(x_ref.shape[0] // FFN_SUB):
        rows = slice(sub * FFN_SUB, (sub + 1) * FFN_SUB)
        xn_ref[rows, :] = _rms_norm(x_ref[rows, :], gpre_ref[...]).astype(BF16)
        for c in range(d_ff // FF_CHUNK):
            lo = c * FF_CHUNK
            gate = _dot(xn_ref[rows, :], wgu_ref[:, lo:lo + FF_CHUNK])
            up = _dot(xn_ref[rows, :], wgu_ref[:, d_ff + lo:d_ff + lo + FF_CHUNK])
            act_ref[rows, lo:lo + FF_CHUNK] = (gate * jax.nn.sigmoid(gate) * up).astype(BF16)
        y = _dot(act_ref[rows, :], wd_ref[...])
        o_ref[rows, :] = x_ref[rows, :] + 0.5 * _rms_norm(y, gpost_ref[...])


def _stacked_spec(stacked, index):
    k = len(index)
    tail = stacked.shape[k:]
    where = tuple(index) + (0,) * len(tail)
    return pl.BlockSpec((None,) * k + tail, lambda *_: where, pipeline_mode=pl.Buffered(1))


def _ffn(x, g_pre, g_post, w_gate_up, w_down, index):
    m, d = x.shape
    d_ff = w_down.shape[-2]
    assert m % TM_FFN == 0 and d_ff % FF_CHUNK == 0
    row = pl.BlockSpec((TM_FFN, d), lambda i: (i, 0))
    return pl.pallas_call(
        functools.partial(_ffn_body, d_ff=d_ff),
        grid=(m // TM_FFN,),
        in_specs=[row, _const_spec((1, d)), _const_spec((1, d)),
                  _stacked_spec(w_gate_up, index), _stacked_spec(w_down, index)],
        out_specs=row,
        out_shape=jax.ShapeDtypeStruct((m, d), F32),
        scratch_shapes=[pltpu.VMEM((TM_FFN, d), BF16), pltpu.VMEM((TM_FFN, d_ff), BF16)],
        compiler_params=_params("parallel"),
        name="ffn",
    )(x, g_pre, g_post, w_gate_up, w_down)


def _mem_kv_body(mem_ref, g_ref, w_ref, o_ref):
    for i in range(mem_ref.shape[0]):
        o_ref[0, i] = _dot(_rms_norm(mem_ref[i], g_ref[0]).astype(BF16), w_ref[0]).astype(BF16)


def _mem_kv(mem, mem_norm_g, w_mem_kv):
    b, n_mem, d = mem.shape
    depth = w_mem_kv.shape[0]
    return pl.pallas_call(
        _mem_kv_body,
        grid=(depth,),
        in_specs=[_const_spec(mem.shape),
                  pl.BlockSpec((1, 1, d), lambda l: (l, 0, 0)),
                  pl.BlockSpec((1, d, 2 * MEM_WIDTH), lambda l: (l, 0, 0))],
        out_specs=pl.BlockSpec((1, b, n_mem, 2 * MEM_WIDTH), lambda l: (l, 0, 0, 0)),
        out_shape=jax.ShapeDtypeStruct((depth, b, n_mem, 2 * MEM_WIDTH), BF16),
        compiler_params=_params("parallel"),
        name="mem_kv",
    )(mem, mem_norm_g.reshape(depth, 1, d), w_mem_kv)


def _stack_heads(qp, first):
    zero = jnp.zeros_like(qp)
    return jnp.concatenate([jnp.where(first, qp, zero), jnp.where(first, zero, qp)], axis=0)


def _softmax_pv(s, v, with_lse=False):
    mx = jnp.max(s, axis=-1, keepdims=True)
    e = jnp.exp2(s - mx)
    denom = jnp.sum(e, axis=-1, keepdims=True)
    out = _dot(e.astype(BF16), v) / denom
    if with_lse:
        return out, mx + jnp.log2(denom)
    return out


def _mem_attention(qm, kv_ref):
    tm = qm.shape[0]
    outs = []
    for p in range(MEM_WIDTH // PAIR):
        qp = qm[:, p * PAIR:(p + 1) * PAIR]
        k = kv_ref[0, :, p * PAIR:(p + 1) * PAIR]
        v = kv_ref[0, :, MEM_WIDTH + p * PAIR:MEM_WIDTH + (p + 1) * PAIR]
        first = _first_head_lanes(qp.shape)
        acc = _softmax_pv(_dot_nt(_stack_heads(qp, first), k), v)
        outs.append(jnp.where(first, acc[:tm], acc[tm:]))
    return jnp.concatenate(outs, axis=-1)


def _mixer_a_body(x_ref, gpre_ref, win_ref, vg_ref, sw_ref, sb_ref, kv_ref, wout_ref, gpost_ref,
                  o_ref, u_ref, v_ref, gated_ref, wc_ref):
    t_idx = lax.broadcasted_iota(jnp.int32, (CHUNK, CHUNK), 0)
    s_idx = lax.broadcasted_iota(jnp.int32, (CHUNK, CHUNK), 1)
    for g in range(N_MIX_HEADS):
        wc_ref[g] = jnp.where(s_idx <= t_idx, sw_ref[g], jnp.zeros((CHUNK, CHUNK), BF16))
    first = _first_head_lanes((CHUNK, PAIR))

    for sub in range(x_ref.shape[0] // MIX_A_SUB):
        tile = slice(sub * MIX_A_SUB, (sub + 1) * MIX_A_SUB)
        h = _rms_norm(x_ref[tile, :], gpre_ref[...]).astype(BF16)
        u_ref[tile, :] = jax.nn.gelu(_dot(h, win_ref[:, :MIX_WIDTH]))
        v = jax.nn.gelu(_dot(h, win_ref[:, MIX_WIDTH:2 * MIX_WIDTH]))
        mu = jnp.mean(v, axis=-1, keepdims=True)
        vc = v - mu
        var = jnp.mean(vc * vc, axis=-1, keepdims=True)
        v_ref[tile, :] = (vc * lax.rsqrt(var + LN_EPS) * vg_ref[...]).astype(BF16)
        qm = (_dot(h, win_ref[:, 2 * MIX_WIDTH:]) * Q_SCALE).astype(BF16)

        for c in range(sub * MIX_A_SUB // CHUNK, (sub + 1) * MIX_A_SUB // CHUNK):
            rows = slice(c * CHUNK, (c + 1) * CHUNK)
            for p in range(MIX_WIDTH // PAIR):
                lanes = slice(p * PAIR, (p + 1) * PAIR)
                vp = v_ref[rows, lanes]
                mixed = jnp.where(first, _dot(wc_ref[2 * p], vp), _dot(wc_ref[2 * p + 1], vp))
                gated_ref[rows, lanes] = (u_ref[rows, lanes]
                                          * (mixed + sb_ref[:, lanes])).astype(BF16)

        mem_out = _mem_attention(qm, kv_ref).astype(BF16)
        y = (_dot(gated_ref[tile, :], wout_ref[:MIX_WIDTH, :])
             + _dot(mem_out, wout_ref[MIX_WIDTH:, :]))
        o_ref[tile, :] = x_ref[tile, :] + _rms_norm(y, gpost_ref[...])


def _mixer_a(x, seq, g_pre, g_post, kv, w_in, spatial_w, spatial_b, v_norm_g, w_out):
    m, d = x.shape
    tm = TM_MIX_A
    assert seq % tm == 0 and tm % CHUNK == 0
    tiles_per_seq = seq // tm
    n_mem = kv.shape[1]
    row = pl.BlockSpec((tm, d), lambda i: (i, 0))
    bias = jnp.repeat(spatial_b.T, HEAD_DIM, axis=1)
    return pl.pallas_call(
        _mixer_a_body,
        grid=(m // tm,),
        in_specs=[row, _const_spec((1, d)), _const_spec(w_in.shape), _const_spec((1, MIX_WIDTH)),
                  _const_spec(spatial_w.shape), _const_spec(bias.shape),
                  pl.BlockSpec((1, n_mem, 2 * MEM_WIDTH), lambda i: (i // tiles_per_seq, 0, 0)),
                  _const_spec(w_out.shape), _const_spec((1, d))],
        out_specs=row,
        out_shape=jax.ShapeDtypeStruct((m, d), F32),
        scratch_shapes=[pltpu.VMEM((tm, MIX_WIDTH), F32), pltpu.VMEM((tm, MIX_WIDTH), BF16),
                        pltpu.VMEM((tm, MIX_WIDTH), BF16),
                        pltpu.VMEM((N_MIX_HEADS, CHUNK, CHUNK), BF16)],
        compiler_params=_params("parallel"),
        name="mixer_a",
    )(x, g_pre, w_in, v_norm_g.reshape(1, MIX_WIDTH), spatial_w, bias, kv, w_out, g_post)


def _rope_constants():
    inv_freq = ROPE_THETA ** (-jnp.arange(0, HEAD_DIM, 2, dtype=F32) / HEAD_DIM)
    freq = jnp.tile(inv_freq, LANES // (HEAD_DIM // 2)).reshape(1, LANES)
    half = jnp.concatenate([-jnp.ones((HEAD_DIM // 2,), F32), jnp.ones((HEAD_DIM // 2,), F32)])
    sign = jnp.tile(half, LANES // HEAD_DIM).reshape(1, LANES)
    return freq, sign


def _rope(t, cos, sin_signed):
    first_half = lax.broadcasted_iota(jnp.int32, t.shape, 1) % HEAD_DIM < HEAD_DIM // 2
    swapped = jnp.where(first_half,
                        pltpu.roll(t, LANES - HEAD_DIM // 2, axis=1),
                        pltpu.roll(t, HEAD_DIM // 2, axis=1))
    return t * cos + swapped * sin_signed


def _residue_spec(tm, dilation, width, tiles_per_seq):
    return pl.BlockSpec((1, dilation, tm // dilation, width),
                        lambda i: (i // tiles_per_seq, 0, i % tiles_per_seq, 0))


def _proj_b_body(x_ref, g_ref, w_ref, pos_ref, freq_ref, sign_ref, g0_ref, g1_ref, g2_ref, qm_ref,
                 h_ref, t_ref):
    tm = x_ref.shape[0]
    h_ref[...] = _rms_norm(x_ref[...], g_ref[...]).astype(BF16)
    angle = pos_ref[...].astype(F32) * freq_ref[...]
    cos = jnp.cos(angle)
    sin = jnp.sin(angle) * sign_ref[...]
    outs = (g0_ref, g1_ref, g2_ref)
    n_groups = len(DILATED_GROUPS)
    for c in range(3 * n_groups):
        kind, grp = divmod(c, n_groups)
        lo = c * DIL_OUT_WIDTH
        t = _dot(h_ref[...], w_ref[:, lo:lo + DIL_OUT_WIDTH])
        if kind < 2:
            scale = Q_SCALE if kind == 0 else 1.0
            t = jnp.concatenate(
                [_rope(t[:, j * LANES:(j + 1) * LANES], cos, sin)
                 for j in range(DIL_OUT_WIDTH // LANES)], axis=-1) * scale
        dil = DILATED_GROUPS[grp][1]
        cols = slice(kind * DIL_OUT_WIDTH, (kind + 1) * DIL_OUT_WIDTH)
        if dil == 1:
            outs[grp][0, 0, :, cols] = t.astype(BF16)
        else:
            halves = DIL_OUT_WIDTH // LANES
            for j in range(halves):
                t_ref[j] = t[:, j * LANES:(j + 1) * LANES]
            for r in range(dil):
                outs[grp][0, r, :, cols] = jnp.concatenate(
                    [t_ref[j, pl.ds(r, tm // dil, stride=dil), :] for j in range(halves)],
                    axis=-1).astype(BF16)
    qm_ref[...] = (_dot(h_ref[...], w_ref[:, 3 * MIX_WIDTH:]) * Q_SCALE).astype(BF16)


def _proj_b(x, batch, seq, g_pre, w_in, positions):
    m, d = x.shape
    tm = TM_PROJ
    tiles_per_seq = seq // tm
    freq, sign = _rope_constants()
    row = lambda w: pl.BlockSpec((tm, w), lambda i: (i, 0))
    width = 3 * DIL_OUT_WIDTH
    group_specs = [_residue_spec(tm, dil, width, tiles_per_seq) for _, dil in DILATED_GROUPS]
    group_shapes = [jax.ShapeDtypeStruct((batch, dil, seq // dil, width), BF16)
                    for _, dil in DILATED_GROUPS]
    return pl.pallas_call(
        _proj_b_body,
        grid=(m // tm,),
        in_specs=[row(d), _const_spec((1, d)), _const_spec(w_in.shape), row(1),
                  _const_spec((1, LANES)), _const_spec((1, LANES))],
        out_specs=group_specs + [row(MEM_WIDTH)],
        out_shape=group_shapes + [jax.ShapeDtypeStruct((m, MEM_WIDTH), BF16)],
        scratch_shapes=[pltpu.VMEM((tm, d), BF16),
                        pltpu.VMEM((DIL_OUT_WIDTH // LANES, tm, LANES), F32)],
        compiler_params=_params("parallel"),
        name="proj_b",
    )(x, g_pre, w_in, positions.reshape(m, 1), freq, sign)


def _dilated_body(qkv_ref, o_ref, lse_ref, *, n_blocks, span, unroll):
    w = DIL_OUT_WIDTH
    qi = lax.broadcasted_iota(jnp.int32, (2 * span, 2 * span), 0) % span
    ki = lax.broadcasted_iota(jnp.int32, (2 * span, 2 * span), 1)
    band = ((ki < span) & (ki >= qi)) | ((ki >= span) & (ki - span <= qi))
    own_only = (lax.broadcasted_iota(jnp.int32, (2 * span, span), 1)
                <= lax.broadcasted_iota(jnp.int32, (2 * span, span), 0) % span)
    first = _first_head_lanes((span, PAIR))
    own_lanes = jnp.concatenate([first, ~first], axis=0)

    def block(r0, has_prev):
        own = pl.ds(r0, span)
        keys = pl.ds(r0 - span, 2 * span) if has_prev else own
        for r in range(qkv_ref.shape[1]):
            for p in range(w // PAIR):
                ql = slice(p * PAIR, (p + 1) * PAIR)
                q2 = _stack_heads(qkv_ref[0, r, own, ql], first)
                k2 = qkv_ref[0, r, keys, w + p * PAIR:w + (p + 1) * PAIR]
                v2 = qkv_ref[0, r, keys, 2 * w + p * PAIR:2 * w + (p + 1) * PAIR]
                s = jnp.where(band if has_prev else own_only, _dot_nt(q2, k2), NEG_INF)
                acc, lse = _softmax_pv(s, v2, with_lse=True)
                o_ref[0, r, own, ql] = jnp.where(first, acc[:span], acc[span:]).astype(BF16)
                lse = jnp.where(own_lanes, lse, 0.0)
                lse_ref[0, r, own, ql] = lse[:span] + lse[span:]

    block(0, False)

    def step(blk, carry):
        block(pl.multiple_of(blk * span, span), True)
        return carry

    lax.fori_loop(1, n_blocks, step, 0, unroll=unroll)


def _dilated_group(qkv, window, dilation):
    b, _, length, width = qkv.shape
    span = window // dilation
    assert length % span == 0
    per_step = min(dilation, DIL_CHAINS)
    unroll = DIL_CHAINS // per_step
    assert dilation % per_step == 0
    out_spec = pl.BlockSpec((1, per_step, length, DIL_OUT_WIDTH), lambda i, r: (i, r, 0, 0))
    return pl.pallas_call(
        functools.partial(_dilated_body, n_blocks=length // span, span=span, unroll=unroll),
        grid=(b, dilation // per_step),
        in_specs=[pl.BlockSpec((1, per_step, length, width), lambda i, r: (i, r, 0, 0))],
        out_specs=[out_spec, out_spec],
        out_shape=[jax.ShapeDtypeStruct((b, dilation, length, DIL_OUT_WIDTH), BF16),
                   jax.ShapeDtypeStruct((b, dilation, length, DIL_OUT_WIDTH), F32)],
        compiler_params=_params("parallel", "parallel"),
        name=f"dilated_{dilation}",
    )(qkv)


def _out_b_body(x_ref, o0_ref, o1_ref, o2_ref, l0_ref, l1_ref, l2_ref, qm_ref, kv_ref, w_ref,
                g_ref, o_ref, *scratch):
    tm = x_ref.shape[0]
    halves = DIL_OUT_WIDTH // LANES

    def to_token_major(ref, buf):
        dil = ref.shape[1]
        for r in range(dil):
            rows = ref[0, r].astype(F32)
            for j in range(halves):
                buf[j, pl.ds(r, tm // dil, stride=dil), :] = rows[:, j * LANES:(j + 1) * LANES]

    for g, (o_g, l_g) in enumerate(((o0_ref, l0_ref), (o1_ref, l1_ref), (o2_ref, l2_ref))):
        to_token_major(o_g, scratch[2 * g])
        to_token_major(l_g, scratch[2 * g + 1])

    for sub in range(tm // OUT_SUB):
        tile = slice(sub * OUT_SUB, (sub + 1) * OUT_SUB)
        read = lambda buf: jnp.concatenate([buf[j, tile, :] for j in range(halves)], axis=-1)
        outs = [read(scratch[2 * g]) for g in range(len(DILATED_GROUPS))]
        lses = [read(scratch[2 * g + 1]) for g in range(len(DILATED_GROUPS))]
        mx = jnp.maximum(jnp.maximum(lses[0], lses[1]), lses[2])
        es = [jnp.exp2(l - mx) for l in lses]
        denom = es[0] + es[1] + es[2]
        merged = sum((e / denom) * o for e, o in zip(es, outs))
        mem_out = _mem_attention(qm_ref[tile, :], kv_ref).astype(BF16)
        y = (_dot(merged.astype(BF16), w_ref[:DIL_OUT_WIDTH, :])
             + _dot(mem_out, w_ref[DIL_OUT_WIDTH:, :]))
        o_ref[tile, :] = x_ref[tile, :] + _rms_norm(y, g_ref[...])


def _out_b(x, seq, outs, lses, qm, kv, w_out, g_post):
    m, d = x.shape
    tm = TM_OUT
    tiles_per_seq = seq // tm
    n_mem = kv.shape[1]
    row = lambda w: pl.BlockSpec((tm, w), lambda i: (i, 0))
    group_specs = [_residue_spec(tm, dil, DIL_OUT_WIDTH, tiles_per_seq)
                   for _, dil in DILATED_GROUPS]
    return pl.pallas_call(
        _out_b_body,
        grid=(m // tm,),
        in_specs=[row(d)] + group_specs * 2 + [
            row(MEM_WIDTH),
            pl.BlockSpec((1, n_mem, 2 * MEM_WIDTH), lambda i: (i // tiles_per_seq, 0, 0)),
            _const_spec(w_out.shape), _const_spec((1, d))],
        out_specs=row(d),
        out_shape=jax.ShapeDtypeStruct((m, d), F32),
        scratch_shapes=[pltpu.VMEM((DIL_OUT_WIDTH // LANES, tm, LANES), F32)]
        * (2 * len(DILATED_GROUPS)),
        compiler_params=_params("parallel"),
        name="out_b",
    )(x, *outs, *lses, qm, kv, w_out, g_post)


def _split3(x):
    hi = x.astype(BF16)
    r1 = x - hi.astype(F32)
    mid = r1.astype(BF16)
    lo = (r1 - mid.astype(F32)).astype(BF16)
    return hi, mid, lo


N_BIAS_TERMS = 3
AUG_WIDTH = N_MIX_HEADS * LANES
V_ROWS = HEAD_DIM + 16


def _pack_terms(hi, mid, lo):
    return (hi.astype(F32) + pltpu.roll(mid.astype(F32), N_MIX_HEADS, axis=1)
            + pltpu.roll(lo.astype(F32), 2 * N_MIX_HEADS, axis=1)).astype(BF16)


def _bias_selectors():
    rows = jnp.arange(LANES)
    term, head = rows // N_MIX_HEADS, rows % N_MIX_HEADS
    valid = rows < N_BIAS_TERMS * N_MIX_HEADS
    cols = jnp.arange(MIX_WIDTH)
    base = (head // 2) * PAIR + jnp.where(head % 2 == 0, HEAD_DIM, 0)

    def select(offset):
        target = base + offset + term
        return ((cols[None, :] == target[:, None]) & valid[:, None]).astype(BF16)

    def ones(offset):
        lane = cols % HEAD_DIM
        return ((lane >= offset) & (lane < offset + N_BIAS_TERMS)).astype(F32)[None, :]

    q_sel, k_sel = select(0), select(N_BIAS_TERMS)
    q_one, k_one = ones(N_BIAS_TERMS), ones(0)
    return q_sel, k_sel, q_one, k_one


def _proj_c_body(x_ref, g_ref, w_ref, wvt_ref, wf_ref, wqm_ref, fb_ref, qsel_ref, ksel_ref,
                 qone_ref, kone_ref, q_ref, k_ref, vt_ref, qm_ref, h_ref, carry_ref, qext_ref,
                 kext_ref, *, tiles_per_seq):
    ts = TK_FOX

    @pl.when(pl.program_id(0) % tiles_per_seq == 0)
    def _():
        carry_ref[...] = jnp.zeros_like(carry_ref)

    lane = lax.broadcasted_iota(jnp.int32, (ts, LANES), 1)
    tri = (lax.broadcasted_iota(jnp.int32, (ts, ts), 1)
           <= lax.broadcasted_iota(jnp.int32, (ts, ts), 0)).astype(BF16)
    first = _first_head_lanes((ts, PAIR))
    ones_row = (lax.broadcasted_iota(jnp.int32, (V_ROWS - HEAD_DIM, ts), 0) == 0).astype(BF16)

    for sub in range(x_ref.shape[0] // ts):
        tile = slice(sub * ts, (sub + 1) * ts)
        h_ref[tile, :] = _rms_norm(x_ref[tile, :], g_ref[...]).astype(BF16)

        z = _dot(h_ref[tile, :], wf_ref[...]) + fb_ref[...]
        log_f = jnp.minimum(z, 0.0) - jnp.log1p(jnp.exp(-jnp.abs(z)))
        log_f = jnp.where(lane < N_MIX_HEADS, log_f, 0.0)
        sums = _dot(tri, _pack_terms(*_split3(log_f)))
        local = (sums + pltpu.roll(sums, LANES - N_MIX_HEADS, axis=1)
                 + pltpu.roll(sums, LANES - 2 * N_MIX_HEADS, axis=1))
        c = carry_ref[...] + jnp.where(lane < N_MIX_HEADS, local, 0.0)
        carry_ref[...] = c[ts - 1:ts, :]
        terms = _pack_terms(*_split3(c * LOG2_E))
        qext_ref[tile, :] = _dot(terms, qsel_ref[...]) + qone_ref[...]
        kext_ref[tile, :] = kone_ref[...] - _dot(terms, ksel_ref[...])

        for c_idx in range(2 * MIX_WIDTH // MXU_N):
            lo = c_idx * MXU_N
            is_q = lo < MIX_WIDTH
            t = _dot(h_ref[tile, :], w_ref[:, lo:lo + MXU_N])
            if is_q:
                t = t * Q_SCALE
            dst, ext_ref = (q_ref, qext_ref) if is_q else (k_ref, kext_ref)
            for j in range(MXU_N // PAIR):
                p = (lo % MIX_WIDTH) // PAIR + j
                pair = t[:, j * PAIR:(j + 1) * PAIR]
                ext = ext_ref[tile, p * PAIR:(p + 1) * PAIR]
                dst[tile, 2 * p * LANES:(2 * p + 1) * LANES] = jnp.where(
                    first, pair, ext).astype(BF16)
                dst[tile, (2 * p + 1) * LANES:(2 * p + 2) * LANES] = pltpu.roll(
                    jnp.where(first, ext, pair), HEAD_DIM, axis=1).astype(BF16)
        for c_idx in range(MIX_WIDTH // MXU_N):
            lo = c_idx * MXU_N
            vt = _dot_nt(wvt_ref[lo:lo + MXU_N, :], h_ref[tile, :]).astype(BF16)
            for j in range(MXU_N // HEAD_DIM):
                head = lo // HEAD_DIM + j
                vt_ref[sub, head * V_ROWS:head * V_ROWS + HEAD_DIM, :] = (
                    vt[j * HEAD_DIM:(j + 1) * HEAD_DIM])
                vt_ref[sub, head * V_ROWS + HEAD_DIM:(head + 1) * V_ROWS, :] = ones_row
        qm_ref[tile, :] = (_dot(h_ref[tile, :], wqm_ref[...]) * Q_SCALE).astype(BF16)


def _proj_c(x, seq, g_pre, w_in, forget_bias):
    m, d = x.shape
    tm = TM_PROJ_C
    v_tiles = tm // TK_FOX
    qkv_w = 3 * MIX_WIDTH
    w_v_t = w_in[:, 2 * MIX_WIDTH:qkv_w].T
    w_f = jnp.pad(w_in[:, qkv_w:qkv_w + N_MIX_HEADS], ((0, 0), (0, LANES - N_MIX_HEADS)))
    w_qm = w_in[:, qkv_w + N_MIX_HEADS:]
    fb = jnp.concatenate([forget_bias.astype(F32), jnp.zeros((LANES - N_MIX_HEADS,), F32)])
    consts = _bias_selectors()
    row = lambda w_: pl.BlockSpec((tm, w_), lambda i: (i, 0))
    return pl.pallas_call(
        functools.partial(_proj_c_body, tiles_per_seq=seq // tm),
        grid=(m // tm,),
        in_specs=[row(d), _const_spec((1, d)), _const_spec(w_in.shape), _const_spec(w_v_t.shape),
                  _const_spec(w_f.shape), _const_spec(w_qm.shape), _const_spec((1, LANES))]
        + [_const_spec(c.shape) for c in consts],
        out_specs=[row(AUG_WIDTH), row(AUG_WIDTH),
                   pl.BlockSpec((v_tiles, N_MIX_HEADS * V_ROWS, TK_FOX), lambda i: (i, 0, 0)),
                   row(MEM_WIDTH)],
        out_shape=[jax.ShapeDtypeStruct((m, AUG_WIDTH), BF16),
                   jax.ShapeDtypeStruct((m, AUG_WIDTH), BF16),
                   jax.ShapeDtypeStruct((m // TK_FOX, N_MIX_HEADS * V_ROWS, TK_FOX), BF16),
                   jax.ShapeDtypeStruct((m, MEM_WIDTH), BF16)],
        scratch_shapes=[pltpu.VMEM((tm, d), BF16), pltpu.VMEM((1, LANES), F32),
                        pltpu.VMEM((tm, MIX_WIDTH), F32), pltpu.VMEM((tm, MIX_WIDTH), F32)],
        compiler_params=_params("arbitrary"),
        name="proj_c",
    )(x, g_pre, w_in, w_v_t, w_f, w_qm, fb.reshape(1, LANES), *consts)


def _fox_body(q_ref, qnext_ref, k_ref, vt_ref, o_ref, acc_ref, s0_ref, s1_ref, s2_ref,
              mx0_ref, mx1_ref, mx2_ref, *, tq, n_q):
    i = pl.program_id(2)
    causal = (lax.broadcasted_iota(jnp.int32, (tq, tq), 1)
              >= lax.broadcasted_iota(jnp.int32, (tq, tq), 0))
    acc_ref[...] = jnp.zeros_like(acc_ref)

    def scores(queries_ref, j, buf, diagonal):
        s_ref, mx_ref = buf
        rows = pl.ds(pl.multiple_of(j * tq, tq), tq)
        for h in range(FOX_HEADS):
            lanes = slice(h * LANES, (h + 1) * LANES)
            st = _dot_nt(k_ref[0, rows, lanes], queries_ref[0, :, lanes])
            if diagonal:
                st = jnp.where(causal, st, NEG_INF)
            s_ref[h] = st
            mx_ref[h] = jnp.max(st, axis=0, keepdims=True)

    def consume(j, buf, m_run):
        s_ref, mx_ref = buf
        m_out = []
        for h in range(FOX_HEADS):
            m_new = jnp.maximum(m_run[h], mx_ref[h])
            alpha = jnp.exp2(m_run[h] - m_new)
            e = jnp.exp2(s_ref[h] - m_new).astype(BF16)
            acc_ref[h] = alpha * acc_ref[h] + _dot(vt_ref[j, h * V_ROWS:(h + 1) * V_ROWS, :], e)
            m_out.append(m_new)
        return tuple(m_out)

    def run(diag, other, handover):
        @pl.when(i == 0)
        def _():
            scores(q_ref, i, diag, True)

        def prefetch_next_tile():
            scores(qnext_ref, jnp.minimum(i + 1, n_q - 1), handover, True)

        def two_tiles(t, m_run):
            j = i - 2 * t
            scores(q_ref, j - 1, other, False)
            m_run = consume(j, diag, m_run)
            scores(q_ref, j - 2, diag, False)
            return consume(j - 1, other, m_run)

        def last_two(m_run):
            scores(q_ref, 0, other, False)
            m_run = consume(1, diag, m_run)
            prefetch_next_tile()
            return consume(0, other, m_run)

        def last_one(m_run):
            prefetch_next_tile()
            return consume(0, diag, m_run)

        m_run = tuple(jnp.full((1, tq), NEG_INF, F32) for _ in range(FOX_HEADS))
        m_run = lax.fori_loop(0, i // 2, two_tiles, m_run)
        lax.cond(i % 2 == 1, last_two, last_one, m_run)

    bufs = ((s0_ref, mx0_ref), (s1_ref, mx1_ref), (s2_ref, mx2_ref))
    lax.cond(i % 2 == 0,
             lambda: run(bufs[0], bufs[1], bufs[2]),
             lambda: run(bufs[2], bufs[1], bufs[0]))
    for p in range(FOX_HEADS // 2):
        halves = []
        for h in (2 * p, 2 * p + 1):
            halves.append(acc_ref[h, :HEAD_DIM, :] / acc_ref[h, HEAD_DIM:HEAD_DIM + 1, :])
        o_ref[0, :, p * PAIR:(p + 1) * PAIR] = jnp.concatenate(halves, axis=0).T.astype(BF16)


def _fox_attention(q_aug, k_aug, v_t, batch, seq):
    tq = tk = TK_FOX
    q3 = q_aug.reshape(batch, seq, AUG_WIDTH)
    k3 = k_aug.reshape(batch, seq, AUG_WIDTH)
    aug_w = FOX_HEADS * LANES
    v_w = FOX_HEADS * HEAD_DIM
    score_buf = pltpu.VMEM((FOX_HEADS, tk, tq), F32)
    max_buf = pltpu.VMEM((FOX_HEADS, 1, tq), F32)
    n_q = seq // tq
    out = pl.pallas_call(
        functools.partial(_fox_body, tq=tq, n_q=n_q),
        grid=(batch, N_MIX_HEADS // FOX_HEADS, n_q),
        in_specs=[pl.BlockSpec((1, tq, aug_w), lambda b, p, i: (b, i, p)),
                  pl.BlockSpec((1, tq, aug_w), lambda b, p, i: (b, jnp.minimum(i + 1, n_q - 1), p)),
                  pl.BlockSpec((1, seq, aug_w), lambda b, p, i: (b, 0, p)),
                  pl.BlockSpec((seq // tk, FOX_HEADS * V_ROWS, tk), lambda b, p, i: (b, p, 0))],
        out_specs=pl.BlockSpec((1, tq, v_w), lambda b, p, i: (b, i, p)),
        out_shape=jax.ShapeDtypeStruct((batch, seq, MIX_WIDTH), BF16),
        scratch_shapes=[pltpu.VMEM((FOX_HEADS, V_ROWS, tq), F32)] + [score_buf] * 3
        + [max_buf] * 3,
        compiler_params=_params("parallel", "arbitrary", "arbitrary"),
        name="fox_attention",
    )(q3, q3, k3, v_t)
    return out.reshape(batch * seq, MIX_WIDTH)


def _out_c_body(x_ref, att_ref, qm_ref, kv_ref, w_ref, g_ref, o_ref):
    for sub in range(x_ref.shape[0] // OUT_SUB):
        tile = slice(sub * OUT_SUB, (sub + 1) * OUT_SUB)
        mem_out = _mem_attention(qm_ref[tile, :], kv_ref).astype(BF16)
        y = _dot(att_ref[tile, :], w_ref[:MIX_WIDTH, :]) + _dot(mem_out, w_ref[MIX_WIDTH:, :])
        o_ref[tile, :] = x_ref[tile, :] + _rms_norm(y, g_ref[...])


def _out_c(x, seq, att, qm, kv, w_out, g_post):
    m, d = x.shape
    tm = TM_OUT
    tiles_per_seq = seq // tm
    n_mem = kv.shape[1]
    row = lambda w: pl.BlockSpec((tm, w), lambda i: (i, 0))
    return pl.pallas_call(
        _out_c_body,
        grid=(m // tm,),
        in_specs=[row(d), row(MIX_WIDTH), row(MEM_WIDTH),
                  pl.BlockSpec((1, n_mem, 2 * MEM_WIDTH), lambda i: (i // tiles_per_seq, 0, 0)),
                  _const_spec(w_out.shape), _const_spec((1, d))],
        out_specs=row(d),
        out_shape=jax.ShapeDtypeStruct((m, d), F32),
        compiler_params=_params("parallel"),
        name="out_c",
    )(x, att, qm, kv, w_out, g_post)


def kernel(x, mem, positions, norm_g, mem_norm_g, w_mem_kv, ffn_w_gate_up, ffn_w_down,
           a_w_in, a_spatial_w, a_spatial_b, a_v_norm_g, a_w_out,
           b_w_in, b_w_out, c_w_in, c_forget_bias, c_w_out):
    batch, seq, d = x.shape
    depth = norm_g.shape[0]
    bf = lambda w: w.astype(BF16)

    kv_all = _mem_kv(mem, mem_norm_g, bf(w_mem_kv))
    w_gate_up, w_down = bf(ffn_w_gate_up), bf(ffn_w_down)
    xf = x.reshape(batch * seq, d)
    for i in range(depth):
        kind, j = i % 3, i // 3
        g = norm_g[i].reshape(norm_g.shape[1], 1, d)
        xf = _ffn(xf, g[0], g[1], w_gate_up, w_down, (i, 0))
        kv = kv_all[i]
        if kind == 0:
            xf = _mixer_a(xf, seq, g[2], g[3], kv, bf(a_w_in[j]), bf(a_spatial_w[j]),
                          a_spatial_b[j], a_v_norm_g[j], bf(a_w_out[j]))
        elif kind == 1:
            *groups, qm = _proj_b(xf, batch, seq, g[2], bf(b_w_in[j]), positions)
            outs, lses = zip(*[_dilated_group(qkv_g, w, dil)
                               for qkv_g, (w, dil) in zip(groups, DILATED_GROUPS)])
            xf = _out_b(xf, seq, outs, lses, qm, kv, bf(b_w_out[j]), g[3])
        else:
            q_aug, k_aug, v_t, qm = _proj_c(xf, seq, g[2], bf(c_w_in[j]), c_forget_bias[j])
            att = _fox_attention(q_aug, k_aug, v_t, batch, seq)
            xf = _out_c(xf, seq, att, qm, kv, bf(c_w_out[j]), g[3])
        xf = _ffn(xf, g[4], g[5], w_gate_up, w_down, (i, 1))
    return xf.reshape(batch, seq, d)
```

```python
import functools

import jax
import jax.numpy as jnp
from jax import lax
from jax.experimental import pallas as pl
from jax.experimental.pallas import tpu as pltpu

F32 = jnp.float32
BF16 = jnp.bfloat16

HEAD_DIM = 64
N_MIX_HEADS = 12
MIX_WIDTH = N_MIX_HEADS * HEAD_DIM
N_MEM_HEADS = 4
MEM_WIDTH = N_MEM_HEADS * HEAD_DIM
CHUNK = 128
ROPE_THETA = 10000.0
DILATED_GROUPS = ((128, 1), (512, 4), (2048, 16))
DIL_OUT_WIDTH = (N_MIX_HEADS // len(DILATED_GROUPS)) * HEAD_DIM
RMS_EPS = 1e-6
LN_EPS = 1e-5
NEG_INF = -1e30
ATTN_SCALE = HEAD_DIM ** -0.5
LOG2_E = 1.4426950408889634
Q_SCALE = ATTN_SCALE * LOG2_E

LANES = 128
BF16_SUBLANES = 16
PAIR = 2 * HEAD_DIM
MXU_N = 256
VMEM_LIMIT_BYTES = 56 * 1024 * 1024

TM_FFN = 1024
FFN_SUB = 512
TM_PROJ = 512
TM_MIX_A = 1024
MIX_A_SUB = 512
TM_OUT = 1024
OUT_SUB = 512
TK_FOX = 512
TM_PROJ_C = 1024
FOX_HEADS = 4
DIL_CHAINS = 8
FF_CHUNK = 256


def _params(*semantics):
    return pltpu.CompilerParams(dimension_semantics=semantics,
                                vmem_limit_bytes=VMEM_LIMIT_BYTES)


def _const_spec(shape):
    zeros = (0,) * len(shape)
    return pl.BlockSpec(shape, lambda *_: zeros, pipeline_mode=pl.Buffered(1))


def _rms_norm(x, g):
    ms = jnp.mean(x * x, axis=-1, keepdims=True)
    return x * lax.rsqrt(ms + RMS_EPS) * g


def _dot(a, b):
    return jnp.dot(a, b, preferred_element_type=F32)


def _dot_nt(a, b):
    return lax.dot_general(a, b, (((1,), (1,)), ((), ())), preferred_element_type=F32)


def _first_head_lanes(shape):
    return lax.broadcasted_iota(jnp.int32, shape, len(shape) - 1) % PAIR < HEAD_DIM


def _ffn_body(x_ref, gpre_ref, gpost_ref, wgu_ref, wd_ref, o_ref, xn_ref, act_ref, *, d_ff):
    for sub in range(x_ref.shape[0] // FFN_SUB):
        rows = slice(sub * FFN_SUB, (sub + 1) * FFN_SUB)
        xn_ref[rows, :] = _rms_norm(x_ref[rows, :], gpre_ref[...]).astype(BF16)
        for c in range(d_ff // FF_CHUNK):
            lo = c * FF_CHUNK
            gate = _dot(xn_ref[rows, :], wgu_ref[:, lo:lo + FF_CHUNK])
            up = _dot(xn_ref[rows, :], wgu_ref[:, d_ff + lo:d_ff + lo + FF_CHUNK])
            act_ref[rows, lo:lo + FF_CHUNK] = (gate * jax.nn.sigmoid(gate) * up).astype(BF16)
        y = _dot(act_ref[rows, :], wd_ref[...])
        o_ref[rows, :] = x_ref[rows, :] + 0.5 * _rms_norm(y, gpost_ref[...])


def _stacked_spec(stacked, index):
    k = len(index)
    tail = stacked.shape[k:]
    where = tuple(index) + (0,) * len(tail)
    return pl.BlockSpec((None,) * k + tail, lambda *_: where, pipeline_mode=pl.Buffered(1))


def _ffn(x, g_pre, g_post, w_gate_up, w_down, index):
    m, d = x.shape
    d_ff = w_down.shape[-2]
    assert m % TM_FFN == 0 and d_ff % FF_CHUNK == 0
    row = pl.BlockSpec((TM_FFN, d), lambda i: (i, 0))
    return pl.pallas_call(
        functools.partial(_ffn_body, d_ff=d_ff),
        grid=(m // TM_FFN,),
        in_specs=[row, _const_spec((1, d)), _const_spec((1, d)),
                  _stacked_spec(w_gate_up, index), _stacked_spec(w_down, index)],
        out_specs=row,
        out_shape=jax.ShapeDtypeStruct((m, d), F32),
        scratch_shapes=[pltpu.VMEM((TM_FFN, d), BF16), pltpu.VMEM((TM_FFN, d_ff), BF16)],
        compiler_params=_params("parallel"),
        name="ffn",
    )(x, g_pre, g_post, w_gate_up, w_down)


def _mem_kv_body(mem_ref, g_ref, w_ref, o_ref):
    for i in range(mem_ref.shape[0]):
        o_ref[0, i] = _dot(_rms_norm(mem_ref[i], g_ref[0]).astype(BF16), w_ref[0]).astype(BF16)


def _mem_kv(mem, mem_norm_g, w_mem_kv):
    b, n_mem, d = mem.shape
    depth = w_mem_kv.shape[0]
    return pl.pallas_call(
        _mem_kv_body,
        grid=(depth,),
        in_specs=[_const_spec(mem.shape),
                  pl.BlockSpec((1, 1, d), lambda l: (l, 0, 0)),
                  pl.BlockSpec((1, d, 2 * MEM_WIDTH), lambda l: (l, 0, 0))],
        out_specs=pl.BlockSpec((1, b, n_mem, 2 * MEM_WIDTH), lambda l: (l, 0, 0, 0)),
        out_shape=jax.ShapeDtypeStruct((depth, b, n_mem, 2 * MEM_WIDTH), BF16),
        compiler_params=_params("parallel"),
        name="mem_kv",
    )(mem, mem_norm_g.reshape(depth, 1, d), w_mem_kv)


def _stack_heads(qp, first):
    zero = jnp.zeros_like(qp)
    return jnp.concatenate([jnp.where(first, qp, zero), jnp.where(first, zero, qp)], axis=0)


def _softmax_pv(s, v, with_lse=False):
    mx = jnp.max(s, axis=-1, keepdims=True)
    e = jnp.exp2(s - mx)
    denom = jnp.sum(e, axis=-1, keepdims=True)
    out = _dot(e.astype(BF16), v) / denom
    if with_lse:
        return out, mx + jnp.log2(denom)
    return out


def _mem_attention(qm, kv_ref):
    tm = qm.shape[0]
    outs = []
    for p in range(MEM_WIDTH // PAIR):
        qp = qm[:, p * PAIR:(p + 1) * PAIR]
        k = kv_ref[0, :, p * PAIR:(p + 1) * PAIR]
        v = kv_ref[0, :, MEM_WIDTH + p * PAIR:MEM_WIDTH + (p + 1) * PAIR]
        first = _first_head_lanes(qp.shape)
        acc = _softmax_pv(_dot_nt(_stack_heads(qp, first), k), v)
        outs.append(jnp.where(first, acc[:tm], acc[tm:]))
    return jnp.concatenate(outs, axis=-1)


def _mixer_a_body(x_ref, gpre_ref, win_ref, vg_ref, sw_ref, sb_ref, kv_ref, wout_ref, gpost_ref,
                  o_ref, u_ref, v_ref, gated_ref, wc_ref):
    t_idx = lax.broadcasted_iota(jnp.int32, (CHUNK, CHUNK), 0)
    s_idx = lax.broadcasted_iota(jnp.int32, (CHUNK, CHUNK), 1)
    for g in range(N_MIX_HEADS):
        wc_ref[g] = jnp.where(s_idx <= t_idx, sw_ref[g], jnp.zeros((CHUNK, CHUNK), BF16))
    first = _first_head_lanes((CHUNK, PAIR))

    for sub in range(x_ref.shape[0] // MIX_A_SUB):
        tile = slice(sub * MIX_A_SUB, (sub + 1) * MIX_A_SUB)
        h = _rms_norm(x_ref[tile, :], gpre_ref[...]).astype(BF16)
        u_ref[tile, :] = jax.nn.gelu(_dot(h, win_ref[:, :MIX_WIDTH]))
        v = jax.nn.gelu(_dot(h, win_ref[:, MIX_WIDTH:2 * MIX_WIDTH]))
        mu = jnp.mean(v, axis=-1, keepdims=True)
        vc = v - mu
        var = jnp.mean(vc * vc, axis=-1, keepdims=True)
        v_ref[tile, :] = (vc * lax.rsqrt(var + LN_EPS) * vg_ref[...]).astype(BF16)
        qm = (_dot(h, win_ref[:, 2 * MIX_WIDTH:]) * Q_SCALE).astype(BF16)

        for c in range(sub * MIX_A_SUB // CHUNK, (sub + 1) * MIX_A_SUB // CHUNK):
            rows = slice(c * CHUNK, (c + 1) * CHUNK)
            for p in range(MIX_WIDTH // PAIR):
                lanes = slice(p * PAIR, (p + 1) * PAIR)
                vp = v_ref[rows, lanes]
                mixed = jnp.where(first, _dot(wc_ref[2 * p], vp), _dot(wc_ref[2 * p + 1], vp))
                gated_ref[rows, lanes] = (u_ref[rows, lanes]
                                          * (mixed + sb_ref[:, lanes])).astype(BF16)

        mem_out = _mem_attention(qm, kv_ref).astype(BF16)
        y = (_dot(gated_ref[tile, :], wout_ref[:MIX_WIDTH, :])
             + _dot(mem_out, wout_ref[MIX_WIDTH:, :]))
        o_ref[tile, :] = x_ref[tile, :] + _rms_norm(y, gpost_ref[...])


def _mixer_a(x, seq, g_pre, g_post, kv, w_in, spatial_w, spatial_b, v_norm_g, w_out):
    m, d = x.shape
    tm = TM_MIX_A
    assert seq % tm == 0 and tm % CHUNK == 0
    tiles_per_seq = seq // tm
    n_mem = kv.shape[1]
    row = pl.BlockSpec((tm, d), lambda i: (i, 0))
    bias = jnp.repeat(spatial_b.T, HEAD_DIM, axis=1)
    return pl.pallas_call(
        _mixer_a_body,
        grid=(m // tm,),
        in_specs=[row, _const_spec((1, d)), _const_spec(w_in.shape), _const_spec((1, MIX_WIDTH)),
                  _const_spec(spatial_w.shape), _const_spec(bias.shape),
                  pl.BlockSpec((1, n_mem, 2 * MEM_WIDTH), lambda i: (i // tiles_per_seq, 0, 0)),
                  _const_spec(w_out.shape), _const_spec((1, d))],
        out_specs=row,
        out_shape=jax.ShapeDtypeStruct((m, d), F32),
        scratch_shapes=[pltpu.VMEM((tm, MIX_WIDTH), F32), pltpu.VMEM((tm, MIX_WIDTH), BF16),
                        pltpu.VMEM((tm, MIX_WIDTH), BF16),
                        pltpu.VMEM((N_MIX_HEADS, CHUNK, CHUNK), BF16)],
        compiler_params=_params("parallel"),
        name="mixer_a",
    )(x, g_pre, w_in, v_norm_g.reshape(1, MIX_WIDTH), spatial_w, bias, kv, w_out, g_post)


def _rope_constants():
    inv_freq = ROPE_THETA ** (-jnp.arange(0, HEAD_DIM, 2, dtype=F32) / HEAD_DIM)
    freq = jnp.tile(inv_freq, LANES // (HEAD_DIM // 2)).reshape(1, LANES)
    half = jnp.concatenate([-jnp.ones((HEAD_DIM // 2,), F32), jnp.ones((HEAD_DIM // 2,), F32)])
    sign = jnp.tile(half, LANES // HEAD_DIM).reshape(1, LANES)
    return freq, sign


def _rope(t, cos, sin_signed):
    first_half = lax.broadcasted_iota(jnp.int32, t.shape, 1) % HEAD_DIM < HEAD_DIM // 2
    swapped = jnp.where(first_half,
                        pltpu.roll(t, LANES - HEAD_DIM // 2, axis=1),
                        pltpu.roll(t, HEAD_DIM // 2, axis=1))
    return t * cos + swapped * sin_signed


def _residue_spec(tm, dilation, width, tiles_per_seq):
    return pl.BlockSpec((1, dilation, tm // dilation, width),
                        lambda i: (i // tiles_per_seq, 0, i % tiles_per_seq, 0))


def _proj_b_body(x_ref, g_ref, w_ref, pos_ref, freq_ref, sign_ref, g0_ref, g1_ref, g2_ref, qm_ref,
                 h_ref, t_ref):
    tm = x_ref.shape[0]
    h_ref[...] = _rms_norm(x_ref[...], g_ref[...]).astype(BF16)
    angle = pos_ref[...].astype(F32) * freq_ref[...]
    cos = jnp.cos(angle)
    sin = jnp.sin(angle) * sign_ref[...]
    outs = (g0_ref, g1_ref, g2_ref)
    n_groups = len(DILATED_GROUPS)
    for c in range(3 * n_groups):
        kind, grp = divmod(c, n_groups)
        lo = c * DIL_OUT_WIDTH
        t = _dot(h_ref[...], w_ref[:, lo:lo + DIL_OUT_WIDTH])
        if kind < 2:
            scale = Q_SCALE if kind == 0 else 1.0
            t = jnp.concatenate(
                [_rope(t[:, j * LANES:(j + 1) * LANES], cos, sin)
                 for j in range(DIL_OUT_WIDTH // LANES)], axis=-1) * scale
        dil = DILATED_GROUPS[grp][1]
        cols = slice(kind * DIL_OUT_WIDTH, (kind + 1) * DIL_OUT_WIDTH)
        if dil == 1:
            outs[grp][0, 0, :, cols] = t.astype(BF16)
        else:
            halves = DIL_OUT_WIDTH // LANES
            for j in range(halves):
                t_ref[j] = t[:, j * LANES:(j + 1) * LANES]
            for r in range(dil):
                outs[grp][0, r, :, cols] = jnp.concatenate(
                    [t_ref[j, pl.ds(r, tm // dil, stride=dil), :] for j in range(halves)],
                    axis=-1).astype(BF16)
    qm_ref[...] = (_dot(h_ref[...], w_ref[:, 3 * MIX_WIDTH:]) * Q_SCALE).astype(BF16)


def _proj_b(x, batch, seq, g_pre, w_in, positions):
    m, d = x.shape
    tm = TM_PROJ
    tiles_per_seq = seq // tm
    freq, sign = _rope_constants()
    row = lambda w: pl.BlockSpec((tm, w), lambda i: (i, 0))
    width = 3 * DIL_OUT_WIDTH
    group_specs = [_residue_spec(tm, dil, width, tiles_per_seq) for _, dil in DILATED_GROUPS]
    group_shapes = [jax.ShapeDtypeStruct((batch, dil, seq // dil, width), BF16)
                    for _, dil in DILATED_GROUPS]
    return pl.pallas_call(
        _proj_b_body,
        grid=(m // tm,),
        in_specs=[row(d), _const_spec((1, d)), _const_spec(w_in.shape), row(1),
                  _const_spec((1, LANES)), _const_spec((1, LANES))],
        out_specs=group_specs + [row(MEM_WIDTH)],
        out_shape=group_shapes + [jax.ShapeDtypeStruct((m, MEM_WIDTH), BF16)],
        scratch_shapes=[pltpu.VMEM((tm, d), BF16),
                        pltpu.VMEM((DIL_OUT_WIDTH // LANES, tm, LANES), F32)],
        compiler_params=_params("parallel"),
        name="proj_b",
    )(x, g_pre, w_in, positions.reshape(m, 1), freq, sign)


def _dilated_body(qkv_ref, o_ref, lse_ref, *, n_blocks, span, unroll):
    w = DIL_OUT_WIDTH
    qi = lax.broadcasted_iota(jnp.int32, (2 * span, 2 * span), 0) % span
    ki = lax.broadcasted_iota(jnp.int32, (2 * span, 2 * span), 1)
    band = ((ki < span) & (ki >= qi)) | ((ki >= span) & (ki - span <= qi))
    own_only = (lax.broadcasted_iota(jnp.int32, (2 * span, span), 1)
                <= lax.broadcasted_iota(jnp.int32, (2 * span, span), 0) % span)
    first = _first_head_lanes((span, PAIR))
    own_lanes = jnp.concatenate([first, ~first], axis=0)

    def block(r0, has_prev):
        own = pl.ds(r0, span)
        keys = pl.ds(r0 - span, 2 * span) if has_prev else own
        for r in range(qkv_ref.shape[1]):
            for p in range(w // PAIR):
                ql = slice(p * PAIR, (p + 1) * PAIR)
                q2 = _stack_heads(qkv_ref[0, r, own, ql], first)
                k2 = qkv_ref[0, r, keys, w + p * PAIR:w + (p + 1) * PAIR]
                v2 = qkv_ref[0, r, keys, 2 * w + p * PAIR:2 * w + (p + 1) * PAIR]
                s = jnp.where(band if has_prev else own_only, _dot_nt(q2, k2), NEG_INF)
                acc, lse = _softmax_pv(s, v2, with_lse=True)
                o_ref[0, r, own, ql] = jnp.where(first, acc[:span], acc[span:]).astype(BF16)
                lse = jnp.where(own_lanes, lse, 0.0)
                lse_ref[0, r, own, ql] = lse[:span] + lse[span:]

    block(0, False)

    def step(blk, carry):
        block(pl.multiple_of(blk * span, span), True)
        return carry

    lax.fori_loop(1, n_blocks, step, 0, unroll=unroll)


def _dilated_group(qkv, window, dilation):
    b, _, length, width = qkv.shape
    span = window // dilation
    assert length % span == 0
    per_step = min(dilation, DIL_CHAINS)
    unroll = DIL_CHAINS // per_step
    assert dilation % per_step == 0
    out_spec = pl.BlockSpec((1, per_step, length, DIL_OUT_WIDTH), lambda i, r: (i, r, 0, 0))
    return pl.pallas_call(
        functools.partial(_dilated_body, n_blocks=length // span, span=span, unroll=unroll),
        grid=(b, dilation // per_step),
        in_specs=[pl.BlockSpec((1, per_step, length, width), lambda i, r: (i, r, 0, 0))],
        out_specs=[out_spec, out_spec],
        out_shape=[jax.ShapeDtypeStruct((b, dilation, length, DIL_OUT_WIDTH), BF16),
                   jax.ShapeDtypeStruct((b, dilation, length, DIL_OUT_WIDTH), F32)],
        compiler_params=_params("parallel", "parallel"),
        name=f"dilated_{dilation}",
    )(qkv)


def _out_b_body(x_ref, o0_ref, o1_ref, o2_ref, l0_ref, l1_ref, l2_ref, qm_ref, kv_ref, w_ref,
                g_ref, o_ref, *scratch):
    tm = x_ref.shape[0]
    halves = DIL_OUT_WIDTH // LANES

    def to_token_major(ref, buf):
        dil = ref.shape[1]
        for r in range(dil):
            rows = ref[0, r].astype(F32)
            for j in range(halves):
                buf[j, pl.ds(r, tm // dil, stride=dil), :] = rows[:, j * LANES:(j + 1) * LANES]

    for g, (o_g, l_g) in enumerate(((o0_ref, l0_ref), (o1_ref, l1_ref), (o2_ref, l2_ref))):
        to_token_major(o_g, scratch[2 * g])
        to_token_major(l_g, scratch[2 * g + 1])

    for sub in range(tm // OUT_SUB):
        tile = slice(sub * OUT_SUB, (sub + 1) * OUT_SUB)
        read = lambda buf: jnp.concatenate([buf[j, tile, :] for j in range(halves)], axis=-1)
        outs = [read(scratch[2 * g]) for g in range(len(DILATED_GROUPS))]
        lses = [read(scratch[2 * g + 1]) for g in range(len(DILATED_GROUPS))]
        mx = jnp.maximum(jnp.maximum(lses[0], lses[1]), lses[2])
        es = [jnp.exp2(l - mx) for l in lses]
        denom = es[0] + es[1] + es[2]
        merged = sum((e / denom) * o for e, o in zip(es, outs))
        mem_out = _mem_attention(qm_ref[tile, :], kv_ref).astype(BF16)
        y = (_dot(merged.astype(BF16), w_ref[:DIL_OUT_WIDTH, :])
             + _dot(mem_out, w_ref[DIL_OUT_WIDTH:, :]))
        o_ref[tile, :] = x_ref[tile, :] + _rms_norm(y, g_ref[...])


def _out_b(x, seq, outs, lses, qm, kv, w_out, g_post):
    m, d = x.shape
    tm = TM_OUT
    tiles_per_seq = seq // tm
    n_mem = kv.shape[1]
    row = lambda w: pl.BlockSpec((tm, w), lambda i: (i, 0))
    group_specs = [_residue_spec(tm, dil, DIL_OUT_WIDTH, tiles_per_seq)
                   for _, dil in DILATED_GROUPS]
    return pl.pallas_call(
        _out_b_body,
        grid=(m // tm,),
        in_specs=[row(d)] + group_specs * 2 + [
            row(MEM_WIDTH),
            pl.BlockSpec((1, n_mem, 2 * MEM_WIDTH), lambda i: (i // tiles_per_seq, 0, 0)),
            _const_spec(w_out.shape), _const_spec((1, d))],
        out_specs=row(d),
        out_shape=jax.ShapeDtypeStruct((m, d), F32),
        scratch_shapes=[pltpu.VMEM((DIL_OUT_WIDTH // LANES, tm, LANES), F32)]
        * (2 * len(DILATED_GROUPS)),
        compiler_params=_params("parallel"),
        name="out_b",
    )(x, *outs, *lses, qm, kv, w_out, g_post)


def _split3(x):
    hi = x.astype(BF16)
    r1 = x - hi.astype(F32)
    mid = r1.astype(BF16)
    lo = (r1 - mid.astype(F32)).astype(BF16)
    return hi, mid, lo


N_BIAS_TERMS = 3
AUG_WIDTH = N_MIX_HEADS * LANES
V_ROWS = HEAD_DIM + BF16_SUBLANES


def _pack_terms(hi, mid, lo):
    return (hi.astype(F32) + pltpu.roll(mid.astype(F32), N_MIX_HEADS, axis=1)
            + pltpu.roll(lo.astype(F32), 2 * N_MIX_HEADS, axis=1)).astype(BF16)


def _bias_selectors():
    rows = jnp.arange(LANES)
    term, head = rows // N_MIX_HEADS, rows % N_MIX_HEADS
    valid = rows < N_BIAS_TERMS * N_MIX_HEADS
    cols = jnp.arange(MIX_WIDTH)
    base = (head // 2) * PAIR + jnp.where(head % 2 == 0, HEAD_DIM, 0)

    def select(offset):
        target = base + offset + term
        return ((cols[None, :] == target[:, None]) & valid[:, None]).astype(BF16)

    def ones(offset):
        lane = cols % HEAD_DIM
        return ((lane >= offset) & (lane < offset + N_BIAS_TERMS)).astype(F32)[None, :]

    q_sel, k_sel = select(0), select(N_BIAS_TERMS)
    q_one, k_one = ones(N_BIAS_TERMS), ones(0)
    return q_sel, k_sel, q_one, k_one


def _proj_c_body(x_ref, g_ref, w_ref, wvt_ref, wf_ref, wqm_ref, fb_ref, qsel_ref, ksel_ref,
                 qone_ref, kone_ref, q_ref, k_ref, vt_ref, qm_ref, h_ref, carry_ref, qext_ref,
                 kext_ref, *, tiles_per_seq):
    ts = TK_FOX

    @pl.when(pl.program_id(0) % tiles_per_seq == 0)
    def _():
        carry_ref[...] = jnp.zeros_like(carry_ref)

    lane = lax.broadcasted_iota(jnp.int32, (ts, LANES), 1)
    tri = (lax.broadcasted_iota(jnp.int32, (ts, ts), 1)
           <= lax.broadcasted_iota(jnp.int32, (ts, ts), 0)).astype(BF16)
    first = _first_head_lanes((ts, PAIR))
    ones_row = (lax.broadcasted_iota(jnp.int32, (V_ROWS - HEAD_DIM, ts), 0) == 0).astype(BF16)

    for sub in range(x_ref.shape[0] // ts):
        tile = slice(sub * ts, (sub + 1) * ts)
        h_ref[tile, :] = _rms_norm(x_ref[tile, :], g_ref[...]).astype(BF16)

        z = _dot(h_ref[tile, :], wf_ref[...]) + fb_ref[...]
        log_f = jnp.minimum(z, 0.0) - jnp.log1p(jnp.exp(-jnp.abs(z)))
        log_f = jnp.where(lane < N_MIX_HEADS, log_f, 0.0)
        sums = _dot(tri, _pack_terms(*_split3(log_f)))
        local = (sums + pltpu.roll(sums, LANES - N_MIX_HEADS, axis=1)
                 + pltpu.roll(sums, LANES - 2 * N_MIX_HEADS, axis=1))
        c = carry_ref[...] + jnp.where(lane < N_MIX_HEADS, local, 0.0)
        carry_ref[...] = c[ts - 1:ts, :]
        terms = _pack_terms(*_split3(c * LOG2_E))
        qext_ref[tile, :] = _dot(terms, qsel_ref[...]) + qone_ref[...]
        kext_ref[tile, :] = kone_ref[...] - _dot(terms, ksel_ref[...])

        for c_idx in range(2 * MIX_WIDTH // MXU_N):
            lo = c_idx * MXU_N
            is_q = lo < MIX_WIDTH
            t = _dot(h_ref[tile, :], w_ref[:, lo:lo + MXU_N])
            if is_q:
                t = t * Q_SCALE
            dst, ext_ref = (q_ref, qext_ref) if is_q else (k_ref, kext_ref)
            for j in range(MXU_N // PAIR):
                p = (lo % MIX_WIDTH) // PAIR + j
                pair = t[:, j * PAIR:(j + 1) * PAIR]
                ext = ext_ref[tile, p * PAIR:(p + 1) * PAIR]
                dst[tile, 2 * p * LANES:(2 * p + 1) * LANES] = jnp.where(
                    first, pair, ext).astype(BF16)
                dst[tile, (2 * p + 1) * LANES:(2 * p + 2) * LANES] = pltpu.roll(
                    jnp.where(first, ext, pair), HEAD_DIM, axis=1).astype(BF16)
        for c_idx in range(MIX_WIDTH // MXU_N):
            lo = c_idx * MXU_N
            vt = _dot_nt(wvt_ref[lo:lo + MXU_N, :], h_ref[tile, :]).astype(BF16)
            for j in range(MXU_N // HEAD_DIM):
                head = lo // HEAD_DIM + j
                vt_ref[sub, head * V_ROWS:head * V_ROWS + HEAD_DIM, :] = (
                    vt[j * HEAD_DIM:(j + 1) * HEAD_DIM])
                vt_ref[sub, head * V_ROWS + HEAD_DIM:(head + 1) * V_ROWS, :] = ones_row
        qm_ref[tile, :] = (_dot(h_ref[tile, :], wqm_ref[...]) * Q_SCALE).astype(BF16)


def _proj_c(x, seq, g_pre, w_in, forget_bias):
    m, d = x.shape
    tm = TM_PROJ_C
    v_tiles = tm // TK_FOX
    qkv_w = 3 * MIX_WIDTH
    w_v_t = w_in[:, 2 * MIX_WIDTH:qkv_w].T
    w_f = jnp.pad(w_in[:, qkv_w:qkv_w + N_MIX_HEADS], ((0, 0), (0, LANES - N_MIX_HEADS)))
    w_qm = w_in[:, qkv_w + N_MIX_HEADS:]
    fb = jnp.concatenate([forget_bias.astype(F32), jnp.zeros((LANES - N_MIX_HEADS,), F32)])
    consts = _bias_selectors()
    row = lambda w_: pl.BlockSpec((tm, w_), lambda i: (i, 0))
    return pl.pallas_call(
        functools.partial(_proj_c_body, tiles_per_seq=seq // tm),
        grid=(m // tm,),
        in_specs=[row(d), _const_spec((1, d)), _const_spec(w_in.shape), _const_spec(w_v_t.shape),
                  _const_spec(w_f.shape), _const_spec(w_qm.shape), _const_spec((1, LANES))]
        + [_const_spec(c.shape) for c in consts],
        out_specs=[row(AUG_WIDTH), row(AUG_WIDTH),
                   pl.BlockSpec((v_tiles, N_MIX_HEADS * V_ROWS, TK_FOX), lambda i: (i, 0, 0)),
                   row(MEM_WIDTH)],
        out_shape=[jax.ShapeDtypeStruct((m, AUG_WIDTH), BF16),
                   jax.ShapeDtypeStruct((m, AUG_WIDTH), BF16),
                   jax.ShapeDtypeStruct((m // TK_FOX, N_MIX_HEADS * V_ROWS, TK_FOX), BF16),
                   jax.ShapeDtypeStruct((m, MEM_WIDTH), BF16)],
        scratch_shapes=[pltpu.VMEM((tm, d), BF16), pltpu.VMEM((1, LANES), F32),
                        pltpu.VMEM((tm, MIX_WIDTH), F32), pltpu.VMEM((tm, MIX_WIDTH), F32)],
        compiler_params=_params("arbitrary"),
        name="proj_c",
    )(x, g_pre, w_in, w_v_t, w_f, w_qm, fb.reshape(1, LANES), *consts)


def _fox_body(q_ref, qnext_ref, k_ref, vt_ref, o_ref, acc_ref, s0_ref, s1_ref, s2_ref,
              mx0_ref, mx1_ref, mx2_ref, *, tq, n_q):
    i = pl.program_id(2)
    causal = (lax.broadcasted_iota(jnp.int32, (tq, tq), 1)
              >= lax.broadcasted_iota(jnp.int32, (tq, tq), 0))
    acc_ref[...] = jnp.zeros_like(acc_ref)

    def scores(queries_ref, j, buf, diagonal):
        s_ref, mx_ref = buf
        rows = pl.ds(pl.multiple_of(j * tq, tq), tq)
        for h in range(FOX_HEADS):
            lanes = slice(h * LANES, (h + 1) * LANES)
            st = _dot_nt(k_ref[0, rows, lanes], queries_ref[0, :, lanes])
            if diagonal:
                st = jnp.where(causal, st, NEG_INF)
            s_ref[h] = st
            mx_ref[h] = jnp.max(st, axis=0, keepdims=True)

    def consume(j, buf, m_run):
        s_ref, mx_ref = buf
        m_out = []
        for h in range(FOX_HEADS):
            m_new = jnp.maximum(m_run[h], mx_ref[h])
            alpha = jnp.exp2(m_run[h] - m_new)
            e = jnp.exp2(s_ref[h] - m_new).astype(BF16)
            acc_ref[h] = alpha * acc_ref[h] + _dot(vt_ref[j, h * V_ROWS:(h + 1) * V_ROWS, :], e)
            m_out.append(m_new)
        return tuple(m_out)

    def run(diag, other, handover):
        @pl.when(i == 0)
        def _():
            scores(q_ref, i, diag, True)

        def prefetch_next_tile():
            scores(qnext_ref, jnp.minimum(i + 1, n_q - 1), handover, True)

        def two_tiles(t, m_run):
            j = i - 2 * t
            scores(q_ref, j - 1, other, False)
            m_run = consume(j, diag, m_run)
            scores(q_ref, j - 2, diag, False)
            return consume(j - 1, other, m_run)

        def last_two(m_run):
            scores(q_ref, 0, other, False)
            m_run = consume(1, diag, m_run)
            prefetch_next_tile()
            return consume(0, other, m_run)

        def last_one(m_run):
            prefetch_next_tile()
            return consume(0, diag, m_run)

        m_run = tuple(jnp.full((1, tq), NEG_INF, F32) for _ in range(FOX_HEADS))
        m_run = lax.fori_loop(0, i // 2, two_tiles, m_run)
        lax.cond(i % 2 == 1, last_two, last_one, m_run)

    bufs = ((s0_ref, mx0_ref), (s1_ref, mx1_ref), (s2_ref, mx2_ref))
    lax.cond(i % 2 == 0,
             lambda: run(bufs[0], bufs[1], bufs[2]),
             lambda: run(bufs[2], bufs[1], bufs[0]))
    for p in range(FOX_HEADS // 2):
        halves = []
        for h in (2 * p, 2 * p + 1):
            halves.append(acc_ref[h, :HEAD_DIM, :] / acc_ref[h, HEAD_DIM:HEAD_DIM + 1, :])
        o_ref[0, :, p * PAIR:(p + 1) * PAIR] = jnp.concatenate(halves, axis=0).T.astype(BF16)


def _fox_attention(q_aug, k_aug, v_t, batch, seq):
    tq = tk = TK_FOX
    q3 = q_aug.reshape(batch, seq, AUG_WIDTH)
    k3 = k_aug.reshape(batch, seq, AUG_WIDTH)
    aug_w = FOX_HEADS * LANES
    v_w = FOX_HEADS * HEAD_DIM
    score_buf = pltpu.VMEM((FOX_HEADS, tk, tq), F32)
    max_buf = pltpu.VMEM((FOX_HEADS, 1, tq), F32)
    n_q = seq // tq
    out = pl.pallas_call(
        functools.partial(_fox_body, tq=tq, n_q=n_q),
        grid=(batch, N_MIX_HEADS // FOX_HEADS, n_q),
        in_specs=[pl.BlockSpec((1, tq, aug_w), lambda b, p, i: (b, i, p)),
                  pl.BlockSpec((1, tq, aug_w), lambda b, p, i: (b, jnp.minimum(i + 1, n_q - 1), p)),
                  pl.BlockSpec((1, seq, aug_w), lambda b, p, i: (b, 0, p)),
                  pl.BlockSpec((seq // tk, FOX_HEADS * V_ROWS, tk), lambda b, p, i: (b, p, 0))],
        out_specs=pl.BlockSpec((1, tq, v_w), lambda b, p, i: (b, i, p)),
        out_shape=jax.ShapeDtypeStruct((batch, seq, MIX_WIDTH), BF16),
        scratch_shapes=[pltpu.VMEM((FOX_HEADS, V_ROWS, tq), F32)] + [score_buf] * 3
        + [max_buf] * 3,
        compiler_params=_params("parallel", "arbitrary", "arbitrary"),
        name="fox_attention",
    )(q3, q3, k3, v_t)
    return out.reshape(batch * seq, MIX_WIDTH)


def _out_c_body(x_ref, att_ref, qm_ref, kv_ref, w_ref, g_ref, o_ref):
    for sub in range(x_ref.shape[0] // OUT_SUB):
        tile = slice(sub * OUT_SUB, (sub + 1) * OUT_SUB)
        mem_out = _mem_attention(qm_ref[tile, :], kv_ref).astype(BF16)
        y = _dot(att_ref[tile, :], w_ref[:MIX_WIDTH, :]) + _dot(mem_out, w_ref[MIX_WIDTH:, :])
        o_ref[tile, :] = x_ref[tile, :] + _rms_norm(y, g_ref[...])


def _out_c(x, seq, att, qm, kv, w_out, g_post):
    m, d = x.shape
    tm = TM_OUT
    tiles_per_seq = seq // tm
    n_mem = kv.shape[1]
    row = lambda w: pl.BlockSpec((tm, w), lambda i: (i, 0))
    return pl.pallas_call(
        _out_c_body,
        grid=(m // tm,),
        in_specs=[row(d), row(MIX_WIDTH), row(MEM_WIDTH),
                  pl.BlockSpec((1, n_mem, 2 * MEM_WIDTH), lambda i: (i // tiles_per_seq, 0, 0)),
                  _const_spec(w_out.shape), _const_spec((1, d))],
        out_specs=row(d),
        out_shape=jax.ShapeDtypeStruct((m, d), F32),
        compiler_params=_params("parallel"),
        name="out_c",
    )(x, att, qm, kv, w_out, g_post)


def kernel(x, mem, positions, norm_g, mem_norm_g, w_mem_kv, ffn_w_gate_up, ffn_w_down,
           a_w_in, a_spatial_w, a_spatial_b, a_v_norm_g, a_w_out,
           b_w_in, b_w_out, c_w_in, c_forget_bias, c_w_out):
    batch, seq, d = x.shape
    depth = norm_g.shape[0]
    bf = lambda w: w.astype(BF16)

    kv_all = _mem_kv(mem, mem_norm_g, bf(w_mem_kv))
    w_gate_up, w_down = bf(ffn_w_gate_up), bf(ffn_w_down)
    xf = x.reshape(batch * seq, d)
    for i in range(depth):
        kind, j = i % 3, i // 3
        g = norm_g[i].reshape(norm_g.shape[1], 1, d)
        xf = _ffn(xf, g[0], g[1], w_gate_up, w_down, (i, 0))
        kv = kv_all[i]
        if kind == 0:
            xf = _mixer_a(xf, seq, g[2], g[3], kv, bf(a_w_in[j]), bf(a_spatial_w[j]),
                          a_spatial_b[j], a_v_norm_g[j], bf(a_w_out[j]))
        elif kind == 1:
            *groups, qm = _proj_b(xf, batch, seq, g[2], bf(b_w_in[j]), positions)
            outs, lses = zip(*[_dilated_group(qkv_g, w, dil)
                               for qkv_g, (w, dil) in zip(groups, DILATED_GROUPS)])
            xf = _out_b(xf, seq, outs, lses, qm, kv, bf(b_w_out[j]), g[3])
        else:
            q_aug, k_aug, v_t, qm = _proj_c(xf, seq, g[2], bf(c_w_in[j]), c_forget_bias[j])
            att = _fox_attention(q_aug, k_aug, v_t, batch, seq)
            xf = _out_c(xf, seq, att, qm, kv, bf(c_w_out[j]), g[3])
        xf = _ffn(xf, g[4], g[5], w_gate_up, w_down, (i, 1))
    return xf.reshape(batch, seq, d)
```

```python
import functools

import jax
import jax.numpy as jnp
from jax import lax
from jax.experimental import pallas as pl
from jax.experimental.pallas import tpu as pltpu

F32 = jnp.float32
BF16 = jnp.bfloat16

HEAD_DIM = 64
N_MIX_HEADS = 12
MIX_WIDTH = N_MIX_HEADS * HEAD_DIM
N_MEM_HEADS = 4
MEM_WIDTH = N_MEM_HEADS * HEAD_DIM
CHUNK = 128
ROPE_THETA = 10000.0
DILATED_GROUPS = ((128, 1), (512, 4), (2048, 16))
DIL_OUT_WIDTH = (N_MIX_HEADS // len(DILATED_GROUPS)) * HEAD_DIM
RMS_EPS = 1e-6
LN_EPS = 1e-5
NEG_INF = -1e30
ATTN_SCALE = HEAD_DIM ** -0.5
LOG2_E = 1.4426950408889634
Q_SCALE = ATTN_SCALE * LOG2_E

LANES = 128
BF16_SUBLANES = 16
PAIR = 2 * HEAD_DIM
MXU_N = 256
VMEM_LIMIT_BYTES = 56 * 1024 * 1024

TM_FFN = 1024
FFN_SUB = 512
TM_PROJ = 512
TM_MIX_A = 1024
MIX_A_SUB = 512
TM_OUT = 1024
OUT_SUB = 512
TK_FOX = 512
TM_PROJ_C = 1024
FOX_HEADS = 4
DIL_CHAINS = 8
FF_CHUNK = 256


def _params(*semantics):
    return pltpu.CompilerParams(dimension_semantics=semantics,
                                vmem_limit_bytes=VMEM_LIMIT_BYTES)


def _const_spec(shape):
    zeros = (0,) * len(shape)
    return pl.BlockSpec(shape, lambda *_: zeros, pipeline_mode=pl.Buffered(1))


def _rms_norm(x, g):
    ms = jnp.mean(x * x, axis=-1, keepdims=True)
    return x * lax.rsqrt(ms + RMS_EPS) * g


def _dot(a, b):
    return jnp.dot(a, b, preferred_element_type=F32)


def _dot_nt(a, b):
    return lax.dot_general(a, b, (((1,), (1,)), ((), ())), preferred_element_type=F32)


def _first_head_lanes(shape):
    return lax.broadcasted_iota(jnp.int32, shape, len(shape) - 1) % PAIR < HEAD_DIM


def _ffn_body(x_ref, gpre_ref, gpost_ref, wgu_ref, wd_ref, o_ref, xn_ref, act_ref, *, d_ff):
    for sub in range(x_ref.shape[0] // FFN_SUB):
        rows = slice(sub * FFN_SUB, (sub + 1) * FFN_SUB)
        xn_ref[rows, :] = _rms_norm(x_ref[rows, :], gpre_ref[...]).astype(BF16)
        for c in range(d_ff // FF_CHUNK):
            lo = c * FF_CHUNK
            gate = _dot(xn_ref[rows, :], wgu_ref[:, lo:lo + FF_CHUNK])
            up = _dot(xn_ref[rows, :], wgu_ref[:, d_ff + lo:d_ff + lo + FF_CHUNK])
            act_ref[rows, lo:lo + FF_CHUNK] = (gate * jax.nn.sigmoid(gate) * up).astype(BF16)
        y = _dot(act_ref[rows, :], wd_ref[...])
        o_ref[rows, :] = x_ref[rows, :] + 0.5 * _rms_norm(y, gpost_ref[...])


def _stacked_spec(stacked, index):
    k = len(index)
    tail = stacked.shape[k:]
    where = tuple(index) + (0,) * len(tail)
    return pl.BlockSpec((None,) * k + tail, lambda *_: where, pipeline_mode=pl.Buffered(1))


def _ffn(x, g_pre, g_post, w_gate_up, w_down, index):
    m, d = x.shape
    d_ff = w_down.shape[-2]
    assert m % TM_FFN == 0 and d_ff % FF_CHUNK == 0
    row = pl.BlockSpec((TM_FFN, d), lambda i: (i, 0))
    return pl.pallas_call(
        functools.partial(_ffn_body, d_ff=d_ff),
        grid=(m // TM_FFN,),
        in_specs=[row, _const_spec((1, d)), _const_spec((1, d)),
                  _stacked_spec(w_gate_up, index), _stacked_spec(w_down, index)],
        out_specs=row,
        out_shape=jax.ShapeDtypeStruct((m, d), F32),
        scratch_shapes=[pltpu.VMEM((TM_FFN, d), BF16), pltpu.VMEM((TM_FFN, d_ff), BF16)],
        compiler_params=_params("parallel"),
        name="ffn",
    )(x, g_pre, g_post, w_gate_up, w_down)


def _mem_kv_body(mem_ref, g_ref, w_ref, o_ref):
    for i in range(mem_ref.shape[0]):
        o_ref[0, i] = _dot(_rms_norm(mem_ref[i], g_ref[0]).astype(BF16), w_ref[0]).astype(BF16)


def _mem_kv(mem, mem_norm_g, w_mem_kv):
    b, n_mem, d = mem.shape
    depth = w_mem_kv.shape[0]
    return pl.pallas_call(
        _mem_kv_body,
        grid=(depth,),
        in_specs=[_const_spec(mem.shape),
                  pl.BlockSpec((1, 1, d), lambda l: (l, 0, 0)),
                  pl.BlockSpec((1, d, 2 * MEM_WIDTH), lambda l: (l, 0, 0))],
        out_specs=pl.BlockSpec((1, b, n_mem, 2 * MEM_WIDTH), lambda l: (l, 0, 0, 0)),
        out_shape=jax.ShapeDtypeStruct((depth, b, n_mem, 2 * MEM_WIDTH), BF16),
        compiler_params=_params("parallel"),
        name="mem_kv",
    )(mem, mem_norm_g.reshape(depth, 1, d), w_mem_kv)


def _stack_heads(qp, first):
    zero = jnp.zeros_like(qp)
    return jnp.concatenate([jnp.where(first, qp, zero), jnp.where(first, zero, qp)], axis=0)


def _softmax_pv(s, v, with_lse=False):
    mx = jnp.max(s, axis=-1, keepdims=True)
    e = jnp.exp2(s - mx)
    denom = jnp.sum(e, axis=-1, keepdims=True)
    out = _dot(e.astype(BF16), v) / denom
    if with_lse:
        return out, mx + jnp.log2(denom)
    return out


def _mem_attention(qm, kv_ref):
    tm = qm.shape[0]
    outs = []
    for p in range(MEM_WIDTH // PAIR):
        qp = qm[:, p * PAIR:(p + 1) * PAIR]
        k = kv_ref[0, :, p * PAIR:(p + 1) * PAIR]
        v = kv_ref[0, :, MEM_WIDTH + p * PAIR:MEM_WIDTH + (p + 1) * PAIR]
        first = _first_head_lanes(qp.shape)
        acc = _softmax_pv(_dot_nt(_stack_heads(qp, first), k), v)
        outs.append(jnp.where(first, acc[:tm], acc[tm:]))
    return jnp.concatenate(outs, axis=-1)


def _mixer_a_body(x_ref, gpre_ref, win_ref, vg_ref, sw_ref, sb_ref, kv_ref, wout_ref, gpost_ref,
                  o_ref, uv_ref, v_ref, gated_ref, wc_ref, h_ref):
    t_idx = lax.broadcasted_iota(jnp.int32, (CHUNK, CHUNK), 0)
    s_idx = lax.broadcasted_iota(jnp.int32, (CHUNK, CHUNK), 1)
    for g in range(N_MIX_HEADS):
        wc_ref[g] = jnp.where(s_idx <= t_idx, sw_ref[g], jnp.zeros((CHUNK, CHUNK), BF16))
    first = _first_head_lanes((CHUNK, PAIR))

    for sub in range(x_ref.shape[0] // MIX_A_SUB):
        tile = slice(sub * MIX_A_SUB, (sub + 1) * MIX_A_SUB)
        h_ref[tile, :] = _rms_norm(x_ref[tile, :], gpre_ref[...]).astype(BF16)
        for c in range(2 * MIX_WIDTH // MXU_N):
            lo = c * MXU_N
            uv_ref[tile, lo:lo + MXU_N] = jax.nn.gelu(
                _dot(h_ref[tile, :], win_ref[:, lo:lo + MXU_N]))
        v = uv_ref[tile, MIX_WIDTH:]
        mu = jnp.mean(v, axis=-1, keepdims=True)
        vc = v - mu
        var = jnp.mean(vc * vc, axis=-1, keepdims=True)
        v_ref[tile, :] = (vc * lax.rsqrt(var + LN_EPS) * vg_ref[...]).astype(BF16)
        qm = (_dot(h_ref[tile, :], win_ref[:, 2 * MIX_WIDTH:]) * Q_SCALE).astype(BF16)

        for c in range(sub * MIX_A_SUB // CHUNK, (sub + 1) * MIX_A_SUB // CHUNK):
            rows = slice(c * CHUNK, (c + 1) * CHUNK)
            for p in range(MIX_WIDTH // PAIR):
                lanes = slice(p * PAIR, (p + 1) * PAIR)
                vp = v_ref[rows, lanes]
                mixed = jnp.where(first, _dot(wc_ref[2 * p], vp), _dot(wc_ref[2 * p + 1], vp))
                gated_ref[rows, lanes] = (uv_ref[rows, lanes]
                                          * (mixed + sb_ref[:, lanes])).astype(BF16)

        mem_out = _mem_attention(qm, kv_ref).astype(BF16)
        y = (_dot(gated_ref[tile, :], wout_ref[:MIX_WIDTH, :])
             + _dot(mem_out, wout_ref[MIX_WIDTH:, :]))
        o_ref[tile, :] = x_ref[tile, :] + _rms_norm(y, gpost_ref[...])


def _mixer_a(x, seq, g_pre, g_post, kv, w_in, spatial_w, spatial_b, v_norm_g, w_out):
    m, d = x.shape
    tm = TM_MIX_A
    assert seq % tm == 0 and tm % CHUNK == 0
    tiles_per_seq = seq // tm
    n_mem = kv.shape[1]
    row = pl.BlockSpec((tm, d), lambda i: (i, 0))
    bias = jnp.repeat(spatial_b.T, HEAD_DIM, axis=1)
    return pl.pallas_call(
        _mixer_a_body,
        grid=(m // tm,),
        in_specs=[row, _const_spec((1, d)), _const_spec(w_in.shape), _const_spec((1, MIX_WIDTH)),
                  _const_spec(spatial_w.shape), _const_spec(bias.shape),
                  pl.BlockSpec((1, n_mem, 2 * MEM_WIDTH), lambda i: (i // tiles_per_seq, 0, 0)),
                  _const_spec(w_out.shape), _const_spec((1, d))],
        out_specs=row,
        out_shape=jax.ShapeDtypeStruct((m, d), F32),
        scratch_shapes=[pltpu.VMEM((tm, 2 * MIX_WIDTH), F32), pltpu.VMEM((tm, MIX_WIDTH), BF16),
                        pltpu.VMEM((tm, MIX_WIDTH), BF16),
                        pltpu.VMEM((N_MIX_HEADS, CHUNK, CHUNK), BF16),
                        pltpu.VMEM((tm, d), BF16)],
        compiler_params=_params("parallel"),
        name="mixer_a",
    )(x, g_pre, w_in, v_norm_g.reshape(1, MIX_WIDTH), spatial_w, bias, kv, w_out, g_post)


def _rope_constants():
    inv_freq = ROPE_THETA ** (-jnp.arange(0, HEAD_DIM, 2, dtype=F32) / HEAD_DIM)
    freq = jnp.tile(inv_freq, LANES // (HEAD_DIM // 2)).reshape(1, LANES)
    half = jnp.concatenate([-jnp.ones((HEAD_DIM // 2,), F32), jnp.ones((HEAD_DIM // 2,), F32)])
    sign = jnp.tile(half, LANES // HEAD_DIM).reshape(1, LANES)
    return freq, sign


def _rope(t, cos, sin_signed):
    first_half = lax.broadcasted_iota(jnp.int32, t.shape, 1) % HEAD_DIM < HEAD_DIM // 2
    swapped = jnp.where(first_half,
                        pltpu.roll(t, LANES - HEAD_DIM // 2, axis=1),
                        pltpu.roll(t, HEAD_DIM // 2, axis=1))
    return t * cos + swapped * sin_signed


def _residue_spec(tm, dilation, width, tiles_per_seq):
    return pl.BlockSpec((1, dilation, tm // dilation, width),
                        lambda i: (i // tiles_per_seq, 0, i % tiles_per_seq, 0))


def _proj_b_body(x_ref, g_ref, w_ref, pos_ref, freq_ref, sign_ref, g0_ref, g1_ref, g2_ref, qm_ref,
                 h_ref, t_ref):
    tm = x_ref.shape[0]
    h_ref[...] = _rms_norm(x_ref[...], g_ref[...]).astype(BF16)
    angle = pos_ref[...].astype(F32) * freq_ref[...]
    cos = jnp.cos(angle)
    sin = jnp.sin(angle) * sign_ref[...]
    outs = (g0_ref, g1_ref, g2_ref)
    n_groups = len(DILATED_GROUPS)
    for c in range(3 * n_groups):
        kind, grp = divmod(c, n_groups)
        lo = c * DIL_OUT_WIDTH
        t = _dot(h_ref[...], w_ref[:, lo:lo + DIL_OUT_WIDTH])
        if kind < 2:
            scale = Q_SCALE if kind == 0 else 1.0
            t = jnp.concatenate(
                [_rope(t[:, j * LANES:(j + 1) * LANES], cos, sin)
                 for j in range(DIL_OUT_WIDTH // LANES)], axis=-1) * scale
        dil = DILATED_GROUPS[grp][1]
        cols = slice(kind * DIL_OUT_WIDTH, (kind + 1) * DIL_OUT_WIDTH)
        if dil == 1:
            outs[grp][0, 0, :, cols] = t.astype(BF16)
        else:
            halves = DIL_OUT_WIDTH // LANES
            for j in range(halves):
                t_ref[j] = t[:, j * LANES:(j + 1) * LANES]
            for r in range(dil):
                outs[grp][0, r, :, cols] = jnp.concatenate(
                    [t_ref[j, pl.ds(r, tm // dil, stride=dil), :] for j in range(halves)],
                    axis=-1).astype(BF16)
    qm_ref[...] = (_dot(h_ref[...], w_ref[:, 3 * MIX_WIDTH:]) * Q_SCALE).astype(BF16)


def _proj_b(x, batch, seq, g_pre, w_in, positions):
    m, d = x.shape
    tm = TM_PROJ
    tiles_per_seq = seq // tm
    freq, sign = _rope_constants()
    row = lambda w: pl.BlockSpec((tm, w), lambda i: (i, 0))
    width = 3 * DIL_OUT_WIDTH
    group_specs = [_residue_spec(tm, dil, width, tiles_per_seq) for _, dil in DILATED_GROUPS]
    group_shapes = [jax.ShapeDtypeStruct((batch, dil, seq // dil, width), BF16)
                    for _, dil in DILATED_GROUPS]
    return pl.pallas_call(
        _proj_b_body,
        grid=(m // tm,),
        in_specs=[row(d), _const_spec((1, d)), _const_spec(w_in.shape), row(1),
                  _const_spec((1, LANES)), _const_spec((1, LANES))],
        out_specs=group_specs + [row(MEM_WIDTH)],
        out_shape=group_shapes + [jax.ShapeDtypeStruct((m, MEM_WIDTH), BF16)],
        scratch_shapes=[pltpu.VMEM((tm, d), BF16),
                        pltpu.VMEM((DIL_OUT_WIDTH // LANES, tm, LANES), F32)],
        compiler_params=_params("parallel"),
        name="proj_b",
    )(x, g_pre, w_in, positions.reshape(m, 1), freq, sign)


def _dilated_body(qkv_ref, o_ref, lse_ref, *, n_blocks, span, unroll):
    w = DIL_OUT_WIDTH
    qi = lax.broadcasted_iota(jnp.int32, (2 * span, 2 * span), 0) % span
    ki = lax.broadcasted_iota(jnp.int32, (2 * span, 2 * span), 1)
    band = ((ki < span) & (ki >= qi)) | ((ki >= span) & (ki - span <= qi))
    own_only = (lax.broadcasted_iota(jnp.int32, (2 * span, span), 1)
                <= lax.broadcasted_iota(jnp.int32, (2 * span, span), 0) % span)
    first = _first_head_lanes((span, PAIR))
    own_lanes = jnp.concatenate([first, ~first], axis=0)

    def block(r0, has_prev):
        own = pl.ds(r0, span)
        keys = pl.ds(r0 - span, 2 * span) if has_prev else own
        for r in range(qkv_ref.shape[1]):
            for p in range(w // PAIR):
                ql = slice(p * PAIR, (p + 1) * PAIR)
                q2 = _stack_heads(qkv_ref[0, r, own, ql], first)
                k2 = qkv_ref[0, r, keys, w + p * PAIR:w + (p + 1) * PAIR]
                v2 = qkv_ref[0, r, keys, 2 * w + p * PAIR:2 * w + (p + 1) * PAIR]
                s = jnp.where(band if has_prev else own_only, _dot_nt(q2, k2), NEG_INF)
                acc, lse = _softmax_pv(s, v2, with_lse=True)
                o_ref[0, r, own, ql] = jnp.where(first, acc[:span], acc[span:]).astype(BF16)
                lse = jnp.where(own_lanes, lse, 0.0)
                lse_ref[0, r, own, ql] = lse[:span] + lse[span:]

    block(0, False)

    def step(blk, carry):
        block(pl.multiple_of(blk * span, span), True)
        return carry

    lax.fori_loop(1, n_blocks, step, 0, unroll=unroll)


def _dilated_group(qkv, window, dilation):
    b, _, length, width = qkv.shape
    span = window // dilation
    assert length % span == 0
    per_step = min(dilation, DIL_CHAINS)
    unroll = DIL_CHAINS // per_step
    assert dilation % per_step == 0
    out_spec = pl.BlockSpec((1, per_step, length, DIL_OUT_WIDTH), lambda i, r: (i, r, 0, 0))
    return pl.pallas_call(
        functools.partial(_dilated_body, n_blocks=length // span, span=span, unroll=unroll),
        grid=(b, dilation // per_step),
        in_specs=[pl.BlockSpec((1, per_step, length, width), lambda i, r: (i, r, 0, 0))],
        out_specs=[out_spec, out_spec],
        out_shape=[jax.ShapeDtypeStruct((b, dilation, length, DIL_OUT_WIDTH), BF16),
                   jax.ShapeDtypeStruct((b, dilation, length, DIL_OUT_WIDTH), F32)],
        compiler_params=_params("parallel", "parallel"),
        name=f"dilated_{dilation}",
    )(qkv)


def _out_b_body(x_ref, o0_ref, o1_ref, o2_ref, l0_ref, l1_ref, l2_ref, qm_ref, kv_ref, w_ref,
                g_ref, o_ref, *scratch):
    tm = x_ref.shape[0]
    halves = DIL_OUT_WIDTH // LANES

    def to_token_major(ref, buf):
        dil = ref.shape[1]
        for r in range(dil):
            rows = ref[0, r].astype(F32)
            for j in range(halves):
                buf[j, pl.ds(r, tm // dil, stride=dil), :] = rows[:, j * LANES:(j + 1) * LANES]

    for g, (o_g, l_g) in enumerate(((o0_ref, l0_ref), (o1_ref, l1_ref), (o2_ref, l2_ref))):
        to_token_major(o_g, scratch[2 * g])
        to_token_major(l_g, scratch[2 * g + 1])

    for sub in range(tm // OUT_SUB):
        tile = slice(sub * OUT_SUB, (sub + 1) * OUT_SUB)
        read = lambda buf: jnp.concatenate([buf[j, tile, :] for j in range(halves)], axis=-1)
        outs = [read(scratch[2 * g]) for g in range(len(DILATED_GROUPS))]
        lses = [read(scratch[2 * g + 1]) for g in range(len(DILATED_GROUPS))]
        mx = jnp.maximum(jnp.maximum(lses[0], lses[1]), lses[2])
        es = [jnp.exp2(l - mx) for l in lses]
        denom = es[0] + es[1] + es[2]
        merged = sum((e / denom) * o for e, o in zip(es, outs))
        mem_out = _mem_attention(qm_ref[tile, :], kv_ref).astype(BF16)
        y = (_dot(merged.astype(BF16), w_ref[:DIL_OUT_WIDTH, :])
             + _dot(mem_out, w_ref[DIL_OUT_WIDTH:, :]))
        o_ref[tile, :] = x_ref[tile, :] + _rms_norm(y, g_ref[...])


def _out_b(x, seq, outs, lses, qm, kv, w_out, g_post):
    m, d = x.shape
    tm = TM_OUT
    tiles_per_seq = seq // tm
    n_mem = kv.shape[1]
    row = lambda w: pl.BlockSpec((tm, w), lambda i: (i, 0))
    group_specs = [_residue_spec(tm, dil, DIL_OUT_WIDTH, tiles_per_seq)
                   for _, dil in DILATED_GROUPS]
    return pl.pallas_call(
        _out_b_body,
        grid=(m // tm,),
        in_specs=[row(d)] + group_specs * 2 + [
            row(MEM_WIDTH),
            pl.BlockSpec((1, n_mem, 2 * MEM_WIDTH), lambda i: (i // tiles_per_seq, 0, 0)),
            _const_spec(w_out.shape), _const_spec((1, d))],
        out_specs=row(d),
        out_shape=jax.ShapeDtypeStruct((m, d), F32),
        scratch_shapes=[pltpu.VMEM((DIL_OUT_WIDTH // LANES, tm, LANES), F32)]
        * (2 * len(DILATED_GROUPS)),
        compiler_params=_params("parallel"),
        name="out_b",
    )(x, *outs, *lses, qm, kv, w_out, g_post)


def _split3(x):
    hi = x.astype(BF16)
    r1 = x - hi.astype(F32)
    mid = r1.astype(BF16)
    lo = (r1 - mid.astype(F32)).astype(BF16)
    return hi, mid, lo


N_BIAS_TERMS = 3
AUG_WIDTH = N_MIX_HEADS * LANES
V_ROWS = HEAD_DIM + BF16_SUBLANES


def _pack_terms(hi, mid, lo):
    return (hi.astype(F32) + pltpu.roll(mid.astype(F32), N_MIX_HEADS, axis=1)
            + pltpu.roll(lo.astype(F32), 2 * N_MIX_HEADS, axis=1)).astype(BF16)


def _bias_selectors():
    rows = jnp.arange(LANES)
    term, head = rows // N_MIX_HEADS, rows % N_MIX_HEADS
    valid = rows < N_BIAS_TERMS * N_MIX_HEADS
    cols = jnp.arange(MIX_WIDTH)
    base = (head // 2) * PAIR + jnp.where(head % 2 == 0, HEAD_DIM, 0)

    def select(offset):
        target = base + offset + term
        return ((cols[None, :] == target[:, None]) & valid[:, None]).astype(BF16)

    def ones(offset):
        lane = cols % HEAD_DIM
        return ((lane >= offset) & (lane < offset + N_BIAS_TERMS)).astype(F32)[None, :]

    q_sel, k_sel = select(0), select(N_BIAS_TERMS)
    q_one, k_one = ones(N_BIAS_TERMS), ones(0)
    return q_sel, k_sel, q_one, k_one


def _proj_c_body(x_ref, g_ref, w_ref, wvt_ref, wf_ref, wqm_ref, fb_ref, qsel_ref, ksel_ref,
                 qone_ref, kone_ref, q_ref, k_ref, vt_ref, qm_ref, h_ref, carry_ref, qext_ref,
                 kext_ref, *, tiles_per_seq):
    ts = TK_FOX

    @pl.when(pl.program_id(0) % tiles_per_seq == 0)
    def _():
        carry_ref[...] = jnp.zeros_like(carry_ref)

    lane = lax.broadcasted_iota(jnp.int32, (ts, LANES), 1)
    tri = (lax.broadcasted_iota(jnp.int32, (ts, ts), 1)
           <= lax.broadcasted_iota(jnp.int32, (ts, ts), 0)).astype(BF16)
    first = _first_head_lanes((ts, PAIR))
    ones_row = (lax.broadcasted_iota(jnp.int32, (V_ROWS - HEAD_DIM, ts), 0) == 0).astype(BF16)

    for sub in range(x_ref.shape[0] // ts):
        tile = slice(sub * ts, (sub + 1) * ts)
        h_ref[tile, :] = _rms_norm(x_ref[tile, :], g_ref[...]).astype(BF16)

        z = _dot(h_ref[tile, :], wf_ref[...]) + fb_ref[...]
        log_f = jnp.minimum(z, 0.0) - jnp.log1p(jnp.exp(-jnp.abs(z)))
        log_f = jnp.where(lane < N_MIX_HEADS, log_f, 0.0)
        sums = _dot(tri, _pack_terms(*_split3(log_f)))
        local = (sums + pltpu.roll(sums, LANES - N_MIX_HEADS, axis=1)
                 + pltpu.roll(sums, LANES - 2 * N_MIX_HEADS, axis=1))
        c = carry_ref[...] + jnp.where(lane < N_MIX_HEADS, local, 0.0)
        carry_ref[...] = c[ts - 1:ts, :]
        terms = _pack_terms(*_split3(c * LOG2_E))
        qext_ref[tile, :] = _dot(terms, qsel_ref[...]) + qone_ref[...]
        kext_ref[tile, :] = kone_ref[...] - _dot(terms, ksel_ref[...])

        for c_idx in range(2 * MIX_WIDTH // MXU_N):
            lo = c_idx * MXU_N
            is_q = lo < MIX_WIDTH
            t = _dot(h_ref[tile, :], w_ref[:, lo:lo + MXU_N])
            if is_q:
                t = t * Q_SCALE
            dst, ext_ref = (q_ref, qext_ref) if is_q else (k_ref, kext_ref)
            for j in range(MXU_N // PAIR):
                p = (lo % MIX_WIDTH) // PAIR + j
                pair = t[:, j * PAIR:(j + 1) * PAIR]
                ext = ext_ref[tile, p * PAIR:(p + 1) * PAIR]
                dst[tile, 2 * p * LANES:(2 * p + 1) * LANES] = jnp.where(
                    first, pair, ext).astype(BF16)
                dst[tile, (2 * p + 1) * LANES:(2 * p + 2) * LANES] = pltpu.roll(
                    jnp.where(first, ext, pair), HEAD_DIM, axis=1).astype(BF16)
        for c_idx in range(MIX_WIDTH // MXU_N):
            lo = c_idx * MXU_N
            vt = _dot_nt(wvt_ref[lo:lo + MXU_N, :], h_ref[tile, :]).astype(BF16)
            for j in range(MXU_N // HEAD_DIM):
                head = lo // HEAD_DIM + j
                vt_ref[sub, head * V_ROWS:head * V_ROWS + HEAD_DIM, :] = (
                    vt[j * HEAD_DIM:(j + 1) * HEAD_DIM])
                vt_ref[sub, head * V_ROWS + HEAD_DIM:(head + 1) * V_ROWS, :] = ones_row
        qm_ref[tile, :] = (_dot(h_ref[tile, :], wqm_ref[...]) * Q_SCALE).astype(BF16)


def _proj_c(x, seq, g_pre, w_in, forget_bias):
    m, d = x.shape
    tm = TM_PROJ_C
    v_tiles = tm // TK_FOX
    qkv_w = 3 * MIX_WIDTH
    w_v_t = w_in[:, 2 * MIX_WIDTH:qkv_w].T
    w_f = jnp.pad(w_in[:, qkv_w:qkv_w + N_MIX_HEADS], ((0, 0), (0, LANES - N_MIX_HEADS)))
    w_qm = w_in[:, qkv_w + N_MIX_HEADS:]
    fb = jnp.concatenate([forget_bias.astype(F32), jnp.zeros((LANES - N_MIX_HEADS,), F32)])
    consts = _bias_selectors()
    row = lambda w_: pl.BlockSpec((tm, w_), lambda i: (i, 0))
    return pl.pallas_call(
        functools.partial(_proj_c_body, tiles_per_seq=seq // tm),
        grid=(m // tm,),
        in_specs=[row(d), _const_spec((1, d)), _const_spec(w_in.shape), _const_spec(w_v_t.shape),
                  _const_spec(w_f.shape), _const_spec(w_qm.shape), _const_spec((1, LANES))]
        + [_const_spec(c.shape) for c in consts],
        out_specs=[row(AUG_WIDTH), row(AUG_WIDTH),
                   pl.BlockSpec((v_tiles, N_MIX_HEADS * V_ROWS, TK_FOX), lambda i: (i, 0, 0)),
                   row(MEM_WIDTH)],
        out_shape=[jax.ShapeDtypeStruct((m, AUG_WIDTH), BF16),
                   jax.ShapeDtypeStruct((m, AUG_WIDTH), BF16),
                   jax.ShapeDtypeStruct((m // TK_FOX, N_MIX_HEADS * V_ROWS, TK_FOX), BF16),
                   jax.ShapeDtypeStruct((m, MEM_WIDTH), BF16)],
        scratch_shapes=[pltpu.VMEM((tm, d), BF16), pltpu.VMEM((1, LANES), F32),
                        pltpu.VMEM((tm, MIX_WIDTH), F32), pltpu.VMEM((tm, MIX_WIDTH), F32)],
        compiler_params=_params("arbitrary"),
        name="proj_c",
    )(x, g_pre, w_in, w_v_t, w_f, w_qm, fb.reshape(1, LANES), *consts)


def _fox_body(q_ref, qnext_ref, k_ref, vt_ref, o_ref, acc_ref, s0_ref, s1_ref, s2_ref,
              mx0_ref, mx1_ref, mx2_ref, *, tq, n_q):
    i = pl.program_id(2)
    causal = (lax.broadcasted_iota(jnp.int32, (tq, tq), 1)
              >= lax.broadcasted_iota(jnp.int32, (tq, tq), 0))
    acc_ref[...] = jnp.zeros_like(acc_ref)

    def scores(queries_ref, j, buf, diagonal):
        s_ref, mx_ref = buf
        rows = pl.ds(pl.multiple_of(j * tq, tq), tq)
        for h in range(FOX_HEADS):
            lanes = slice(h * LANES, (h + 1) * LANES)
            st = _dot_nt(k_ref[0, rows, lanes], queries_ref[0, :, lanes])
            if diagonal:
                st = jnp.where(causal, st, NEG_INF)
            s_ref[h] = st
            mx_ref[h] = jnp.max(st, axis=0, keepdims=True)

    def consume(j, buf, m_run):
        s_ref, mx_ref = buf
        m_out = []
        for h in range(FOX_HEADS):
            m_new = jnp.maximum(m_run[h], mx_ref[h])
            alpha = jnp.exp2(m_run[h] - m_new)
            e = jnp.exp2(s_ref[h] - m_new).astype(BF16)
            acc_ref[h] = alpha * acc_ref[h] + _dot(vt_ref[j, h * V_ROWS:(h + 1) * V_ROWS, :], e)
            m_out.append(m_new)
        return tuple(m_out)

    def run(diag, other, handover):
        @pl.when(i == 0)
        def _():
            scores(q_ref, i, diag, True)

        def prefetch_next_tile():
            scores(qnext_ref, jnp.minimum(i + 1, n_q - 1), handover, True)

        def two_tiles(t, m_run):
            j = i - 2 * t
            scores(q_ref, j - 1, other, False)
            m_run = consume(j, diag, m_run)
            scores(q_ref, j - 2, diag, False)
            return consume(j - 1, other, m_run)

        def last_two(m_run):
            scores(q_ref, 0, other, False)
            m_run = consume(1, diag, m_run)
            prefetch_next_tile()
            return consume(0, other, m_run)

        def last_one(m_run):
            prefetch_next_tile()
            return consume(0, diag, m_run)

        m_run = tuple(jnp.full((1, tq), NEG_INF, F32) for _ in range(FOX_HEADS))
        m_run = lax.fori_loop(0, i // 2, two_tiles, m_run)
        lax.cond(i % 2 == 1, last_two, last_one, m_run)

    bufs = ((s0_ref, mx0_ref), (s1_ref, mx1_ref), (s2_ref, mx2_ref))
    lax.cond(i % 2 == 0,
             lambda: run(bufs[0], bufs[1], bufs[2]),
             lambda: run(bufs[2], bufs[1], bufs[0]))
    for p in range(FOX_HEADS // 2):
        halves = []
        for h in (2 * p, 2 * p + 1):
            halves.append(acc_ref[h, :HEAD_DIM, :] / acc_ref[h, HEAD_DIM:HEAD_DIM + 1, :])
        o_ref[0, :, p * PAIR:(p + 1) * PAIR] = jnp.concatenate(halves, axis=0).T.astype(BF16)


def _fox_attention(q_aug, k_aug, v_t, batch, seq):
    tq = tk = TK_FOX
    q3 = q_aug.reshape(batch, seq, AUG_WIDTH)
    k3 = k_aug.reshape(batch, seq, AUG_WIDTH)
    aug_w = FOX_HEADS * LANES
    v_w = FOX_HEADS * HEAD_DIM
    score_buf = pltpu.VMEM((FOX_HEADS, tk, tq), F32)
    max_buf = pltpu.VMEM((FOX_HEADS, 1, tq), F32)
    n_q = seq // tq
    out = pl.pallas_call(
        functools.partial(_fox_body, tq=tq, n_q=n_q),
        grid=(batch, N_MIX_HEADS // FOX_HEADS, n_q),
        in_specs=[pl.BlockSpec((1, tq, aug_w), lambda b, p, i: (b, i, p)),
                  pl.BlockSpec((1, tq, aug_w), lambda b, p, i: (b, jnp.minimum(i + 1, n_q - 1), p)),
                  pl.BlockSpec((1, seq, aug_w), lambda b, p, i: (b, 0, p)),
                  pl.BlockSpec((seq // tk, FOX_HEADS * V_ROWS, tk), lambda b, p, i: (b, p, 0))],
        out_specs=pl.BlockSpec((1, tq, v_w), lambda b, p, i: (b, i, p)),
        out_shape=jax.ShapeDtypeStruct((batch, seq, MIX_WIDTH), BF16),
        scratch_shapes=[pltpu.VMEM((FOX_HEADS, V_ROWS, tq), F32)] + [score_buf] * 3
        + [max_buf] * 3,
        compiler_params=_params("parallel", "arbitrary", "arbitrary"),
        name="fox_attention",
    )(q3, q3, k3, v_t)
    return out.reshape(batch * seq, MIX_WIDTH)


def _out_c_body(x_ref, att_ref, qm_ref, kv_ref, w_ref, g_ref, o_ref):
    for sub in range(x_ref.shape[0] // OUT_SUB):
        tile = slice(sub * OUT_SUB, (sub + 1) * OUT_SUB)
        mem_out = _mem_attention(qm_ref[tile, :], kv_ref).astype(BF16)
        y = _dot(att_ref[tile, :], w_ref[:MIX_WIDTH, :]) + _dot(mem_out, w_ref[MIX_WIDTH:, :])
        o_ref[tile, :] = x_ref[tile, :] + _rms_norm(y, g_ref[...])


def _out_c(x, seq, att, qm, kv, w_out, g_post):
    m, d = x.shape
    tm = TM_OUT
    tiles_per_seq = seq // tm
    n_mem = kv.shape[1]
    row = lambda w: pl.BlockSpec((tm, w), lambda i: (i, 0))
    return pl.pallas_call(
        _out_c_body,
        grid=(m // tm,),
        in_specs=[row(d), row(MIX_WIDTH), row(MEM_WIDTH),
                  pl.BlockSpec((1, n_mem, 2 * MEM_WIDTH), lambda i: (i // tiles_per_seq, 0, 0)),
                  _const_spec(w_out.shape), _const_spec((1, d))],
        out_specs=row(d),
        out_shape=jax.ShapeDtypeStruct((m, d), F32),
        compiler_params=_params("parallel"),
        name="out_c",
    )(x, att, qm, kv, w_out, g_post)


def kernel(x, mem, positions, norm_g, mem_norm_g, w_mem_kv, ffn_w_gate_up, ffn_w_down,
           a_w_in, a_spatial_w, a_spatial_b, a_v_norm_g, a_w_out,
           b_w_in, b_w_out, c_w_in, c_forget_bias, c_w_out):
    batch, seq, d = x.shape
    depth = norm_g.shape[0]
    bf = lambda w: w.astype(BF16)

    kv_all = _mem_kv(mem, mem_norm_g, bf(w_mem_kv))
    w_gate_up, w_down = bf(ffn_w_gate_up), bf(ffn_w_down)
    xf = x.reshape(batch * seq, d)
    for i in range(depth):
        kind, j = i % 3, i // 3
        g = norm_g[i].reshape(norm_g.shape[1], 1, d)
        xf = _ffn(xf, g[0], g[1], w_gate_up, w_down, (i, 0))
        kv = kv_all[i]
        if kind == 0:
            xf = _mixer_a(xf, seq, g[2], g[3], kv, bf(a_w_in[j]), bf(a_spatial_w[j]),
                          a_spatial_b[j], a_v_norm_g[j], bf(a_w_out[j]))
        elif kind == 1:
            *groups, qm = _proj_b(xf, batch, seq, g[2], bf(b_w_in[j]), positions)
            outs, lses = zip(*[_dilated_group(qkv_g, w, dil)
                               for qkv_g, (w, dil) in zip(groups, DILATED_GROUPS)])
            xf = _out_b(xf, seq, outs, lses, qm, kv, bf(b_w_out[j]), g[3])
        else:
            q_aug, k_aug, v_t, qm = _proj_c(xf, seq, g[2], bf(c_w_in[j]), c_forget_bias[j])
            att = _fox_attention(q_aug, k_aug, v_t, batch, seq)
            xf = _out_c(xf, seq, att, qm, kv, bf(c_w_out[j]), g[3])
        xf = _ffn(xf, g[4], g[5], w_gate_up, w_down, (i, 1))
    return xf.reshape(batch, seq, d)
```
